```python
import math
import jax, jax.numpy as jnp
from jax import lax
import numpy as np

D_MODEL = 1024
BATCH = 4
SEQ = 8192
DEPTH = 2
DEC_BATCH = 2
DEC_SEQ = 8192
PAST_LEN = 128

CONV_WIDTH = D_MODEL // 4
CONV_GROUPS = 4
CONV_K = 3
DIFF_WIDTH = D_MODEL // 2
DIFF_HEADS = 4
DIFF_HEAD_DIM = DIFF_WIDTH // (2 * DIFF_HEADS)
HG_WIDTH = D_MODEL // 4
HG_HEADS = 4
HG_HEAD_DIM = HG_WIDTH // HG_HEADS
D_MIX = CONV_WIDTH + DIFF_WIDTH + HG_WIDTH
PROJ_WIDTHS = (CONV_WIDTH, CONV_WIDTH, CONV_WIDTH,
               2 * DIFF_HEADS * DIFF_HEAD_DIM, 2 * DIFF_HEADS * DIFF_HEAD_DIM, DIFF_WIDTH,
               HG_WIDTH, HG_WIDTH, HG_WIDTH, HG_WIDTH, HG_WIDTH)
D_PROJ = 3 * CONV_WIDTH + 2 * (2 * DIFF_HEADS * DIFF_HEAD_DIM) + DIFF_WIDTH + 5 * HG_WIDTH
Q_BLOCK = 128
HG_CHUNK = 64
ROPE_THETA = 10000.0
N_EXPERTS = 16
EC_CAPACITY_FACTOR = 2
D_EXPERT = 1024
NORM_EPS = 1e-6

kernel_name = "hybrid_bidir_encoder_convdiffhgrn2_ec"


def rmsnorm(x, g):
    xf = x.astype(jnp.float32)
    y = xf * lax.rsqrt(jnp.mean(xf * xf, axis=-1, keepdims=True) + NORM_EPS)
    return (y * g.astype(jnp.float32)).astype(x.dtype)


def rope_tables(seq, dim):
    inv = 1.0 / (ROPE_THETA ** (jnp.arange(0, dim, 2, dtype=jnp.float32) / dim))
    ang = jnp.arange(seq, dtype=jnp.float32)[:, None] * inv[None, :]
    ang = jnp.concatenate([ang, ang], axis=-1)
    return jnp.cos(ang), jnp.sin(ang)


def apply_rope(x, cos, sin):
    x = x.astype(jnp.float32)
    x1, x2 = jnp.split(x, 2, axis=-1)
    rot = jnp.concatenate([-x2, x1], axis=-1)
    return x * cos + rot * sin


def short_conv_mixer(b, c, h, w):
    z = c * h
    zp = jnp.pad(z, ((0, 0), (1, 1), (0, 0)))
    conv = w[0] * zp[:, :-2] + w[1] * zp[:, 1:-1] + w[2] * zp[:, 2:]
    return b * conv


def diff_attention(q, k, v, lam_params, subln_g, layer_idx):
    B, S, _ = q.shape
    H, d = DIFF_HEADS, DIFF_HEAD_DIM
    q = q.reshape(B, S, H, 2, d)
    k = k.reshape(B, S, H, 2, d)
    v = v.reshape(B, S, H, 2 * d)
    cos, sin = rope_tables(S, d)
    cos, sin = cos[:, None, None, :], sin[:, None, None, :]
    qr = apply_rope(q, cos, sin) * (d ** -0.5)
    kr = apply_rope(k, cos, sin)
    lam_init = 0.8 - 0.6 * math.exp(-0.3 * layer_idx)
    lp = lam_params.astype(jnp.float32)
    lam = jnp.exp(jnp.sum(lp[0] * lp[1])) - jnp.exp(jnp.sum(lp[2] * lp[3])) + lam_init
    kh = kr.transpose(0, 2, 3, 1, 4)
    vh = v.astype(jnp.float32).transpose(0, 2, 1, 3)
    nb = S // Q_BLOCK
    qb = qr.reshape(B, nb, Q_BLOCK, H, 2, d).transpose(1, 0, 3, 4, 2, 5)

    def block(qblk):
        s = jnp.einsum('bhiqd,bhikd->bhiqk', qblk, kh)
        p = jax.nn.softmax(s, axis=-1)
        a = p[:, :, 0] - lam * p[:, :, 1]
        return jnp.einsum('bhqk,bhkd->bhqd', a, vh)

    o = lax.map(block, qb)
    o = o.transpose(1, 0, 3, 2, 4).reshape(B, S, H, 2 * d)
    o = rmsnorm(o, subln_g) * (1.0 - lam_init)
    return o.reshape(B, S, H * 2 * d)


def hgrn2_chunk_scan(q, k, v, logf):
    B, H, S, dk = q.shape
    dv = v.shape[-1]
    nc = S // HG_CHUNK

    def to_chunks(t):
        return t.reshape(B, H, nc, HG_CHUNK, t.shape[-1]).transpose(2, 0, 1, 3, 4)

    qc, kc, vc, gc = to_chunks(q), to_chunks(k), to_chunks(v), to_chunks(logf)
    mask = jnp.tril(jnp.ones((HG_CHUNK, HG_CHUNK), dtype=bool))[:, :, None]

    def step(state, inp):
        qt, kt, vt, gt = inp
        G = jnp.cumsum(gt, axis=2)
        o_inter = jnp.einsum('bhtk,bhkv->bhtv', qt * jnp.exp(G), state)
        diff = G[:, :, :, None, :] - G[:, :, None, :, :]
        decay = jnp.exp(jnp.where(mask, diff, -jnp.inf))
        A = jnp.einsum('bhtk,bhtsk,bhsk->bhts', qt, decay, kt)
        o = o_inter + jnp.einsum('bhts,bhsv->bhtv', A, vt)
        g_last = G[:, :, -1]
        kdec = kt * jnp.exp(g_last[:, :, None, :] - G)
        state = jnp.exp(g_last)[..., None] * state + jnp.einsum('bhsk,bhsv->bhkv', kdec, vt)
        return state, o

    s0 = jnp.zeros((B, H, dk, dv), jnp.float32)
    _, o = lax.scan(step, s0, (qc, kc, vc, gc))
    return o.transpose(1, 2, 0, 3, 4).reshape(B, H, S, dv)


def hgrn2_mixer(q, f_fwd, f_bwd, i, g, lb, norm_g):
    B, S, _ = q.shape

    def heads(t):
        return t.reshape(B, S, HG_HEADS, HG_HEAD_DIM).transpose(0, 2, 1, 3).astype(jnp.float32)

    qh = jax.nn.silu(heads(q)) * (HG_HEAD_DIM ** -0.5)
    vh = heads(i)
    lb = lb.astype(jnp.float32).reshape(2, HG_HEADS, 1, HG_HEAD_DIM)

    def gates(fl, lbd):
        f = lbd + (1.0 - lbd) * jax.nn.sigmoid(heads(fl))
        return 1.0 - f, jnp.log(f)

    kf, gf = gates(f_fwd, lb[0])
    kb, gb = gates(f_bwd, lb[1])
    o_f = hgrn2_chunk_scan(qh, kf, vh, gf)
    flip = lambda t: jnp.flip(t, axis=2)
    o_b = flip(hgrn2_chunk_scan(flip(qh), flip(kb), flip(vh), flip(gb)))
    o = (o_f + o_b).transpose(0, 2, 1, 3)
    gate = jax.nn.silu(g.reshape(B, S, HG_HEADS, HG_HEAD_DIM).astype(jnp.float32))
    o = rmsnorm(o, norm_g) * gate
    return o.reshape(B, S, HG_WIDTH)


def expert_choice_ffn(h, router_w, w_gate, w_up, w_down):
    B, S, D = h.shape
    n = B * S
    cap = (EC_CAPACITY_FACTOR * n) // N_EXPERTS
    tok = h.reshape(n, D)
    aff = jax.nn.softmax(jnp.einsum('nd,de->ne', tok.astype(jnp.float32),
                                    router_w.astype(jnp.float32)), axis=-1)
    gate, idx = lax.top_k(aff.T, cap)
    xe = tok[idx]
    hid = jax.nn.silu(jnp.einsum('ecd,edf->ecf', xe, w_gate)) * jnp.einsum('ecd,edf->ecf', xe, w_up)
    ye = jnp.einsum('ecf,efd->ecd', hid, w_down) * gate[..., None].astype(h.dtype)
    out = jnp.zeros((n, D), ye.dtype).at[idx.reshape(-1)].add(ye.reshape(-1, D))
    return out.reshape(B, S, D).astype(h.dtype)


def encoder_layer(x, l, lb_l, norm_mix, w_in, conv_w, diff_lambda, diff_subln, hgrn_norm,
                  w_out, norm_ffn, router_w, w_gate, w_up, w_down):
    h = rmsnorm(x, norm_mix)
    proj = jnp.einsum('bsd,dn->bsn', h, w_in)
    parts = []
    off = 0
    for wdt in PROJ_WIDTHS:
        parts.append(proj[..., off:off + wdt])
        off += wdt
    cb, cc, ch, dq, dk, dv, hq, hf_f, hf_b, hi, hg = parts
    y_conv = short_conv_mixer(cb, cc, ch, conv_w)
    y_diff = diff_attention(dq, dk, dv, diff_lambda, diff_subln, l).astype(x.dtype)
    y_hgrn = hgrn2_mixer(hq, hf_f, hf_b, hi, hg, lb_l, hgrn_norm).astype(x.dtype)
    mix = jnp.concatenate([y_conv, y_diff, y_hgrn], axis=-1)
    x = x + jnp.einsum('bsm,md->bsd', mix, w_out)
    h = rmsnorm(x, norm_ffn)
    return x + expert_choice_ffn(h, router_w, w_gate, w_up, w_down)


def trunk(x, norm_mix, w_in, conv_w, diff_lambda, diff_subln, hgrn_lb, hgrn_norm, w_out,
          norm_ffn, router_w, w_gate, w_up, w_down, norm_final):
    lbs = jnp.cumsum(jax.nn.softmax(hgrn_lb.astype(jnp.float32), axis=0), axis=0)
    for l in range(DEPTH):
        x = encoder_layer(x, l, lbs[l] - lbs[0], norm_mix[l], w_in[l], conv_w[l], diff_lambda[l],
                          diff_subln[l], hgrn_norm[l], w_out[l], norm_ffn[l], router_w[l],
                          w_gate[l], w_up[l], w_down[l])
    return rmsnorm(x, norm_final)


def setup_inputs(seed: int = 0) -> dict:
    key = jax.random.key(seed)
    ks = jax.random.split(key, 18)
    f32 = jnp.float32
    nrm = lambda k, shp, s: (jax.random.normal(k, shp, f32) * s).astype(f32)
    return {
        "x_prompt": nrm(ks[0], (BATCH, SEQ, D_MODEL), 1.0),
        "x_sample": nrm(ks[1], (DEC_BATCH, DEC_SEQ, D_MODEL), 1.0),
        "norm_mix": 1.0 + nrm(ks[2], (DEPTH, D_MODEL), 0.02),
        "w_in": nrm(ks[3], (DEPTH, D_MODEL, D_PROJ), D_MODEL ** -0.5),
        "conv_w": nrm(ks[4], (DEPTH, CONV_K, CONV_WIDTH), CONV_K ** -0.5),
        "diff_lambda": nrm(ks[5], (DEPTH, 4, DIFF_HEAD_DIM), 0.1),
        "diff_subln": 1.0 + nrm(ks[6], (DEPTH, 2 * DIFF_HEAD_DIM), 0.02),
        "hgrn_lb": nrm(ks[7], (DEPTH, 2 * HG_WIDTH), 0.5),
        "hgrn_norm": 1.0 + nrm(ks[8], (DEPTH, HG_HEAD_DIM), 0.02),
        "w_out": nrm(ks[9], (DEPTH, D_MIX, D_MODEL), D_MIX ** -0.5),
        "norm_ffn": 1.0 + nrm(ks[10], (DEPTH, D_MODEL), 0.02),
        "router_w": nrm(ks[11], (DEPTH, D_MODEL, N_EXPERTS), D_MODEL ** -0.5),
        "w_gate": nrm(ks[12], (DEPTH, N_EXPERTS, D_MODEL, D_EXPERT), D_MODEL ** -0.5),
        "w_up": nrm(ks[13], (DEPTH, N_EXPERTS, D_MODEL, D_EXPERT), D_MODEL ** -0.5),
        "w_down": nrm(ks[14], (DEPTH, N_EXPERTS, D_EXPERT, D_MODEL), D_EXPERT ** -0.5),
        "norm_final": 1.0 + nrm(ks[15], (D_MODEL,), 0.02),
    }


def reference(x_prompt, x_sample, norm_mix, w_in, conv_w, diff_lambda, diff_subln, hgrn_lb,
              hgrn_norm, w_out, norm_ffn, router_w, w_gate, w_up, w_down, norm_final):
    y_prompt = trunk(x_prompt, norm_mix, w_in, conv_w, diff_lambda, diff_subln, hgrn_lb, hgrn_norm,
                     w_out, norm_ffn, router_w, w_gate, w_up, w_down, norm_final)
    y_sample = trunk(x_sample, norm_mix, w_in, conv_w, diff_lambda, diff_subln, hgrn_lb, hgrn_norm,
                     w_out, norm_ffn, router_w, w_gate, w_up, w_down, norm_final)
    return (y_prompt, y_sample)
```

```python
import functools
import math

import jax
import jax.numpy as jnp
from jax import lax
from jax.experimental import pallas as pl
from jax.experimental.pallas import tpu as pltpu

F32 = jnp.float32
BF16 = jnp.bfloat16
I32 = jnp.int32

D_MODEL = 1024
CONV_WIDTH = 256
DIFF_WIDTH = 512
DIFF_HEADS = 4
DIFF_HEAD_DIM = 64
HG_WIDTH = 256
HG_HEADS = 4
HG_HEAD_DIM = 64
HG_CHUNK = 64
D_PROJ = 3584
ROPE_THETA = 10000.0
N_EXPERTS = 16
EC_CAPACITY_FACTOR = 2
NORM_EPS = 1e-6

VMEM_LIMIT_BYTES = 56 * 1024 * 1024
LANES = 128

NT_DIMS = (((1,), (1,)), ((), ()))


def _params(*sem):
    return pltpu.CompilerParams(dimension_semantics=sem, vmem_limit_bytes=VMEM_LIMIT_BYTES)


def _split3(x):
    hi = x.astype(BF16)
    r1 = x - hi.astype(F32)
    mid = r1.astype(BF16)
    lo = (r1 - mid.astype(F32)).astype(BF16)
    return hi, mid, lo


def _dot(a, b):
    return jnp.dot(a, b, preferred_element_type=F32)


def _dot_nt(a, b):
    return lax.dot_general(a, b, NT_DIMS, preferred_element_type=F32)


def _dot_exact_lhs(a_bf16, x_f32):
    hi, mid, lo = _split3(x_f32)
    return _dot(a_bf16, hi) + _dot(a_bf16, mid) + _dot(a_bf16, lo)


def _sigmoid(x):
    return 1.0 / (1.0 + jnp.exp(-x))


def _inproj_kernel(x_ref, g_ref, w_ref, cos_ref, sina_ref, sinb_ref,
                   bz_ref, q_ref, k_ref, v_ref, hg_ref):
    x = x_ref[...]
    ms = jnp.mean(x * x, axis=-1, keepdims=True)
    h = (x * lax.rsqrt(ms + NORM_EPS) * g_ref[...]).astype(BF16)

    def proj(a, b):
        return _dot(h, w_ref[:, a:b])

    cw = CONV_WIDTH
    bz_ref[:, 0:cw] = proj(0, cw).astype(BF16)
    bz_ref[:, cw:2 * cw] = (proj(cw, 2 * cw) * proj(2 * cw, 3 * cw)).astype(BF16)

    reps = DIFF_WIDTH // LANES
    cos = jnp.concatenate([cos_ref[...]] * reps, axis=1)
    sina = jnp.concatenate([sina_ref[...]] * reps, axis=1)
    sinb = jnp.concatenate([sinb_ref[...]] * reps, axis=1)
    half = DIFF_HEAD_DIM // 2

    def rope(t):
        return (t * cos + pltpu.roll(t, DIFF_WIDTH - half, axis=1) * sina
                + pltpu.roll(t, half, axis=1) * sinb)

    o = 3 * cw
    q_ref[...] = (rope(proj(o, o + DIFF_WIDTH)) * (DIFF_HEAD_DIM ** -0.5)).astype(BF16)
    o += DIFF_WIDTH
    k_ref[...] = rope(proj(o, o + DIFF_WIDTH)).astype(BF16)
    o += DIFF_WIDTH
    v_ref[...] = proj(o, o + DIFF_WIDTH).astype(BF16)
    o += DIFF_WIDTH
    hg_ref[...] = proj(o, D_PROJ)


def _rope_tables(seq):
    d = DIFF_HEAD_DIM
    inv = 1.0 / (ROPE_THETA ** (jnp.arange(0, d, 2, dtype=F32) / d))
    ang = jnp.arange(seq, dtype=F32)[:, None] * inv[None, :]
    ang = jnp.concatenate([ang, ang, ang, ang], axis=-1)
    cos, sin = jnp.cos(ang), jnp.sin(ang)
    first = (jnp.arange(LANES) % d) < (d // 2)
    sina = jnp.where(first[None, :], -sin, 0.0)
    sinb = jnp.where(first[None, :], 0.0, sin)
    return cos, sina, sinb


def _inproj(x2, g, w_bf16, tables, seq, tm):
    n = x2.shape[0]
    nblk_s = seq // tm
    cos, sina, sinb = tables
    tab_spec = pl.BlockSpec((tm, LANES), lambda i: (i % nblk_s, 0))
    row = lambda w: pl.BlockSpec((tm, w), lambda i: (i, 0))
    hgw = 5 * HG_WIDTH
    return pl.pallas_call(
        _inproj_kernel,
        grid=(n // tm,),
        in_specs=[row(D_MODEL),
                  pl.BlockSpec((1, D_MODEL), lambda i: (0, 0)),
                  pl.BlockSpec((D_MODEL, D_PROJ), lambda i: (0, 0)),
                  tab_spec, tab_spec, tab_spec],
        out_specs=[row(2 * CONV_WIDTH), row(DIFF_WIDTH), row(DIFF_WIDTH), row(DIFF_WIDTH), row(hgw)],
        out_shape=[jax.ShapeDtypeStruct((n, 2 * CONV_WIDTH), BF16),
                   jax.ShapeDtypeStruct((n, DIFF_WIDTH), BF16),
                   jax.ShapeDtypeStruct((n, DIFF_WIDTH), BF16),
                   jax.ShapeDtypeStruct((n, DIFF_WIDTH), BF16),
                   jax.ShapeDtypeStruct((n, hgw), F32)],
        compiler_params=_params("parallel"),
        name="inproj",
    )(x2, g.reshape(1, D_MODEL), w_bf16, cos, sina, sinb)


def _attn_kernel(q_ref, k_ref, v_ref, lam_ref, g_ref, o_ref, *, tk, lam_init):
    q = q_ref[0]
    tq = q.shape[0]
    seq = k_ref.shape[1]
    d = DIFF_HEAD_DIM
    lane = lax.broadcasted_iota(I32, q.shape, 1)
    zero = jnp.zeros_like(q)
    q1 = jnp.where(lane < d, q, zero)
    q2 = jnp.where(lane >= d, q, zero)

    def step(j, carry):
        start = pl.multiple_of(j * tk, tk)
        ks = k_ref[0, pl.ds(start, tk), :]
        vs = v_ref[0, pl.ds(start, tk), :]

        def half(qh, m, l, acc):
            s = _dot_nt(qh, ks)
            mn = jnp.maximum(m, jnp.max(s, axis=1, keepdims=True))
            p = jnp.exp(s - mn)
            al = jnp.exp(m - mn)
            l = al * l + jnp.sum(p, axis=1, keepdims=True)
            acc = al * acc + _dot(p.astype(BF16), vs)
            return mn, l, acc

        m1, l1, a1, m2, l2, a2 = carry
        m1, l1, a1 = half(q1, m1, l1, a1)
        m2, l2, a2 = half(q2, m2, l2, a2)
        return m1, l1, a1, m2, l2, a2

    neg = jnp.full((tq, 1), -1e30, F32)
    z1 = jnp.zeros((tq, 1), F32)
    za = jnp.zeros((tq, 2 * d), F32)
    m1, l1, a1, m2, l2, a2 = lax.fori_loop(0, seq // tk, step, (neg, z1, za, neg, z1, za))

    lp = lam_ref[...]
    lam = (jnp.exp(jnp.sum(lp[0:1] * lp[1:2], axis=1, keepdims=True))
           - jnp.exp(jnp.sum(lp[2:3] * lp[3:4], axis=1, keepdims=True)) + lam_init)
    o = a1 / l1 - lam * (a2 / l2)
    ms = jnp.mean(o * o, axis=-1, keepdims=True)
    y = o * lax.rsqrt(ms + NORM_EPS) * g_ref[...] * (1.0 - lam_init)
    o_ref[0] = y.astype(BF16)


def _attention(q3, k3, v3, lam_params, subln_g, layer, tq, tk):
    bsz, seq, _ = q3.shape
    lam_init = 0.8 - 0.6 * math.exp(-0.3 * layer)
    hd = 2 * DIFF_HEAD_DIM
    return pl.pallas_call(
        functools.partial(_attn_kernel, tk=tk, lam_init=lam_init),
        grid=(bsz, DIFF_HEADS, seq // tq),
        in_specs=[pl.BlockSpec((1, tq, hd), lambda b, h, i: (b, i, h)),
                  pl.BlockSpec((1, seq, hd), lambda b, h, i: (b, 0, h)),
                  pl.BlockSpec((1, seq, hd), lambda b, h, i: (b, 0, h)),
                  pl.BlockSpec((4, DIFF_HEAD_DIM), lambda b, h, i: (0, 0)),
                  pl.BlockSpec((1, hd), lambda b, h, i: (0, 0))],
        out_specs=pl.BlockSpec((1, tq, hd), lambda b, h, i: (b, i, h)),
        out_shape=jax.ShapeDtypeStruct((bsz, seq, DIFF_WIDTH), BF16),
        compiler_params=_params("parallel", "parallel", "parallel"),
        name="attn",
    )(q3, k3, v3, lam_params.astype(F32), subln_g.reshape(1, hd).astype(F32))


def _hgrn_kernel(qf_ref, ff_ref, if_ref, qb_ref, fb_ref, ib_ref, lbp_ref,
                 of_ref, ob_ref, stf_ref, stb_ref, *, layer):
    c_len = HG_CHUNK
    w = HG_WIDTH
    hd = HG_HEAD_DIM
    t_len = qf_ref.shape[1]
    nchunk = t_len // c_len

    @pl.when(pl.program_id(1) == 0)
    def _():
        stf_ref[...] = jnp.zeros_like(stf_ref)
        stb_ref[...] = jnp.zeros_like(stb_ref)

    lbp = lbp_ref[...]
    e = jnp.exp(lbp - jnp.max(lbp, axis=0, keepdims=True))
    sm = e / jnp.sum(e, axis=0, keepdims=True)
    lb = jnp.zeros((1, 2 * w), F32)
    for l in range(1, layer + 1):
        lb = lb + sm[l:l + 1]
    lb_f, lb_b = lb[:, :w], lb[:, w:]

    r = lax.broadcasted_iota(I32, (c_len, c_len), 0)
    c = lax.broadcasted_iota(I32, (c_len, c_len), 1)
    low = r >= c
    upp = r <= c
    low_m = jnp.where(low, 1.0, 0.0).astype(BF16)
    upp_m = jnp.where(upp, 1.0, 0.0).astype(BF16)
    lane = lax.broadcasted_iota(I32, (1, w), 1)
    heads = [(lane >= h * hd) & (lane < (h + 1) * hd) for h in range(HG_HEADS)]
    br = lax.broadcasted_iota(I32, (w, w), 0) // hd
    bc = lax.broadcasted_iota(I32, (w, w), 1) // hd
    same_head = br == bc
    eye = jnp.where(lax.broadcasted_iota(I32, (w, w), 0) == lax.broadcasted_iota(I32, (w, w), 1),
                    1.0, 0.0).astype(BF16)
    mid = c_len // 2

    def chunk(xq, xf, xi, lbd, tri_m, tri, last, st_ref):
        qh = xq * _sigmoid(xq) * (hd ** -0.5)
        f = lbd + (1.0 - lbd) * _sigmoid(xf)
        kk = 1.0 - f
        g = jnp.log(f)
        G = _dot_exact_lhs(tri_m, g)
        g_ref_row = G[mid:mid + 1, :]
        g_last = G[last:last + 1, :]
        qt = (qh * jnp.exp(G - g_ref_row)).astype(BF16)
        kt = (kk * jnp.exp(g_ref_row - G)).astype(BF16)
        q_in = (qh * jnp.exp(G)).astype(BF16)
        k_out = (kk * jnp.exp(g_last - G)).astype(BF16)
        vb = xi.astype(BF16)
        st = st_ref[...]
        o = _dot_nt(q_in, st.astype(BF16))
        zq = jnp.zeros_like(qt)
        for h in range(HG_HEADS):
            a = _dot_nt(jnp.where(heads[h], qt, zq), kt)
            a = jnp.where(tri, a, 0.0).astype(BF16)
            o = o + jnp.where(heads[h], _dot(a, vb), 0.0)
        v_t = _dot_nt(eye, vb).astype(BF16)
        upd = _dot(v_t, k_out)
        st_ref[...] = st * jnp.exp(g_last) + jnp.where(same_head, upd, 0.0)
        return o

    def body(i, carry):
        sf = pl.multiple_of(i * c_len, c_len)
        sb = pl.multiple_of((nchunk - 1 - i) * c_len, c_len)
        of_ref[0, pl.ds(sf, c_len), :] = chunk(
            qf_ref[0, pl.ds(sf, c_len), :], ff_ref[0, pl.ds(sf, c_len), :],
            if_ref[0, pl.ds(sf, c_len), :], lb_f, low_m, low, c_len - 1, stf_ref)
        ob_ref[0, pl.ds(sb, c_len), :] = chunk(
            qb_ref[0, pl.ds(sb, c_len), :], fb_ref[0, pl.ds(sb, c_len), :],
            ib_ref[0, pl.ds(sb, c_len), :], lb_b, upp_m, upp, 0, stb_ref)
        return carry

    lax.fori_loop(0, nchunk, body, 0)


def _hgrn(hg3, hgrn_lb, layer, t_len):
    bsz, seq, _ = hg3.shape
    nt = seq // t_len
    w = HG_WIDTH
    fwd = lambda col: pl.BlockSpec((1, t_len, w), lambda b, j: (b, j, col))
    bwd = lambda col: pl.BlockSpec((1, t_len, w), lambda b, j: (b, nt - 1 - j, col))
    depth = hgrn_lb.shape[0]
    return pl.pallas_call(
        functools.partial(_hgrn_kernel, layer=layer),
        grid=(bsz, nt),
        in_specs=[fwd(0), fwd(1), fwd(3), bwd(0), bwd(2), bwd(3),
                  pl.BlockSpec((depth, 2 * w), lambda b, j: (0, 0))],
        out_specs=[pl.BlockSpec((1, t_len, w), lambda b, j: (b, j, 0)),
                   pl.BlockSpec((1, t_len, w), lambda b, j: (b, nt - 1 - j, 0))],
        out_shape=[jax.ShapeDtypeStruct((bsz, seq, w), F32),
                   jax.ShapeDtypeStruct((bsz, seq, w), F32)],
        scratch_shapes=[pltpu.VMEM((w, w), F32), pltpu.VMEM((w, w), F32)],
        compiler_params=_params("parallel", "arbitrary"),
        name="hgrn",
    )(hg3, hg3, hg3, hg3, hg3, hg3, hgrn_lb.astype(F32))


def _outproj_kernel(x_ref, bz_ref, zp_ref, zn_ref, yd_ref, of_ref, ob_ref, hgate_ref,
                    cw_ref, hn_ref, wo_ref, nf_ref, rw_ref,
                    x1_ref, h2_ref, aff_ref, *, blocks_per_seq):
    tm = x_ref.shape[0]
    cwid = CONV_WIDTH
    i = pl.program_id(0)
    pos = i % blocks_per_seq
    halo = zp_ref.shape[0]

    b = bz_ref[:, 0:cwid].astype(F32)
    z = bz_ref[:, cwid:2 * cwid].astype(F32)
    zprev = zp_ref[:, cwid:2 * cwid].astype(F32)[halo - 1:halo]
    znext = zn_ref[:, cwid:2 * cwid].astype(F32)[0:1]
    zprev = jnp.where(pos == 0, 0.0, zprev)
    znext = jnp.where(pos == blocks_per_seq - 1, 0.0, znext)
    row = lax.broadcasted_iota(I32, (tm, cwid), 0)
    z_up = jnp.where(row == 0, zprev, pltpu.roll(z, 1, axis=0))
    z_dn = jnp.where(row == tm - 1, znext, pltpu.roll(z, tm - 1, axis=0))
    cw = cw_ref[...]
    y_conv = b * (cw[0:1] * z_up + cw[1:2] * z + cw[2:3] * z_dn)

    o = of_ref[...] + ob_ref[...]
    w = HG_WIDTH
    br = lax.broadcasted_iota(I32, (w, w), 0) // HG_HEAD_DIM
    bc = lax.broadcasted_iota(I32, (w, w), 1) // HG_HEAD_DIM
    pool = jnp.where(br == bc, 1.0, 0.0).astype(BF16)
    sq = o * o
    sq_hi = sq.astype(BF16)
    sq_lo = (sq - sq_hi.astype(F32)).astype(BF16)
    ms = (_dot(sq_hi, pool) + _dot(sq_lo, pool)) * (1.0 / HG_HEAD_DIM)
    gt = hgate_ref[...]
    y_h = o * lax.rsqrt(ms + NORM_EPS) * hn_ref[...] * (gt * _sigmoid(gt))

    mixed = (_dot(y_conv.astype(BF16), wo_ref[0:cwid, :])
             + _dot(yd_ref[...], wo_ref[cwid:cwid + DIFF_WIDTH, :])
             + _dot(y_h.astype(BF16), wo_ref[cwid + DIFF_WIDTH:, :]))
    x1 = x_ref[...] + mixed
    x1_ref[...] = x1

    ms2 = jnp.mean(x1 * x1, axis=-1, keepdims=True)
    h2 = x1 * lax.rsqrt(ms2 + NORM_EPS) * nf_ref[...]
    h2_ref[...] = h2.astype(BF16)

    h_hi, h_mid, h_lo = _split3(h2)
    r_hi, r_mid, r_lo = rw_ref[0], rw_ref[1], rw_ref[2]
    logits = (_dot(h_hi, r_hi) + (_dot(h_hi, r_mid) + _dot(h_mid, r_hi))
              + (_dot(h_hi, r_lo) + _dot(h_mid, r_mid) + _dot(h_lo, r_hi)))
    ex = jnp.exp(logits - jnp.max(logits, axis=-1, keepdims=True))
    aff_ref[...] = ex / jnp.sum(ex, axis=-1, keepdims=True)


def _outproj(x2, bz, ydiff, of2, ob2, hg2, conv_w, hgrn_norm, wo_bf16, norm_ffn, rw3, seq, tm):
    n = x2.shape[0]
    halo = 16
    hb = tm // halo
    nhalo = n // halo
    row = lambda w: pl.BlockSpec((tm, w), lambda i: (i, 0))
    full = lambda a, b: pl.BlockSpec((a, b), lambda i: (0, 0))
    hn = jnp.tile(hgrn_norm.astype(F32), HG_HEADS).reshape(1, HG_WIDTH)
    return pl.pallas_call(
        functools.partial(_outproj_kernel, blocks_per_seq=seq // tm),
        grid=(n // tm,),
        in_specs=[row(D_MODEL), row(2 * CONV_WIDTH),
                  pl.BlockSpec((halo, 2 * CONV_WIDTH), lambda i: (jnp.maximum(i * hb - 1, 0), 0)),
                  pl.BlockSpec((halo, 2 * CONV_WIDTH), lambda i: (jnp.minimum((i + 1) * hb, nhalo - 1), 0)),
                  row(DIFF_WIDTH), row(HG_WIDTH), row(HG_WIDTH),
                  pl.BlockSpec((tm, HG_WIDTH), lambda i: (i, 4)),
                  full(3, CONV_WIDTH), full(1, HG_WIDTH), full(D_MODEL, D_MODEL), full(1, D_MODEL),
                  pl.BlockSpec((3, D_MODEL, N_EXPERTS), lambda i: (0, 0, 0))],
        out_specs=[row(D_MODEL), row(D_MODEL), row(N_EXPERTS)],
        out_shape=[jax.ShapeDtypeStruct((n, D_MODEL), F32),
                   jax.ShapeDtypeStruct((n, D_MODEL), BF16),
                   jax.ShapeDtypeStruct((n, N_EXPERTS), F32)],
        compiler_params=_params("parallel"),
        name="outproj",
    )(x2, bz, bz, bz, ydiff, of2, ob2, hg2, conv_w.astype(F32), hn, wo_bf16,
      norm_ffn.reshape(1, D_MODEL).astype(F32), rw3)


def _select_kernel(aff_ref, rankm_ref, rankx_ref, pre_ref, tot_ref, *, cap):
    nblk = aff_ref.shape[0]
    ne = N_EXPERTS
    bits = pltpu.bitcast(aff_ref[...], I32)

    def count(mask):
        part = jnp.sum(jnp.where(mask, 1.0, 0.0), axis=0)
        return jnp.sum(part, axis=1, keepdims=True)

    def search(i, tau):
        cand = tau | jnp.left_shift(jnp.int32(1), 30 - i)
        return jnp.where(count(bits >= cand[None]) >= cap, cand, tau)

    tau = lax.fori_loop(0, 31, search, jnp.zeros((ne, 1), I32))[None]

    r = lax.broadcasted_iota(I32, (LANES, LANES), 0)
    c = lax.broadcasted_iota(I32, (LANES, LANES), 1)
    incl = jnp.where(r <= c, 1.0, 0.0).astype(BF16)
    ones = jnp.ones((LANES, LANES), BF16)

    def prefix(mask):
        m = jnp.where(mask, 1.0, 0.0)
        m2 = m.reshape(nblk * ne, LANES).astype(BF16)
        pre_ref[...] = _dot(m2, incl).reshape(nblk, ne, LANES) - m
        tot_ref[...] = _dot(m2, ones).reshape(nblk, ne, LANES)

        def carry(b, run):
            pre_ref[b] = pre_ref[b] + run
            return run + tot_ref[b]

        lax.fori_loop(0, nblk, carry, jnp.zeros((ne, LANES), F32))
        return pre_ref[...]

    above = bits > tau
    tie = bits == tau
    need = cap - count(above)
    sel = above | (tie & (prefix(tie) < need[None]))
    rank = prefix(sel).astype(I32)
    rankx_ref[...] = rank
    rankm_ref[...] = jnp.where(sel, rank, -1)


def _select(aff3, cap):
    nblk = aff3.shape[0]
    shp = (nblk, N_EXPERTS, LANES)
    spec = pl.BlockSpec(shp, lambda i: (0, 0, 0))
    return pl.pallas_call(
        functools.partial(_select_kernel, cap=cap),
        grid=(1,),
        in_specs=[spec],
        out_specs=[spec, spec],
        out_shape=[jax.ShapeDtypeStruct(shp, I32), jax.ShapeDtypeStruct(shp, I32)],
        scratch_shapes=[pltpu.VMEM(shp, F32), pltpu.VMEM(shp, F32)],
        compiler_params=_params("arbitrary"),
        name="select",
    )(aff3)


def _expert_kernel(base_ref, rank_ref, h_ref, wg_ref, wu_ref, wd_ref, ye_ref, xe_ref,
                   *, nt, tt, sj, cap, win):
    e = pl.program_id(0)
    s = pl.program_id(1)

    @pl.when(s == 0)
    def _():
        xe_ref[...] = jnp.zeros_like(xe_ref)

    @pl.when(s < nt)
    def _():
        base = base_ref[e * nt + s]
        wb = pl.multiple_of(jnp.minimum((base // 8) * 8, cap - win), 8)
        rk = rank_ref[0, 0]
        rel = jnp.concatenate([rk[a:a + 1, :] for a in range(tt // LANES)], axis=1) - wb
        slot = lax.broadcasted_iota(I32, (win, tt), 0)
        onehot = jnp.where(slot == rel, 1.0, 0.0).astype(BF16)
        xe_ref[pl.ds(wb, win), :] += _dot(onehot, h_ref[...])

    @pl.when(s >= nt)
    def _():
        j0 = pl.multiple_of((s - nt) * sj, sj)
        xj = xe_ref[pl.ds(j0, sj), :].astype(BF16)
        a = _dot(xj, wg_ref[0])
        u = _dot(xj, wu_ref[0])
        hid = (a * _sigmoid(a) * u).astype(BF16)
        ye_ref[0] = _dot(hid, wd_ref[0]).astype(BF16)


def _experts(base_flat, rank4, h2, wg, wu, wd, cap, tt, sj):
    n = h2.shape[0]
    nt = n // tt
    nj = cap // sj
    win = min(tt + 8, cap)
    wspec = pl.BlockSpec((1, D_MODEL, D_MODEL), lambda e, s, b: (e, 0, 0))
    grid_spec = pltpu.PrefetchScalarGridSpec(
        num_scalar_prefetch=1,
        grid=(N_EXPERTS, nt + nj),
        in_specs=[pl.BlockSpec((1, 1, tt // LANES, LANES),
                               lambda e, s, b: (e, jnp.minimum(s, nt - 1), 0, 0)),
                  pl.BlockSpec((tt, D_MODEL), lambda e, s, b: (jnp.minimum(s, nt - 1), 0)),
                  wspec, wspec, wspec],
        out_specs=pl.BlockSpec((1, sj, D_MODEL), lambda e, s, b: (e, jnp.maximum(s - nt, 0), 0)),
        scratch_shapes=[pltpu.VMEM((cap, D_MODEL), F32)],
    )
    return pl.pallas_call(
        functools.partial(_expert_kernel, nt=nt, tt=tt, sj=sj, cap=cap, win=win),
        grid_spec=grid_spec,
        out_shape=jax.ShapeDtypeStruct((N_EXPERTS, cap, D_MODEL), BF16),
        compiler_params=_params("arbitrary", "arbitrary"),
        name="experts",
    )(base_flat, rank4, h2, wg, wu, wd)


def _combine_kernel(base_ref, x_ref, rank_ref, gate_ref, nrm_ref, *rest, nt, cap, win, final):
    ye_refs = rest[:N_EXPERTS]
    out_ref = rest[N_EXPERTS]
    t = pl.program_id(0)
    tt = x_ref.shape[0]
    rank = rank_ref[...]
    gate = gate_ref[...]
    lane = lax.broadcasted_iota(I32, (tt, win), 1)
    acc = x_ref[...]
    for e in range(N_EXPERTS):
        wb = _window_start(base_ref[e * nt + t], cap, win)
        onehot = jnp.where(lane == rank[:, e:e + 1] - wb, 1.0, 0.0).astype(BF16)
        acc = acc + gate[:, e:e + 1] * _dot(onehot, ye_refs[e][...])
    if final:
        ms = jnp.mean(acc * acc, axis=-1, keepdims=True)
        acc = acc * lax.rsqrt(ms + NORM_EPS) * nrm_ref[...]
    out_ref[...] = acc


def _window_start(base, cap, win):
    return pl.multiple_of(jnp.minimum((base // 16) * 16, cap - win), 16)


def _combine(base_flat, x1, rank_tok, gate, ye, nrm, cap, tt, final):
    n = x1.shape[0]
    nt = n // tt
    win = min(tt + 16, cap)
    row = lambda w: pl.BlockSpec((tt, w), lambda t, b: (t, 0))

    def ye_spec(e):
        return pl.BlockSpec(
            (pl.Squeezed(), pl.Element(win), pl.Element(D_MODEL)),
            lambda t, b: (e, _window_start(b[e * nt + t], cap, win), 0))

    grid_spec = pltpu.PrefetchScalarGridSpec(
        num_scalar_prefetch=1,
        grid=(nt,),
        in_specs=[row(D_MODEL), row(N_EXPERTS), row(N_EXPERTS),
                  pl.BlockSpec((1, D_MODEL), lambda t, b: (0, 0))]
                 + [ye_spec(e) for e in range(N_EXPERTS)],
        out_specs=row(D_MODEL),
    )
    return pl.pallas_call(
        functools.partial(_combine_kernel, nt=nt, cap=cap, win=win, final=final),
        grid_spec=grid_spec,
        out_shape=jax.ShapeDtypeStruct((n, D_MODEL), F32),
        compiler_params=_params("arbitrary"),
        name="combine",
    )(base_flat, x1, rank_tok, gate, nrm.reshape(1, D_MODEL).astype(F32), *([ye] * N_EXPERTS))


def _tile(n, want):
    t = min(n, want)
    assert n % t == 0, (n, want)
    return t


def _trunk(x, p):
    bsz, seq, _ = x.shape
    n = bsz * seq
    depth = p["w_in"].shape[0]
    cap = (EC_CAPACITY_FACTOR * n) // N_EXPERTS
    nblk = n // LANES
    tm = _tile(seq, 512)
    tq = _tile(seq, 512)
    tk = _tile(seq, 512)
    t_hg = _tile(seq, 512)
    tt_e = _tile(n, 256)
    sj = _tile(cap, 512)
    tt_c = _tile(n, 128)
    tables = _rope_tables(seq)

    x2 = x.reshape(n, D_MODEL)
    for l in range(depth):
        bz, q, k, v, hg = _inproj(x2, p["norm_mix"][l], p["w_in"][l], tables, seq, tm)
        to3 = lambda a: a.reshape(bsz, seq, a.shape[-1])
        ydiff = _attention(to3(q), to3(k), to3(v), p["diff_lambda"][l], p["diff_subln"][l], l, tq, tk)
        o_f, o_b = _hgrn(to3(hg), p["hgrn_lb"], l, t_hg)
        x1, h2, aff = _outproj(x2, bz, ydiff.reshape(n, DIFF_WIDTH), o_f.reshape(n, HG_WIDTH),
                               o_b.reshape(n, HG_WIDTH), hg, p["conv_w"][l], p["hgrn_norm"][l],
                               p["w_out"][l], p["norm_ffn"][l], p["router_w"][l], seq, tm)

        aff3 = aff.reshape(nblk, LANES, N_EXPERTS).transpose(0, 2, 1)
        rankm3, rankx3 = _select(aff3, cap)
        rank_tok = rankm3.transpose(0, 2, 1).reshape(n, N_EXPERTS)
        rank4 = rankm3.transpose(1, 0, 2).reshape(N_EXPERTS, n // tt_e, tt_e // LANES, LANES)
        base128 = rankx3[:, :, 0].T
        base_e = base128[:, ::tt_e // LANES].reshape(-1)
        base_c = base128[:, ::tt_c // LANES].reshape(-1)

        ye = _experts(base_e, rank4, h2, p["w_gate"][l], p["w_up"][l], p["w_down"][l], cap, tt_e, sj)
        x2 = _combine(base_c, x1, rank_tok, aff, ye, p["norm_final"], cap, tt_c, l == depth - 1)
    return x2.reshape(bsz, seq, D_MODEL)


def _split3_param(w):
    hi, mid, lo = _split3(w.astype(F32))
    return jnp.stack([hi, mid, lo], axis=-3)


def kernel(x_prompt, x_sample, norm_mix, w_in, conv_w, diff_lambda, diff_subln, hgrn_lb,
           hgrn_norm, w_out, norm_ffn, router_w, w_gate, w_up, w_down, norm_final):
    p = dict(
        norm_mix=norm_mix.astype(F32), w_in=w_in.astype(BF16), conv_w=conv_w,
        diff_lambda=diff_lambda, diff_subln=diff_subln, hgrn_lb=hgrn_lb, hgrn_norm=hgrn_norm,
        w_out=w_out.astype(BF16), norm_ffn=norm_ffn,
        router_w=_split3_param(router_w),
        w_gate=w_gate.astype(BF16), w_up=w_up.astype(BF16), w_down=w_down.astype(BF16),
        norm_final=norm_final,
    )
    return _trunk(x_prompt, p), _trunk(x_sample, p)
```

```python
import functools
import math

import jax
import jax.numpy as jnp
from jax import lax
from jax.experimental import pallas as pl
from jax.experimental.pallas import tpu as pltpu

F32 = jnp.float32
BF16 = jnp.bfloat16
I32 = jnp.int32

D_MODEL = 1024
CONV_WIDTH = 256
DIFF_WIDTH = 512
DIFF_HEADS = 4
DIFF_HEAD_DIM = 64
HG_WIDTH = 256
HG_HEADS = 4
HG_HEAD_DIM = 64
HG_CHUNK = 64
D_PROJ = 3584
ROPE_THETA = 10000.0
N_EXPERTS = 16
EC_CAPACITY_FACTOR = 2
NORM_EPS = 1e-6

VMEM_LIMIT_BYTES = 56 * 1024 * 1024
LANES = 128
GATHER_ROWS = 128

NT_DIMS = (((1,), (1,)), ((), ()))
Q_SCALE = DIFF_HEAD_DIM ** -0.5 * math.log2(math.e)


def _params(*sem):
    return pltpu.CompilerParams(dimension_semantics=sem, vmem_limit_bytes=VMEM_LIMIT_BYTES)


def _split3(x):
    hi = x.astype(BF16)
    r1 = x - hi.astype(F32)
    mid = r1.astype(BF16)
    lo = (r1 - mid.astype(F32)).astype(BF16)
    return hi, mid, lo


def _dot(a, b):
    return jnp.dot(a, b, preferred_element_type=F32)


def _dot_nt(a, b):
    return lax.dot_general(a, b, NT_DIMS, preferred_element_type=F32)


def _dot_exact_lhs(a_bf16, x_f32):
    hi, mid, lo = _split3(x_f32)
    return _dot(a_bf16, hi) + _dot(a_bf16, mid) + _dot(a_bf16, lo)


def _sigmoid(x):
    return 1.0 / (1.0 + jnp.exp(-x))


def _inproj_kernel(x_ref, g_ref, w_ref, cos_ref, sina_ref, sinb_ref,
                   bz_ref, q_ref, k_ref, v_ref, hg_ref):
    x = x_ref[...]
    ms = jnp.mean(x * x, axis=-1, keepdims=True)
    h = (x * lax.rsqrt(ms + NORM_EPS) * g_ref[...]).astype(BF16)

    def proj(a, b):
        return _dot(h, w_ref[:, a:b])

    cw = CONV_WIDTH
    bz_ref[:, 0:cw] = proj(0, cw).astype(BF16)
    bz_ref[:, cw:2 * cw] = (proj(cw, 2 * cw) * proj(2 * cw, 3 * cw)).astype(BF16)

    reps = DIFF_WIDTH // LANES
    cos = jnp.concatenate([cos_ref[...]] * reps, axis=1)
    sina = jnp.concatenate([sina_ref[...]] * reps, axis=1)
    sinb = jnp.concatenate([sinb_ref[...]] * reps, axis=1)
    half = DIFF_HEAD_DIM // 2

    def rope(t):
        return (t * cos + pltpu.roll(t, DIFF_WIDTH - half, axis=1) * sina
                + pltpu.roll(t, half, axis=1) * sinb)

    o = 3 * cw
    q_ref[...] = (rope(proj(o, o + DIFF_WIDTH)) * Q_SCALE).astype(BF16)
    o += DIFF_WIDTH
    k_ref[...] = rope(proj(o, o + DIFF_WIDTH)).astype(BF16)
    o += DIFF_WIDTH
    v = proj(o, o + DIFF_WIDTH).astype(BF16)
    hd = 2 * DIFF_HEAD_DIM
    ones = jnp.ones((v.shape[0], hd), BF16)
    for head in range(DIFF_HEADS):
        v_ref[:, 2 * head * hd:(2 * head + 1) * hd] = v[:, head * hd:(head + 1) * hd]
        v_ref[:, (2 * head + 1) * hd:(2 * head + 2) * hd] = ones
    o += DIFF_WIDTH
    hg_ref[...] = proj(o, D_PROJ)


def _rope_tables(seq):
    d = DIFF_HEAD_DIM
    inv = 1.0 / (ROPE_THETA ** (jnp.arange(0, d, 2, dtype=F32) / d))
    ang = jnp.arange(seq, dtype=F32)[:, None] * inv[None, :]
    ang = jnp.concatenate([ang, ang, ang, ang], axis=-1)
    cos, sin = jnp.cos(ang), jnp.sin(ang)
    first = (jnp.arange(LANES) % d) < (d // 2)
    sina = jnp.where(first[None, :], -sin, 0.0)
    sinb = jnp.where(first[None, :], 0.0, sin)
    return cos, sina, sinb


def _inproj(x2, g, w_bf16, tables, seq, tm):
    n = x2.shape[0]
    nblk_s = seq // tm
    cos, sina, sinb = tables
    tab_spec = pl.BlockSpec((tm, LANES), lambda i: (i % nblk_s, 0))
    row = lambda w: pl.BlockSpec((tm, w), lambda i: (i, 0))
    hgw = 5 * HG_WIDTH
    return pl.pallas_call(
        _inproj_kernel,
        grid=(n // tm,),
        in_specs=[row(D_MODEL),
                  pl.BlockSpec((1, D_MODEL), lambda i: (0, 0)),
                  pl.BlockSpec((D_MODEL, D_PROJ), lambda i: (0, 0)),
                  tab_spec, tab_spec, tab_spec],
        out_specs=[row(2 * CONV_WIDTH), row(DIFF_WIDTH), row(DIFF_WIDTH), row(2 * DIFF_WIDTH), row(hgw)],
        out_shape=[jax.ShapeDtypeStruct((n, 2 * CONV_WIDTH), BF16),
                   jax.ShapeDtypeStruct((n, DIFF_WIDTH), BF16),
                   jax.ShapeDtypeStruct((n, DIFF_WIDTH), BF16),
                   jax.ShapeDtypeStruct((n, 2 * DIFF_WIDTH), BF16),
                   jax.ShapeDtypeStruct((n, hgw), F32)],
        compiler_params=_params("parallel"),
        name="inproj",
    )(x2, g.reshape(1, D_MODEL), w_bf16, cos, sina, sinb)


def _attn_kernel(q_ref, k_ref, v_ref, lam_ref, g_ref, o_ref, *, tk, lam_init):
    q = q_ref[0]
    tq = q.shape[0]
    seq = k_ref.shape[1]
    nblk = seq // tk
    d = DIFF_HEAD_DIM
    hd = 2 * d
    lane = lax.broadcasted_iota(I32, q.shape, 1)
    zero = jnp.zeros_like(q)
    qs = (jnp.where(lane < d, q, zero), jnp.where(lane >= d, q, zero))

    def scores(j):
        ks = k_ref[0, j * tk:(j + 1) * tk, :]
        out = []
        for h in range(2):
            s = _dot_nt(qs[h], ks)
            out.append((s, jnp.max(s, axis=1, keepdims=True)))
        return out

    def consume(j, sc, state):
        vs = v_ref[0, j * tk:(j + 1) * tk, :]
        out = []
        for h in range(2):
            m, acc = state[h]
            s, bm = sc[h]
            mn = jnp.maximum(m, bm)
            p = jnp.exp2(s - mn).astype(BF16)
            acc = jnp.exp2(m - mn) * acc + _dot(p, vs)
            out.append((mn, acc))
        return out

    init = (jnp.full((tq, 1), -1e30, F32), jnp.zeros((tq, 2 * hd), F32))
    state = [init, init]
    sc = scores(0)
    for j in range(nblk):
        sc_next = scores(j + 1) if j + 1 < nblk else None
        state = consume(j, sc, state)
        sc = sc_next
    (_, a1), (_, a2) = state

    lp = lam_ref[...]
    lam = (jnp.exp(jnp.sum(lp[0:1] * lp[1:2], axis=1, keepdims=True))
           - jnp.exp(jnp.sum(lp[2:3] * lp[3:4], axis=1, keepdims=True)) + lam_init)
    o = a1[:, :hd] / a1[:, hd:] - lam * (a2[:, :hd] / a2[:, hd:])
    ms = jnp.mean(o * o, axis=-1, keepdims=True)
    y = o * lax.rsqrt(ms + NORM_EPS) * g_ref[...] * (1.0 - lam_init)
    o_ref[0] = y.astype(BF16)


def _attention(q3, k3, v3, lam_params, subln_g, layer, tq, tk):
    bsz, seq, _ = q3.shape
    lam_init = 0.8 - 0.6 * math.exp(-0.3 * layer)
    hd = 2 * DIFF_HEAD_DIM
    return pl.pallas_call(
        functools.partial(_attn_kernel, tk=tk, lam_init=lam_init),
        grid=(bsz, DIFF_HEADS, seq // tq),
        in_specs=[pl.BlockSpec((1, tq, hd), lambda b, h, i: (b, i, h)),
                  pl.BlockSpec((1, seq, hd), lambda b, h, i: (b, 0, h)),
                  pl.BlockSpec((1, seq, 2 * hd), lambda b, h, i: (b, 0, h)),
                  pl.BlockSpec((4, DIFF_HEAD_DIM), lambda b, h, i: (0, 0)),
                  pl.BlockSpec((1, hd), lambda b, h, i: (0, 0))],
        out_specs=pl.BlockSpec((1, tq, hd), lambda b, h, i: (b, i, h)),
        out_shape=jax.ShapeDtypeStruct((bsz, seq, DIFF_WIDTH), BF16),
        compiler_params=_params("parallel", "parallel", "parallel"),
        name="attn",
    )(q3, k3, v3, lam_params.astype(F32), subln_g.reshape(1, hd).astype(F32))


def _hgrn_kernel(qf_ref, ff_ref, if_ref, qb_ref, fb_ref, ib_ref, lbp_ref,
                 of_ref, ob_ref, stf_ref, stb_ref, *, layer):
    c_len = HG_CHUNK
    w = HG_WIDTH
    hd = HG_HEAD_DIM
    t_len = qf_ref.shape[1]
    nchunk = t_len // c_len

    @pl.when(pl.program_id(1) == 0)
    def _():
        stf_ref[...] = jnp.zeros_like(stf_ref)
        stb_ref[...] = jnp.zeros_like(stb_ref)

    lbp = lbp_ref[...]
    e = jnp.exp(lbp - jnp.max(lbp, axis=0, keepdims=True))
    sm = e / jnp.sum(e, axis=0, keepdims=True)
    lb = jnp.zeros((1, 2 * w), F32)
    for l in range(1, layer + 1):
        lb = lb + sm[l:l + 1]
    lb_f, lb_b = lb[:, :w], lb[:, w:]

    r = lax.broadcasted_iota(I32, (c_len, c_len), 0)
    c = lax.broadcasted_iota(I32, (c_len, c_len), 1)
    low = r >= c
    upp = r <= c
    low_m = jnp.where(low, 1.0, 0.0).astype(BF16)
    upp_m = jnp.where(upp, 1.0, 0.0).astype(BF16)
    lane = lax.broadcasted_iota(I32, (1, w), 1)
    heads = [(lane >= h * hd) & (lane < (h + 1) * hd) for h in range(HG_HEADS)]
    br = lax.broadcasted_iota(I32, (w, w), 0) // hd
    bc = lax.broadcasted_iota(I32, (w, w), 1) // hd
    same_head = br == bc
    eye = jnp.where(lax.broadcasted_iota(I32, (w, w), 0) == lax.broadcasted_iota(I32, (w, w), 1),
                    1.0, 0.0).astype(BF16)
    mid = c_len // 2

    def chunk(xq, xf, xi, lbd, tri_m, tri, last, st_ref):
        qh = xq * _sigmoid(xq) * (hd ** -0.5)
        f = lbd + (1.0 - lbd) * _sigmoid(xf)
        kk = 1.0 - f
        g = jnp.log(f)
        G = _dot_exact_lhs(tri_m, g)
        g_ref_row = G[mid:mid + 1, :]
        g_last = G[last:last + 1, :]
        qt = (qh * jnp.exp(G - g_ref_row)).astype(BF16)
        kt = (kk * jnp.exp(g_ref_row - G)).astype(BF16)
        q_in = (qh * jnp.exp(G)).astype(BF16)
        k_out = (kk * jnp.exp(g_last - G)).astype(BF16)
        vb = xi.astype(BF16)
        st = st_ref[...]
        o = _dot_nt(q_in, st.astype(BF16))
        zq = jnp.zeros_like(qt)
        for h in range(HG_HEADS):
            a = _dot_nt(jnp.where(heads[h], qt, zq), kt)
            a = jnp.where(tri, a, 0.0).astype(BF16)
            o = o + jnp.where(heads[h], _dot(a, vb), 0.0)
        v_t = _dot_nt(eye, vb).astype(BF16)
        upd = _dot(v_t, k_out)
        st_ref[...] = st * jnp.exp(g_last) + jnp.where(same_head, upd, 0.0)
        return o

    def body(i, carry):
        sf = pl.multiple_of(i * c_len, c_len)
        sb = pl.multiple_of((nchunk - 1 - i) * c_len, c_len)
        of_ref[0, pl.ds(sf, c_len), :] = chunk(
            qf_ref[0, pl.ds(sf, c_len), :], ff_ref[0, pl.ds(sf, c_len), :],
            if_ref[0, pl.ds(sf, c_len), :], lb_f, low_m, low, c_len - 1, stf_ref)
        ob_ref[0, pl.ds(sb, c_len), :] = chunk(
            qb_ref[0, pl.ds(sb, c_len), :], fb_ref[0, pl.ds(sb, c_len), :],
            ib_ref[0, pl.ds(sb, c_len), :], lb_b, upp_m, upp, 0, stb_ref)
        return carry

    lax.fori_loop(0, nchunk, body, 0)


def _hgrn(hg3, hgrn_lb, layer, t_len):
    bsz, seq, _ = hg3.shape
    nt = seq // t_len
    w = HG_WIDTH
    fwd = lambda col: pl.BlockSpec((1, t_len, w), lambda b, j: (b, j, col))
    bwd = lambda col: pl.BlockSpec((1, t_len, w), lambda b, j: (b, nt - 1 - j, col))
    depth = hgrn_lb.shape[0]
    return pl.pallas_call(
        functools.partial(_hgrn_kernel, layer=layer),
        grid=(bsz, nt),
        in_specs=[fwd(0), fwd(1), fwd(3), bwd(0), bwd(2), bwd(3),
                  pl.BlockSpec((depth, 2 * w), lambda b, j: (0, 0))],
        out_specs=[pl.BlockSpec((1, t_len, w), lambda b, j: (b, j, 0)),
                   pl.BlockSpec((1, t_len, w), lambda b, j: (b, nt - 1 - j, 0))],
        out_shape=[jax.ShapeDtypeStruct((bsz, seq, w), F32),
                   jax.ShapeDtypeStruct((bsz, seq, w), F32)],
        scratch_shapes=[pltpu.VMEM((w, w), F32), pltpu.VMEM((w, w), F32)],
        compiler_params=_params("parallel", "arbitrary"),
        name="hgrn",
    )(hg3, hg3, hg3, hg3, hg3, hg3, hgrn_lb.astype(F32))


def _outproj_kernel(x_ref, bz_ref, zp_ref, zn_ref, yd_ref, of_ref, ob_ref, hgate_ref,
                    cw_ref, hn_ref, wo_ref, nf_ref, rw_ref,
                    x1_ref, h2_ref, aff_ref, *, blocks_per_seq):
    tm = x_ref.shape[0]
    cwid = CONV_WIDTH
    i = pl.program_id(0)
    pos = i % blocks_per_seq
    halo = zp_ref.shape[0]

    b = bz_ref[:, 0:cwid].astype(F32)
    z = bz_ref[:, cwid:2 * cwid].astype(F32)
    zprev = zp_ref[:, cwid:2 * cwid].astype(F32)[halo - 1:halo]
    znext = zn_ref[:, cwid:2 * cwid].astype(F32)[0:1]
    zprev = jnp.where(pos == 0, 0.0, zprev)
    znext = jnp.where(pos == blocks_per_seq - 1, 0.0, znext)
    row = lax.broadcasted_iota(I32, (tm, cwid), 0)
    z_up = jnp.where(row == 0, zprev, pltpu.roll(z, 1, axis=0))
    z_dn = jnp.where(row == tm - 1, znext, pltpu.roll(z, tm - 1, axis=0))
    cw = cw_ref[...]
    y_conv = b * (cw[0:1] * z_up + cw[1:2] * z + cw[2:3] * z_dn)

    o = of_ref[...] + ob_ref[...]
    w = HG_WIDTH
    br = lax.broadcasted_iota(I32, (w, w), 0) // HG_HEAD_DIM
    bc = lax.broadcasted_iota(I32, (w, w), 1) // HG_HEAD_DIM
    pool = jnp.where(br == bc, 1.0, 0.0).astype(BF16)
    sq = o * o
    sq_hi = sq.astype(BF16)
    sq_lo = (sq - sq_hi.astype(F32)).astype(BF16)
    ms = (_dot(sq_hi, pool) + _dot(sq_lo, pool)) * (1.0 / HG_HEAD_DIM)
    gt = hgate_ref[...]
    y_h = o * lax.rsqrt(ms + NORM_EPS) * hn_ref[...] * (gt * _sigmoid(gt))

    mixed = (_dot(y_conv.astype(BF16), wo_ref[0:cwid, :])
             + _dot(yd_ref[...], wo_ref[cwid:cwid + DIFF_WIDTH, :])
             + _dot(y_h.astype(BF16), wo_ref[cwid + DIFF_WIDTH:, :]))
    x1 = x_ref[...] + mixed
    x1_ref[...] = x1

    ms2 = jnp.mean(x1 * x1, axis=-1, keepdims=True)
    h2 = x1 * lax.rsqrt(ms2 + NORM_EPS) * nf_ref[...]
    h2_ref[...] = h2.astype(BF16)

    h_hi, h_mid, h_lo = _split3(h2)
    r_hi, r_mid, r_lo = rw_ref[0], rw_ref[1], rw_ref[2]
    logits = (_dot(h_hi, r_hi) + (_dot(h_hi, r_mid) + _dot(h_mid, r_hi))
              + (_dot(h_hi, r_lo) + _dot(h_mid, r_mid) + _dot(h_lo, r_hi)))
    ex = jnp.exp(logits - jnp.max(logits, axis=-1, keepdims=True))
    aff_ref[...] = ex / jnp.sum(ex, axis=-1, keepdims=True)


def _outproj(x2, bz, ydiff, of2, ob2, hg2, conv_w, hgrn_norm, wo_bf16, norm_ffn, rw3, seq, tm):
    n = x2.shape[0]
    halo = 16
    hb = tm // halo
    nhalo = n // halo
    row = lambda w: pl.BlockSpec((tm, w), lambda i: (i, 0))
    full = lambda a, b: pl.BlockSpec((a, b), lambda i: (0, 0))
    hn = jnp.tile(hgrn_norm.astype(F32), HG_HEADS).reshape(1, HG_WIDTH)
    return pl.pallas_call(
        functools.partial(_outproj_kernel, blocks_per_seq=seq // tm),
        grid=(n // tm,),
        in_specs=[row(D_MODEL), row(2 * CONV_WIDTH),
                  pl.BlockSpec((halo, 2 * CONV_WIDTH), lambda i: (jnp.maximum(i * hb - 1, 0), 0)),
                  pl.BlockSpec((halo, 2 * CONV_WIDTH), lambda i: (jnp.minimum((i + 1) * hb, nhalo - 1), 0)),
                  row(DIFF_WIDTH), row(HG_WIDTH), row(HG_WIDTH),
                  pl.BlockSpec((tm, HG_WIDTH), lambda i: (i, 4)),
                  full(3, CONV_WIDTH), full(1, HG_WIDTH), full(D_MODEL, D_MODEL), full(1, D_MODEL),
                  pl.BlockSpec((3, D_MODEL, N_EXPERTS), lambda i: (0, 0, 0))],
        out_specs=[row(D_MODEL), row(D_MODEL), row(N_EXPERTS)],
        out_shape=[jax.ShapeDtypeStruct((n, D_MODEL), F32),
                   jax.ShapeDtypeStruct((n, D_MODEL), BF16),
                   jax.ShapeDtypeStruct((n, N_EXPERTS), F32)],
        compiler_params=_params("parallel"),
        name="outproj",
    )(x2, bz, bz, bz, ydiff, of2, ob2, hg2, conv_w.astype(F32), hn, wo_bf16,
      norm_ffn.reshape(1, D_MODEL).astype(F32), rw3)


def _select_kernel(aff_ref, rankm_ref, rankx_ref, pre_ref, tot_ref, *, cap):
    nblk = aff_ref.shape[0]
    ne = N_EXPERTS
    bits = pltpu.bitcast(aff_ref[...], I32)

    def count(mask):
        part = jnp.sum(jnp.where(mask, 1.0, 0.0), axis=0)
        return jnp.sum(part, axis=1, keepdims=True)

    def search(i, tau):
        cand = tau | jnp.left_shift(jnp.int32(1), 30 - i)
        return jnp.where(count(bits >= cand[None]) >= cap, cand, tau)

    tau = lax.fori_loop(0, 31, search, jnp.zeros((ne, 1), I32))[None]

    r = lax.broadcasted_iota(I32, (LANES, LANES), 0)
    c = lax.broadcasted_iota(I32, (LANES, LANES), 1)
    incl = jnp.where(r <= c, 1.0, 0.0).astype(BF16)
    ones = jnp.ones((LANES, LANES), BF16)

    def prefix(mask):
        m = jnp.where(mask, 1.0, 0.0)
        m2 = m.reshape(nblk * ne, LANES).astype(BF16)
        pre_ref[...] = _dot(m2, incl).reshape(nblk, ne, LANES) - m
        tot_ref[...] = _dot(m2, ones).reshape(nblk, ne, LANES)

        def carry(b, run):
            pre_ref[b] = pre_ref[b] + run
            return run + tot_ref[b]

        lax.fori_loop(0, nblk, carry, jnp.zeros((ne, LANES), F32))
        return pre_ref[...]

    above = bits > tau
    tie = bits == tau
    need = cap - count(above)
    sel = above | (tie & (prefix(tie) < need[None]))
    rank = prefix(sel).astype(I32)
    rankx_ref[...] = rank
    rankm_ref[...] = jnp.where(sel, rank, -1)


def _select(aff3, cap):
    nblk = aff3.shape[0]
    shp = (nblk, N_EXPERTS, LANES)
    spec = pl.BlockSpec(shp, lambda i: (0, 0, 0))
    return pl.pallas_call(
        functools.partial(_select_kernel, cap=cap),
        grid=(1,),
        in_specs=[spec],
        out_specs=[spec, spec],
        out_shape=[jax.ShapeDtypeStruct(shp, I32), jax.ShapeDtypeStruct(shp, I32)],
        scratch_shapes=[pltpu.VMEM(shp, F32), pltpu.VMEM(shp, F32)],
        compiler_params=_params("arbitrary"),
        name="select",
    )(aff3)


def _expert_kernel(base_ref, rank_ref, h_ref, wg_ref, wu_ref, wd_ref, ye_ref, xe_ref,
                   *, nt, tt, sj, cap, win):
    e = pl.program_id(0)
    s = pl.program_id(1)

    @pl.when(s == 0)
    def _():
        xe_ref[...] = jnp.zeros_like(xe_ref)

    @pl.when(s < nt)
    def _():
        base = base_ref[e * (nt + 1) + s]
        count = base_ref[e * (nt + 1) + s + 1] - base
        wb = (base // 8) * 8
        nwin = jnp.where(count > 0, (base - wb + count + win - 1) // win, 0)
        rk = rank_ref[0, 0]
        rel = jnp.concatenate([rk[a:a + 1, :] for a in range(tt // LANES)], axis=1) - wb
        slot = lax.broadcasted_iota(I32, (win, tt), 0)

        def fill(w, carry):
            onehot = jnp.where(slot == rel - w * win, 1.0, 0.0).astype(BF16)
            r0 = pl.multiple_of(wb + w * win, 8)
            xe_ref[pl.ds(r0, win), :] += _dot(onehot, h_ref[...])
            return carry

        lax.fori_loop(0, nwin, fill, 0)

    @pl.when(s >= nt)
    def _():
        j0 = pl.multiple_of((s - nt) * sj, sj)
        xj = xe_ref[pl.ds(j0, sj), :].astype(BF16)
        a = _dot(xj, wg_ref[0])
        u = _dot(xj, wu_ref[0])
        hid = (a * _sigmoid(a) * u).astype(BF16)
        ye_ref[0] = _dot(hid, wd_ref[0]).astype(BF16)


def _experts(base_flat, rank4, h2, wg, wu, wd, cap, tt, sj):
    n = h2.shape[0]
    nt = n // tt
    nj = cap // sj
    win = GATHER_ROWS
    wspec = pl.BlockSpec((1, D_MODEL, D_MODEL), lambda e, s, b: (e, 0, 0))
    grid_spec = pltpu.PrefetchScalarGridSpec(
        num_scalar_prefetch=1,
        grid=(N_EXPERTS, nt + nj),
        in_specs=[pl.BlockSpec((1, 1, tt // LANES, LANES),
                               lambda e, s, b: (e, jnp.minimum(s, nt - 1), 0, 0)),
                  pl.BlockSpec((tt, D_MODEL), lambda e, s, b: (jnp.minimum(s, nt - 1), 0)),
                  wspec, wspec, wspec],
        out_specs=pl.BlockSpec((1, sj, D_MODEL), lambda e, s, b: (e, jnp.maximum(s - nt, 0), 0)),
        scratch_shapes=[pltpu.VMEM((cap + win + 8, D_MODEL), F32)],
    )
    return pl.pallas_call(
        functools.partial(_expert_kernel, nt=nt, tt=tt, sj=sj, cap=cap, win=win),
        grid_spec=grid_spec,
        out_shape=jax.ShapeDtypeStruct((N_EXPERTS, cap, D_MODEL), BF16),
        compiler_params=_params("arbitrary", "arbitrary"),
        name="experts",
    )(base_flat, rank4, h2, wg, wu, wd)


def _combine_kernel(base_ref, x_ref, rank_ref, gate_ref, nrm_ref, *rest, nt, cap, win, final):
    ye_refs = rest[:N_EXPERTS]
    out_ref = rest[N_EXPERTS]
    t = pl.program_id(0)
    tt = x_ref.shape[0]
    rank = rank_ref[...]
    gate = gate_ref[...]
    lane = lax.broadcasted_iota(I32, (tt, win), 1)
    acc = x_ref[...]
    for e in range(N_EXPERTS):
        wb = _window_start(base_ref[e * nt + t], cap, win)
        onehot = jnp.where(lane == rank[:, e:e + 1] - wb, 1.0, 0.0).astype(BF16)
        acc = acc + gate[:, e:e + 1] * _dot(onehot, ye_refs[e][...])
    if final:
        ms = jnp.mean(acc * acc, axis=-1, keepdims=True)
        acc = acc * lax.rsqrt(ms + NORM_EPS) * nrm_ref[...]
    out_ref[...] = acc


def _window_start(base, cap, win):
    return pl.multiple_of(jnp.minimum((base // 16) * 16, cap - win), 16)


def _combine(base_flat, x1, rank_tok, gate, ye, nrm, cap, tt, final):
    n = x1.shape[0]
    nt = n // tt
    win = min(tt + 16, cap)
    row = lambda w: pl.BlockSpec((tt, w), lambda t, b: (t, 0))

    def ye_spec(e):
        return pl.BlockSpec(
            (pl.Squeezed(), pl.Element(win), pl.Element(D_MODEL)),
            lambda t, b: (e, _window_start(b[e * nt + t], cap, win), 0))

    grid_spec = pltpu.PrefetchScalarGridSpec(
        num_scalar_prefetch=1,
        grid=(nt,),
        in_specs=[row(D_MODEL), row(N_EXPERTS), row(N_EXPERTS),
                  pl.BlockSpec((1, D_MODEL), lambda t, b: (0, 0))]
                 + [ye_spec(e) for e in range(N_EXPERTS)],
        out_specs=row(D_MODEL),
    )
    return pl.pallas_call(
        functools.partial(_combine_kernel, nt=nt, cap=cap, win=win, final=final),
        grid_spec=grid_spec,
        out_shape=jax.ShapeDtypeStruct((n, D_MODEL), F32),
        compiler_params=_params("arbitrary"),
        name="combine",
    )(base_flat, x1, rank_tok, gate, nrm.reshape(1, D_MODEL).astype(F32), *([ye] * N_EXPERTS))


def _tile(n, want):
    t = min(n, want)
    assert n % t == 0, (n, want)
    return t


def _trunk(x, p):
    bsz, seq, _ = x.shape
    n = bsz * seq
    depth = p["w_in"].shape[0]
    cap = (EC_CAPACITY_FACTOR * n) // N_EXPERTS
    nblk = n // LANES
    tm = _tile(seq, 512)
    tq = _tile(seq, 512)
    tk = _tile(seq, 2048)
    t_hg = _tile(seq, 512)
    tt_e = _tile(n, 1024)
    sj = _tile(cap, 512)
    tt_c = _tile(n, 128)
    tables = _rope_tables(seq)

    x2 = x.reshape(n, D_MODEL)
    for l in range(depth):
        bz, q, k, v, hg = _inproj(x2, p["norm_mix"][l], p["w_in"][l], tables, seq, tm)
        to3 = lambda a: a.reshape(bsz, seq, a.shape[-1])
        ydiff = _attention(to3(q), to3(k), to3(v), p["diff_lambda"][l], p["diff_subln"][l], l, tq, tk)
        o_f, o_b = _hgrn(to3(hg), p["hgrn_lb"], l, t_hg)
        x1, h2, aff = _outproj(x2, bz, ydiff.reshape(n, DIFF_WIDTH), o_f.reshape(n, HG_WIDTH),
                               o_b.reshape(n, HG_WIDTH), hg, p["conv_w"][l], p["hgrn_norm"][l],
                               p["w_out"][l], p["norm_ffn"][l], p["router_w"][l], seq, tm)

        aff3 = aff.reshape(nblk, LANES, N_EXPERTS).transpose(0, 2, 1)
        rankm3, rankx3 = _select(aff3, cap)
        rank_tok = rankm3.transpose(0, 2, 1).reshape(n, N_EXPERTS)
        rank4 = rankm3.transpose(1, 0, 2).reshape(N_EXPERTS, n // tt_e, tt_e // LANES, LANES)
        base128 = rankx3[:, :, 0].T
        base_e = jnp.concatenate([base128[:, ::tt_e // LANES],
                                  jnp.full((N_EXPERTS, 1), cap, I32)], axis=1).reshape(-1)
        base_c = base128[:, ::tt_c // LANES].reshape(-1)

        ye = _experts(base_e, rank4, h2, p["w_gate"][l], p["w_up"][l], p["w_down"][l], cap, tt_e, sj)
        x2 = _combine(base_c, x1, rank_tok, aff, ye, p["norm_final"], cap, tt_c, l == depth - 1)
    return x2.reshape(bsz, seq, D_MODEL)


def _split3_param(w):
    hi, mid, lo = _split3(w.astype(F32))
    return jnp.stack([hi, mid, lo], axis=-3)


def kernel(x_prompt, x_sample, norm_mix, w_in, conv_w, diff_lambda, diff_subln, hgrn_lb,
           hgrn_norm, w_out, norm_ffn, router_w, w_gate, w_up, w_down, norm_final):
    p = dict(
        norm_mix=norm_mix.astype(F32), w_in=w_in.astype(BF16), conv_w=conv_w,
        diff_lambda=diff_lambda, diff_subln=diff_subln, hgrn_lb=hgrn_lb, hgrn_norm=hgrn_norm,
        w_out=w_out.astype(BF16), norm_ffn=norm_ffn,
        router_w=_split3_param(router_w),
        w_gate=w_gate.astype(BF16), w_up=w_up.astype(BF16), w_down=w_down.astype(BF16),
        norm_final=norm_final,
    )
    return _trunk(x_prompt, p), _trunk(x_sample, p)
```

```python
import functools
import math

import jax
import jax.numpy as jnp
from jax import lax
from jax.experimental import pallas as pl
from jax.experimental.pallas import tpu as pltpu

F32 = jnp.float32
BF16 = jnp.bfloat16
I32 = jnp.int32

D_MODEL = 1024
CONV_WIDTH = 256
DIFF_WIDTH = 512
DIFF_HEADS = 4
DIFF_HEAD_DIM = 64
HG_WIDTH = 256
HG_HEADS = 4
HG_HEAD_DIM = 64
HG_CHUNK = 64
D_PROJ = 3584
ROPE_THETA = 10000.0
N_EXPERTS = 16
EC_CAPACITY_FACTOR = 2
NORM_EPS = 1e-6

VMEM_LIMIT_BYTES = 56 * 1024 * 1024
LANES = 128
GATHER_ROWS = 128
COMBINE_MAIN_ROWS = 64

NT_DIMS = (((1,), (1,)), ((), ()))
Q_SCALE = DIFF_HEAD_DIM ** -0.5 * math.log2(math.e)


def _params(*sem):
    return pltpu.CompilerParams(dimension_semantics=sem, vmem_limit_bytes=VMEM_LIMIT_BYTES)


def _split3(x):
    hi = x.astype(BF16)
    r1 = x - hi.astype(F32)
    mid = r1.astype(BF16)
    lo = (r1 - mid.astype(F32)).astype(BF16)
    return hi, mid, lo


def _dot(a, b):
    return jnp.dot(a, b, preferred_element_type=F32)


def _dot_nt(a, b):
    return lax.dot_general(a, b, NT_DIMS, preferred_element_type=F32)


def _dot_exact_lhs(a_bf16, x_f32):
    hi, mid, lo = _split3(x_f32)
    return _dot(a_bf16, hi) + _dot(a_bf16, mid) + _dot(a_bf16, lo)


def _sigmoid(x):
    return 1.0 / (1.0 + jnp.exp(-x))


def _inproj_kernel(x_ref, g_ref, w_ref, cos_ref, sina_ref, sinb_ref,
                   bz_ref, q_ref, k_ref, v_ref, hg_ref):
    x = x_ref[...]
    ms = jnp.mean(x * x, axis=-1, keepdims=True)
    h = (x * lax.rsqrt(ms + NORM_EPS) * g_ref[...]).astype(BF16)

    def proj(a, b):
        return _dot(h, w_ref[:, a:b])

    cw = CONV_WIDTH
    bz_ref[:, 0:cw] = proj(0, cw).astype(BF16)
    bz_ref[:, cw:2 * cw] = (proj(cw, 2 * cw) * proj(2 * cw, 3 * cw)).astype(BF16)

    reps = DIFF_WIDTH // LANES
    cos = jnp.concatenate([cos_ref[...]] * reps, axis=1)
    sina = jnp.concatenate([sina_ref[...]] * reps, axis=1)
    sinb = jnp.concatenate([sinb_ref[...]] * reps, axis=1)
    half = DIFF_HEAD_DIM // 2

    def rope(t):
        return (t * cos + pltpu.roll(t, DIFF_WIDTH - half, axis=1) * sina
                + pltpu.roll(t, half, axis=1) * sinb)

    o = 3 * cw
    q_ref[...] = (rope(proj(o, o + DIFF_WIDTH)) * Q_SCALE).astype(BF16)
    o += DIFF_WIDTH
    k_ref[...] = rope(proj(o, o + DIFF_WIDTH)).astype(BF16)
    o += DIFF_WIDTH
    v = proj(o, o + DIFF_WIDTH).astype(BF16)
    hd = 2 * DIFF_HEAD_DIM
    ones = jnp.ones((v.shape[0], hd), BF16)
    for head in range(DIFF_HEADS):
        v_ref[:, 2 * head * hd:(2 * head + 1) * hd] = v[:, head * hd:(head + 1) * hd]
        v_ref[:, (2 * head + 1) * hd:(2 * head + 2) * hd] = ones
    o += DIFF_WIDTH
    hg_ref[...] = proj(o, D_PROJ)


def _rope_tables(seq):
    d = DIFF_HEAD_DIM
    inv = 1.0 / (ROPE_THETA ** (jnp.arange(0, d, 2, dtype=F32) / d))
    ang = jnp.arange(seq, dtype=F32)[:, None] * inv[None, :]
    ang = jnp.concatenate([ang, ang, ang, ang], axis=-1)
    cos, sin = jnp.cos(ang), jnp.sin(ang)
    first = (jnp.arange(LANES) % d) < (d // 2)
    sina = jnp.where(first[None, :], -sin, 0.0)
    sinb = jnp.where(first[None, :], 0.0, sin)
    return cos, sina, sinb


def _inproj(x2, g, w_bf16, tables, seq, tm):
    n = x2.shape[0]
    nblk_s = seq // tm
    cos, sina, sinb = tables
    tab_spec = pl.BlockSpec((tm, LANES), lambda i: (i % nblk_s, 0))
    row = lambda w: pl.BlockSpec((tm, w), lambda i: (i, 0))
    hgw = 5 * HG_WIDTH
    return pl.pallas_call(
        _inproj_kernel,
        grid=(n // tm,),
        in_specs=[row(D_MODEL),
                  pl.BlockSpec((1, D_MODEL), lambda i: (0, 0)),
                  pl.BlockSpec((D_MODEL, D_PROJ), lambda i: (0, 0)),
                  tab_spec, tab_spec, tab_spec],
        out_specs=[row(2 * CONV_WIDTH), row(DIFF_WIDTH), row(DIFF_WIDTH), row(2 * DIFF_WIDTH), row(hgw)],
        out_shape=[jax.ShapeDtypeStruct((n, 2 * CONV_WIDTH), BF16),
                   jax.ShapeDtypeStruct((n, DIFF_WIDTH), BF16),
                   jax.ShapeDtypeStruct((n, DIFF_WIDTH), BF16),
                   jax.ShapeDtypeStruct((n, 2 * DIFF_WIDTH), BF16),
                   jax.ShapeDtypeStruct((n, hgw), F32)],
        compiler_params=_params("parallel"),
        name="inproj",
    )(x2, g.reshape(1, D_MODEL), w_bf16, cos, sina, sinb)


def _attn_kernel(q_ref, k_ref, v_ref, lam_ref, g_ref, o_ref, *, tk, lam_init):
    q = q_ref[0]
    tq = q.shape[0]
    seq = k_ref.shape[1]
    nblk = seq // tk
    d = DIFF_HEAD_DIM
    hd = 2 * d
    lane = lax.broadcasted_iota(I32, q.shape, 1)
    zero = jnp.zeros_like(q)
    qs = (jnp.where(lane < d, q, zero), jnp.where(lane >= d, q, zero))

    def scores(j):
        ks = k_ref[0, j * tk:(j + 1) * tk, :]
        out = []
        for h in range(2):
            s = _dot_nt(qs[h], ks)
            out.append((s, jnp.max(s, axis=1, keepdims=True)))
        return out

    def consume(j, sc, state):
        vs = v_ref[0, j * tk:(j + 1) * tk, :]
        out = []
        for h in range(2):
            m, acc = state[h]
            s, bm = sc[h]
            mn = jnp.maximum(m, bm)
            p = jnp.exp2(s - mn).astype(BF16)
            acc = jnp.exp2(m - mn) * acc + _dot(p, vs)
            out.append((mn, acc))
        return out

    init = (jnp.full((tq, 1), -1e30, F32), jnp.zeros((tq, 2 * hd), F32))
    state = [init, init]
    sc = scores(0)
    for j in range(nblk):
        sc_next = scores(j + 1) if j + 1 < nblk else None
        state = consume(j, sc, state)
        sc = sc_next
    (_, a1), (_, a2) = state

    lp = lam_ref[...]
    lam = (jnp.exp(jnp.sum(lp[0:1] * lp[1:2], axis=1, keepdims=True))
           - jnp.exp(jnp.sum(lp[2:3] * lp[3:4], axis=1, keepdims=True)) + lam_init)
    o = a1[:, :hd] / a1[:, hd:] - lam * (a2[:, :hd] / a2[:, hd:])
    ms = jnp.mean(o * o, axis=-1, keepdims=True)
    y = o * lax.rsqrt(ms + NORM_EPS) * g_ref[...] * (1.0 - lam_init)
    o_ref[0] = y.astype(BF16)


def _attention(q3, k3, v3, lam_params, subln_g, layer, tq, tk):
    bsz, seq, _ = q3.shape
    lam_init = 0.8 - 0.6 * math.exp(-0.3 * layer)
    hd = 2 * DIFF_HEAD_DIM
    return pl.pallas_call(
        functools.partial(_attn_kernel, tk=tk, lam_init=lam_init),
        grid=(bsz, DIFF_HEADS, seq // tq),
        in_specs=[pl.BlockSpec((1, tq, hd), lambda b, h, i: (b, i, h)),
                  pl.BlockSpec((1, seq, hd), lambda b, h, i: (b, 0, h)),
                  pl.BlockSpec((1, seq, 2 * hd), lambda b, h, i: (b, 0, h)),
                  pl.BlockSpec((4, DIFF_HEAD_DIM), lambda b, h, i: (0, 0)),
                  pl.BlockSpec((1, hd), lambda b, h, i: (0, 0))],
        out_specs=pl.BlockSpec((1, tq, hd), lambda b, h, i: (b, i, h)),
        out_shape=jax.ShapeDtypeStruct((bsz, seq, DIFF_WIDTH), BF16),
        compiler_params=_params("parallel", "parallel", "parallel"),
        name="attn",
    )(q3, k3, v3, lam_params.astype(F32), subln_g.reshape(1, hd).astype(F32))


def _hgrn_kernel(qf_ref, ff_ref, if_ref, qb_ref, fb_ref, ib_ref, lbp_ref,
                 of_ref, ob_ref, stf_ref, stb_ref, *, layer):
    c_len = HG_CHUNK
    w = HG_WIDTH
    hd = HG_HEAD_DIM
    t_len = qf_ref.shape[1]
    nchunk = t_len // c_len

    @pl.when(pl.program_id(1) == 0)
    def _():
        stf_ref[...] = jnp.zeros_like(stf_ref)
        stb_ref[...] = jnp.zeros_like(stb_ref)

    lbp = lbp_ref[...]
    e = jnp.exp(lbp - jnp.max(lbp, axis=0, keepdims=True))
    sm = e / jnp.sum(e, axis=0, keepdims=True)
    lb = jnp.zeros((1, 2 * w), F32)
    for l in range(1, layer + 1):
        lb = lb + sm[l:l + 1]
    lb_f, lb_b = lb[:, :w], lb[:, w:]

    r = lax.broadcasted_iota(I32, (c_len, c_len), 0)
    c = lax.broadcasted_iota(I32, (c_len, c_len), 1)
    low = r >= c
    upp = r <= c
    low_m = jnp.where(low, 1.0, 0.0).astype(BF16)
    upp_m = jnp.where(upp, 1.0, 0.0).astype(BF16)
    lane = lax.broadcasted_iota(I32, (1, w), 1)
    heads = [(lane >= h * hd) & (lane < (h + 1) * hd) for h in range(HG_HEADS)]
    br = lax.broadcasted_iota(I32, (w, w), 0) // hd
    bc = lax.broadcasted_iota(I32, (w, w), 1) // hd
    same_head = br == bc
    eye = jnp.where(lax.broadcasted_iota(I32, (w, w), 0) == lax.broadcasted_iota(I32, (w, w), 1),
                    1.0, 0.0).astype(BF16)
    mid = c_len // 2

    def bmm(a, b):
        return jnp.einsum('nik,nkj->nij', a, b, preferred_element_type=F32)

    def bmm_nt(a, b):
        return jnp.einsum('nik,njk->nij', a, b, preferred_element_type=F32)

    def scan_block(xq, xf, xi, lbd, tri_m, tri, last, order, st_ref):
        c3 = lambda a: a.reshape(nchunk, c_len, w)
        qh = c3(xq * _sigmoid(xq) * (hd ** -0.5))
        f = lbd + (1.0 - lbd) * _sigmoid(xf)
        kk = c3(1.0 - f)
        tri_b = jnp.broadcast_to(tri_m[None], (nchunk, c_len, c_len))
        g_hi, g_mid, g_lo = _split3(c3(jnp.log(f)))
        G = bmm(tri_b, g_hi) + bmm(tri_b, g_mid) + bmm(tri_b, g_lo)
        g_ref_row = G[:, mid:mid + 1, :]
        g_last = G[:, last:last + 1, :]
        qt = (qh * jnp.exp(G - g_ref_row)).astype(BF16)
        kt = (kk * jnp.exp(g_ref_row - G)).astype(BF16)
        q_in = (qh * jnp.exp(G)).astype(BF16)
        k_out = (kk * jnp.exp(g_last - G)).astype(BF16)
        vb = c3(xi).astype(BF16)
        zq = jnp.zeros_like(qt)
        o = jnp.zeros((nchunk, c_len, w), F32)
        for h in range(HG_HEADS):
            a = bmm_nt(jnp.where(heads[h][None], qt, zq), kt)
            a = jnp.where(tri[None], a, 0.0).astype(BF16)
            o = o + jnp.where(heads[h][None], bmm(a, vb), 0.0)
        eye_b = jnp.broadcast_to(eye[None], (nchunk, w, w))
        v_t = bmm_nt(eye_b, vb).astype(BF16)
        upd = jnp.where(same_head[None], bmm(v_t, k_out), 0.0)
        decay = jnp.exp(g_last)
        st = st_ref[...]
        inter = [None] * nchunk
        for ci in order:
            inter[ci] = _dot_nt(q_in[ci], st.astype(BF16))
            st = st * decay[ci] + upd[ci]
        st_ref[...] = st
        return o.reshape(nchunk * c_len, w) + jnp.concatenate(inter, axis=0)

    fwd_order = list(range(nchunk))
    of_ref[0] = scan_block(qf_ref[0], ff_ref[0], if_ref[0], lb_f, low_m, low, c_len - 1,
                           fwd_order, stf_ref)
    ob_ref[0] = scan_block(qb_ref[0], fb_ref[0], ib_ref[0], lb_b, upp_m, upp, 0,
                           fwd_order[::-1], stb_ref)


def _hgrn(hg3, hgrn_lb, layer, t_len):
    bsz, seq, _ = hg3.shape
    nt = seq // t_len
    w = HG_WIDTH
    fwd = lambda col: pl.BlockSpec((1, t_len, w), lambda b, j: (b, j, col))
    bwd = lambda col: pl.BlockSpec((1, t_len, w), lambda b, j: (b, nt - 1 - j, col))
    depth = hgrn_lb.shape[0]
    return pl.pallas_call(
        functools.partial(_hgrn_kernel, layer=layer),
        grid=(bsz, nt),
        in_specs=[fwd(0), fwd(1), fwd(3), bwd(0), bwd(2), bwd(3),
                  pl.BlockSpec((depth, 2 * w), lambda b, j: (0, 0))],
        out_specs=[pl.BlockSpec((1, t_len, w), lambda b, j: (b, j, 0)),
                   pl.BlockSpec((1, t_len, w), lambda b, j: (b, nt - 1 - j, 0))],
        out_shape=[jax.ShapeDtypeStruct((bsz, seq, w), F32),
                   jax.ShapeDtypeStruct((bsz, seq, w), F32)],
        scratch_shapes=[pltpu.VMEM((w, w), F32), pltpu.VMEM((w, w), F32)],
        compiler_params=_params("parallel", "arbitrary"),
        name="hgrn",
    )(hg3, hg3, hg3, hg3, hg3, hg3, hgrn_lb.astype(F32))


def _outproj_kernel(x_ref, bz_ref, zp_ref, zn_ref, yd_ref, of_ref, ob_ref, hgate_ref,
                    cw_ref, hn_ref, wo_ref, nf_ref, rw_ref,
                    x1_ref, h2_ref, aff_ref, *, blocks_per_seq):
    tm = x_ref.shape[0]
    cwid = CONV_WIDTH
    i = pl.program_id(0)
    pos = i % blocks_per_seq
    halo = zp_ref.shape[0]

    b = bz_ref[:, 0:cwid].astype(F32)
    z = bz_ref[:, cwid:2 * cwid].astype(F32)
    zprev = zp_ref[:, cwid:2 * cwid].astype(F32)[halo - 1:halo]
    znext = zn_ref[:, cwid:2 * cwid].astype(F32)[0:1]
    zprev = jnp.where(pos == 0, 0.0, zprev)
    znext = jnp.where(pos == blocks_per_seq - 1, 0.0, znext)
    row = lax.broadcasted_iota(I32, (tm, cwid), 0)
    z_up = jnp.where(row == 0, zprev, pltpu.roll(z, 1, axis=0))
    z_dn = jnp.where(row == tm - 1, znext, pltpu.roll(z, tm - 1, axis=0))
    cw = cw_ref[...]
    y_conv = b * (cw[0:1] * z_up + cw[1:2] * z + cw[2:3] * z_dn)

    o = of_ref[...] + ob_ref[...]
    w = HG_WIDTH
    br = lax.broadcasted_iota(I32, (w, w), 0) // HG_HEAD_DIM
    bc = lax.broadcasted_iota(I32, (w, w), 1) // HG_HEAD_DIM
    pool = jnp.where(br == bc, 1.0, 0.0).astype(BF16)
    sq = o * o
    sq_hi = sq.astype(BF16)
    sq_lo = (sq - sq_hi.astype(F32)).astype(BF16)
    ms = (_dot(sq_hi, pool) + _dot(sq_lo, pool)) * (1.0 / HG_HEAD_DIM)
    gt = hgate_ref[...]
    y_h = o * lax.rsqrt(ms + NORM_EPS) * hn_ref[...] * (gt * _sigmoid(gt))

    mixed = (_dot(y_conv.astype(BF16), wo_ref[0:cwid, :])
             + _dot(yd_ref[...], wo_ref[cwid:cwid + DIFF_WIDTH, :])
             + _dot(y_h.astype(BF16), wo_ref[cwid + DIFF_WIDTH:, :]))
    x1 = x_ref[...] + mixed
    x1_ref[...] = x1

    ms2 = jnp.mean(x1 * x1, axis=-1, keepdims=True)
    h2 = x1 * lax.rsqrt(ms2 + NORM_EPS) * nf_ref[...]
    h_hi = h2.astype(BF16)
    h2_ref[...] = h_hi

    h_lo = (h2 - h_hi.astype(F32)).astype(BF16)
    rw = rw_ref[...]
    d_hi = _dot(h_hi, rw)
    d_lo = _dot(h_lo, rw)
    ne = N_EXPERTS
    logits = d_hi[:, :ne] + (d_hi[:, ne:] + d_lo[:, :ne]) + d_lo[:, ne:]
    ex = jnp.exp(logits - jnp.max(logits, axis=-1, keepdims=True))
    aff_ref[...] = ex / jnp.sum(ex, axis=-1, keepdims=True)


def _outproj(x2, bz, ydiff, of2, ob2, hg2, conv_w, hgrn_norm, wo_bf16, norm_ffn, rw3, seq, tm):
    n = x2.shape[0]
    halo = 16
    hb = tm // halo
    nhalo = n // halo
    row = lambda w: pl.BlockSpec((tm, w), lambda i: (i, 0))
    full = lambda a, b: pl.BlockSpec((a, b), lambda i: (0, 0))
    hn = jnp.tile(hgrn_norm.astype(F32), HG_HEADS).reshape(1, HG_WIDTH)
    return pl.pallas_call(
        functools.partial(_outproj_kernel, blocks_per_seq=seq // tm),
        grid=(n // tm,),
        in_specs=[row(D_MODEL), row(2 * CONV_WIDTH),
                  pl.BlockSpec((halo, 2 * CONV_WIDTH), lambda i: (jnp.maximum(i * hb - 1, 0), 0)),
                  pl.BlockSpec((halo, 2 * CONV_WIDTH), lambda i: (jnp.minimum((i + 1) * hb, nhalo - 1), 0)),
                  row(DIFF_WIDTH), row(HG_WIDTH), row(HG_WIDTH),
                  pl.BlockSpec((tm, HG_WIDTH), lambda i: (i, 4)),
                  full(3, CONV_WIDTH), full(1, HG_WIDTH), full(D_MODEL, D_MODEL), full(1, D_MODEL),
                  full(D_MODEL, 2 * N_EXPERTS)],
        out_specs=[row(D_MODEL), row(D_MODEL), row(N_EXPERTS)],
        out_shape=[jax.ShapeDtypeStruct((n, D_MODEL), F32),
                   jax.ShapeDtypeStruct((n, D_MODEL), BF16),
                   jax.ShapeDtypeStruct((n, N_EXPERTS), F32)],
        compiler_params=_params("parallel"),
        name="outproj",
    )(x2, bz, bz, bz, ydiff, of2, ob2, hg2, conv_w.astype(F32), hn, wo_bf16,
      norm_ffn.reshape(1, D_MODEL).astype(F32), rw3)


def _select_kernel(aff_ref, rankm_ref, rankx_ref, pre_ref, tot_ref, *, cap):
    nblk = aff_ref.shape[0]
    ne = N_EXPERTS
    bits = pltpu.bitcast(aff_ref[...], I32)

    def count(mask):
        part = jnp.sum(jnp.where(mask, 1.0, 0.0), axis=0)
        return jnp.sum(part, axis=1, keepdims=True)

    def search(i, tau):
        cand = tau | jnp.left_shift(jnp.int32(1), 30 - i)
        return jnp.where(count(bits >= cand[None]) >= cap, cand, tau)

    tau = lax.fori_loop(0, 31, search, jnp.zeros((ne, 1), I32))[None]

    r = lax.broadcasted_iota(I32, (LANES, LANES), 0)
    c = lax.broadcasted_iota(I32, (LANES, LANES), 1)
    incl = jnp.where(r <= c, 1.0, 0.0).astype(BF16)
    ones = jnp.ones((LANES, LANES), BF16)

    def prefix(mask):
        m = jnp.where(mask, 1.0, 0.0)
        m2 = m.reshape(nblk * ne, LANES).astype(BF16)
        pre_ref[...] = _dot(m2, incl).reshape(nblk, ne, LANES) - m
        tot_ref[...] = _dot(m2, ones).reshape(nblk, ne, LANES)

        def carry(b, run):
            pre_ref[b] = pre_ref[b] + run
            return run + tot_ref[b]

        lax.fori_loop(0, nblk, carry, jnp.zeros((ne, LANES), F32))
        return pre_ref[...]

    above = bits > tau
    tie = bits == tau
    need = cap - count(above)
    sel = above | (tie & (prefix(tie) < need[None]))
    rank = prefix(sel).astype(I32)
    rankx_ref[...] = rank
    rankm_ref[...] = jnp.where(sel, rank, -1)


def _select(aff3, cap):
    nblk = aff3.shape[0]
    shp = (nblk, N_EXPERTS, LANES)
    spec = pl.BlockSpec(shp, lambda i: (0, 0, 0))
    return pl.pallas_call(
        functools.partial(_select_kernel, cap=cap),
        grid=(1,),
        in_specs=[spec],
        out_specs=[spec, spec],
        out_shape=[jax.ShapeDtypeStruct(shp, I32), jax.ShapeDtypeStruct(shp, I32)],
        scratch_shapes=[pltpu.VMEM(shp, F32), pltpu.VMEM(shp, F32)],
        compiler_params=_params("arbitrary"),
        name="select",
    )(aff3)


def _expert_kernel(base_ref, rank_ref, h_ref, wg_ref, wu_ref, wd_ref, ye_ref, xe_ref,
                   *, nt, tt, sj, cap, win):
    e = pl.program_id(0)
    s = pl.program_id(1)

    @pl.when(s == 0)
    def _():
        xe_ref[...] = jnp.zeros_like(xe_ref)

    @pl.when(s < nt)
    def _():
        base = base_ref[e * (nt + 1) + s]
        count = base_ref[e * (nt + 1) + s + 1] - base
        wb = (base // 8) * 8
        nwin = jnp.where(count > 0, (base - wb + count + win - 1) // win, 0)
        rk = rank_ref[0, 0]
        rel = jnp.concatenate([rk[a:a + 1, :] for a in range(tt // LANES)], axis=1) - wb
        slot = lax.broadcasted_iota(I32, (win, tt), 0)

        def fill(w, carry):
            onehot = jnp.where(slot == rel - w * win, 1.0, 0.0).astype(BF16)
            r0 = pl.multiple_of(wb + w * win, 8)
            xe_ref[pl.ds(r0, win), :] += _dot(onehot, h_ref[...])
            return carry

        lax.fori_loop(0, nwin, fill, 0)

    @pl.when(s >= nt)
    def _():
        j0 = pl.multiple_of((s - nt) * sj, sj)
        xj = xe_ref[pl.ds(j0, sj), :].astype(BF16)
        a = _dot(xj, wg_ref[0])
        u = _dot(xj, wu_ref[0])
        hid = (a * _sigmoid(a) * u).astype(BF16)
        ye_ref[0] = _dot(hid, wd_ref[0]).astype(BF16)


def _experts(base_flat, rank4, h2, wg, wu, wd, cap, tt, sj):
    n = h2.shape[0]
    nt = n // tt
    nj = cap // sj
    win = GATHER_ROWS
    wspec = pl.BlockSpec((1, D_MODEL, D_MODEL), lambda e, s, b: (e, 0, 0))
    grid_spec = pltpu.PrefetchScalarGridSpec(
        num_scalar_prefetch=1,
        grid=(N_EXPERTS, nt + nj),
        in_specs=[pl.BlockSpec((1, 1, tt // LANES, LANES),
                               lambda e, s, b: (e, jnp.minimum(s, nt - 1), 0, 0)),
                  pl.BlockSpec((tt, D_MODEL), lambda e, s, b: (jnp.minimum(s, nt - 1), 0)),
                  wspec, wspec, wspec],
        out_specs=pl.BlockSpec((1, sj, D_MODEL), lambda e, s, b: (e, jnp.maximum(s - nt, 0), 0)),
        scratch_shapes=[pltpu.VMEM((cap + win + 8, D_MODEL), F32)],
    )
    return pl.pallas_call(
        functools.partial(_expert_kernel, nt=nt, tt=tt, sj=sj, cap=cap, win=win),
        grid_spec=grid_spec,
        out_shape=jax.ShapeDtypeStruct((N_EXPERTS, cap, D_MODEL), BF16),
        compiler_params=_params("arbitrary", "arbitrary"),
        name="experts",
    )(base_flat, rank4, h2, wg, wu, wd)


def _main_start(base, cap, w1):
    return pl.multiple_of(jnp.minimum((base // 16) * 16, cap - w1), 16)


def _over_start(base, nxt, cap, w1, w2):
    wb = _main_start(base, cap, w1)
    need = nxt - wb > w1
    return need, pl.multiple_of(jnp.where(need, jnp.minimum(wb + w1, cap - w2), 0), 16)


def _combine_kernel(base_ref, x_ref, rank_ref, gate_ref, nrm_ref, *rest, nt, cap, w1, w2, final):
    ne = N_EXPERTS
    main_refs = rest[:ne]
    over_refs = rest[ne:2 * ne]
    out_ref = rest[2 * ne]
    t = pl.program_id(0)
    tt = x_ref.shape[0]
    rank = rank_ref[...]
    gate = gate_ref[...]
    bases = [base_ref[e * (nt + 1) + t] for e in range(ne)]
    nexts = [base_ref[e * (nt + 1) + t + 1] for e in range(ne)]

    col = lax.broadcasted_iota(I32, (1, ne), 1)
    wb_row = jnp.zeros((1, ne), I32)
    for e in range(ne):
        wb_row = jnp.where(col == e, _main_start(bases[e], cap, w1), wb_row)
    rel = rank - wb_row
    adj = jnp.where((rank >= 0) & (rel < w1), rel, -1)
    g_hi = gate.astype(BF16).astype(F32)
    g_lo = gate - g_hi
    per_vreg = LANES // w1
    lane = lax.broadcasted_iota(I32, (tt, LANES), 1)
    zero = jnp.zeros((tt, LANES), F32)
    p_hi, p_lo = [], []
    for i in range(ne // per_vreg):
        target = jnp.full((tt, LANES), -1, I32)
        hi = zero
        lo = zero
        for a in range(per_vreg):
            e = i * per_vreg + a
            inside = (lane >= a * w1) & (lane < (a + 1) * w1)
            target = jnp.where(inside, adj[:, e:e + 1] + a * w1, target)
            hi = jnp.where(inside, g_hi[:, e:e + 1], hi)
            lo = jnp.where(inside, g_lo[:, e:e + 1], lo)
        match = lane == target
        p_hi.append(jnp.where(match, hi, zero).astype(BF16))
        p_lo.append(jnp.where(match, lo, zero).astype(BF16))
    y_all = jnp.concatenate([r[...] for r in main_refs], axis=0)
    out_ref[...] = (x_ref[...] + _dot(jnp.concatenate(p_hi, axis=1), y_all)
                    + _dot(jnp.concatenate(p_lo, axis=1), y_all))

    overs = [_over_start(bases[e], nexts[e], cap, w1, w2) for e in range(ne)]
    any_need = overs[0][0]
    for e in range(1, ne):
        any_need = any_need | overs[e][0]

    @pl.when(any_need)
    def _():
        lane2 = lax.broadcasted_iota(I32, (tt, w2), 1)
        acc = jnp.zeros((tt, D_MODEL), F32)
        for e in range(ne):
            need, wo = overs[e]
            first = _main_start(bases[e], cap, w1) + w1
            r = rank[:, e:e + 1]
            match = (lane2 == r - wo) & (r >= first) & need
            onehot = jnp.where(match, 1.0, 0.0).astype(BF16)
            acc = acc + gate[:, e:e + 1] * _dot(onehot, over_refs[e][...])
        out_ref[...] += acc

    if final:
        o = out_ref[...]
        ms = jnp.mean(o * o, axis=-1, keepdims=True)
        out_ref[...] = o * lax.rsqrt(ms + NORM_EPS) * nrm_ref[...]


def _combine(base_flat, x1, rank_tok, gate, ye, nrm, cap, tt, final):
    n = x1.shape[0]
    nt = n // tt
    w1 = COMBINE_MAIN_ROWS
    w2 = tt + 16 - w1
    assert cap >= max(w1, w2) and cap % 16 == 0 and LANES % w1 == 0
    row = lambda w: pl.BlockSpec((tt, w), lambda t, b: (t, 0))

    def main_spec(e):
        return pl.BlockSpec(
            (pl.Squeezed(), pl.Element(w1), pl.Element(D_MODEL)),
            lambda t, b: (e, _main_start(b[e * (nt + 1) + t], cap, w1), 0))

    def over_spec(e):
        return pl.BlockSpec(
            (pl.Squeezed(), pl.Element(w2), pl.Element(D_MODEL)),
            lambda t, b: (e, _over_start(b[e * (nt + 1) + t], b[e * (nt + 1) + t + 1], cap, w1, w2)[1], 0))

    grid_spec = pltpu.PrefetchScalarGridSpec(
        num_scalar_prefetch=1,
        grid=(nt,),
        in_specs=[row(D_MODEL), row(N_EXPERTS), row(N_EXPERTS),
                  pl.BlockSpec((1, D_MODEL), lambda t, b: (0, 0))]
                 + [main_spec(e) for e in range(N_EXPERTS)]
                 + [over_spec(e) for e in range(N_EXPERTS)],
        out_specs=row(D_MODEL),
    )
    return pl.pallas_call(
        functools.partial(_combine_kernel, nt=nt, cap=cap, w1=w1, w2=w2, final=final),
        grid_spec=grid_spec,
        out_shape=jax.ShapeDtypeStruct((n, D_MODEL), F32),
        compiler_params=_params("arbitrary"),
        name="combine",
    )(base_flat, x1, rank_tok, gate, nrm.reshape(1, D_MODEL).astype(F32), *([ye] * (2 * N_EXPERTS)))


def _tile(n, want):
    t = min(n, want)
    assert n % t == 0, (n, want)
    return t


def _routed_ffn(x1, h2, aff, wg, wu, wd, nrm, final):
    n = x1.shape[0]
    cap = (EC_CAPACITY_FACTOR * n) // N_EXPERTS
    nblk = n // LANES
    tt_e = _tile(n, 1024)
    sj = _tile(cap, 512)
    tt_c = _tile(n, 128)

    aff3 = aff.reshape(nblk, LANES, N_EXPERTS).transpose(0, 2, 1)
    rankm3, rankx3 = _select(aff3, cap)
    rank_tok = rankm3.transpose(0, 2, 1).reshape(n, N_EXPERTS)
    rank4 = rankm3.transpose(1, 0, 2).reshape(N_EXPERTS, n // tt_e, tt_e // LANES, LANES)
    base128 = rankx3[:, :, 0].T
    last = jnp.full((N_EXPERTS, 1), cap, I32)
    base_e = jnp.concatenate([base128[:, ::tt_e // LANES], last], axis=1).reshape(-1)
    base_c = jnp.concatenate([base128[:, ::tt_c // LANES], last], axis=1).reshape(-1)

    ye = _experts(base_e, rank4, h2, wg, wu, wd, cap, tt_e, sj)
    return _combine(base_c, x1, rank_tok, aff, ye, nrm, cap, tt_c, final)


def _trunk(x, p):
    bsz, seq, _ = x.shape
    n = bsz * seq
    depth = p["w_in"].shape[0]
    tm = _tile(seq, 512)
    tq = _tile(seq, 512)
    tk = _tile(seq, 2048)
    t_hg = _tile(seq, 512)
    tables = _rope_tables(seq)

    x2 = x.reshape(n, D_MODEL)
    for l in range(depth):
        bz, q, k, v, hg = _inproj(x2, p["norm_mix"][l], p["w_in"][l], tables, seq, tm)
        to3 = lambda a: a.reshape(bsz, seq, a.shape[-1])
        ydiff = _attention(to3(q), to3(k), to3(v), p["diff_lambda"][l], p["diff_subln"][l], l, tq, tk)
        o_f, o_b = _hgrn(to3(hg), p["hgrn_lb"], l, t_hg)
        x1, h2, aff = _outproj(x2, bz, ydiff.reshape(n, DIFF_WIDTH), o_f.reshape(n, HG_WIDTH),
                               o_b.reshape(n, HG_WIDTH), hg, p["conv_w"][l], p["hgrn_norm"][l],
                               p["w_out"][l], p["norm_ffn"][l], p["router_w"][l], seq, tm)
        x2 = _routed_ffn(x1, h2, aff, p["w_gate"][l], p["w_up"][l], p["w_down"][l],
                         p["norm_final"], l == depth - 1)
    return x2.reshape(bsz, seq, D_MODEL)


def _split2_param(w):
    w = w.astype(F32)
    hi = w.astype(BF16)
    lo = (w - hi.astype(F32)).astype(BF16)
    return jnp.concatenate([hi, lo], axis=-1)


def kernel(x_prompt, x_sample, norm_mix, w_in, conv_w, diff_lambda, diff_subln, hgrn_lb,
           hgrn_norm, w_out, norm_ffn, router_w, w_gate, w_up, w_down, norm_final):
    p = dict(
        norm_mix=norm_mix.astype(F32), w_in=w_in.astype(BF16), conv_w=conv_w,
        diff_lambda=diff_lambda, diff_subln=diff_subln, hgrn_lb=hgrn_lb, hgrn_norm=hgrn_norm,
        w_out=w_out.astype(BF16), norm_ffn=norm_ffn,
        router_w=_split2_param(router_w),
        w_gate=w_gate.astype(BF16), w_up=w_up.astype(BF16), w_down=w_down.astype(BF16),
        norm_final=norm_final,
    )
    return _trunk(x_prompt, p), _trunk(x_sample, p)
```

```python
import functools
import math

import jax
import jax.numpy as jnp
from jax import lax
from jax.experimental import pallas as pl
from jax.experimental.pallas import tpu as pltpu

F32 = jnp.float32
BF16 = jnp.bfloat16
I32 = jnp.int32

D_MODEL = 1024
CONV_WIDTH = 256
DIFF_WIDTH = 512
DIFF_HEADS = 4
DIFF_HEAD_DIM = 64
HG_WIDTH = 256
HG_HEADS = 4
HG_HEAD_DIM = 64
HG_CHUNK = 64
D_PROJ = 3584
ROPE_THETA = 10000.0
N_EXPERTS = 16
EC_CAPACITY_FACTOR = 2
NORM_EPS = 1e-6

VMEM_LIMIT_BYTES = 56 * 1024 * 1024
LANES = 128
GATHER_ROWS = 256
COMBINE_MAIN_ROWS = 64

NT_DIMS = (((1,), (1,)), ((), ()))
Q_SCALE = DIFF_HEAD_DIM ** -0.5 * math.log2(math.e)


def _params(*sem):
    return pltpu.CompilerParams(dimension_semantics=sem, vmem_limit_bytes=VMEM_LIMIT_BYTES)


def _split3(x):
    hi = x.astype(BF16)
    r1 = x - hi.astype(F32)
    mid = r1.astype(BF16)
    lo = (r1 - mid.astype(F32)).astype(BF16)
    return hi, mid, lo


def _dot(a, b):
    return jnp.dot(a, b, preferred_element_type=F32)


def _dot_nt(a, b):
    return lax.dot_general(a, b, NT_DIMS, preferred_element_type=F32)


def _dot_exact_lhs(a_bf16, x_f32):
    hi, mid, lo = _split3(x_f32)
    return _dot(a_bf16, hi) + _dot(a_bf16, mid) + _dot(a_bf16, lo)


def _sigmoid(x):
    return 1.0 / (1.0 + jnp.exp(-x))


def _inproj_kernel(x_ref, g_ref, w_ref, cos_ref, sina_ref, sinb_ref,
                   bz_ref, q_ref, k_ref, v_ref, hg_ref):
    x = x_ref[...]
    ms = jnp.mean(x * x, axis=-1, keepdims=True)
    h = (x * lax.rsqrt(ms + NORM_EPS) * g_ref[...]).astype(BF16)

    def proj(a, b):
        return _dot(h, w_ref[:, a:b])

    cw = CONV_WIDTH
    bz_ref[:, 0:cw] = proj(0, cw).astype(BF16)
    bz_ref[:, cw:2 * cw] = (proj(cw, 2 * cw) * proj(2 * cw, 3 * cw)).astype(BF16)

    reps = DIFF_WIDTH // LANES
    cos = jnp.concatenate([cos_ref[...]] * reps, axis=1)
    sina = jnp.concatenate([sina_ref[...]] * reps, axis=1)
    sinb = jnp.concatenate([sinb_ref[...]] * reps, axis=1)
    half = DIFF_HEAD_DIM // 2

    def rope(t):
        return (t * cos + pltpu.roll(t, DIFF_WIDTH - half, axis=1) * sina
                + pltpu.roll(t, half, axis=1) * sinb)

    o = 3 * cw
    q_ref[...] = (rope(proj(o, o + DIFF_WIDTH)) * Q_SCALE).astype(BF16)
    o += DIFF_WIDTH
    k_ref[...] = rope(proj(o, o + DIFF_WIDTH)).astype(BF16)
    o += DIFF_WIDTH
    v = proj(o, o + DIFF_WIDTH).astype(BF16)
    hd = 2 * DIFF_HEAD_DIM
    ones = jnp.ones((v.shape[0], hd), BF16)
    for head in range(DIFF_HEADS):
        v_ref[:, 2 * head * hd:(2 * head + 1) * hd] = v[:, head * hd:(head + 1) * hd]
        v_ref[:, (2 * head + 1) * hd:(2 * head + 2) * hd] = ones
    o += DIFF_WIDTH
    hg_ref[...] = proj(o, D_PROJ)


def _rope_tables(seq):
    d = DIFF_HEAD_DIM
    inv = 1.0 / (ROPE_THETA ** (jnp.arange(0, d, 2, dtype=F32) / d))
    ang = jnp.arange(seq, dtype=F32)[:, None] * inv[None, :]
    ang = jnp.concatenate([ang, ang, ang, ang], axis=-1)
    cos, sin = jnp.cos(ang), jnp.sin(ang)
    first = (jnp.arange(LANES) % d) < (d // 2)
    sina = jnp.where(first[None, :], -sin, 0.0)
    sinb = jnp.where(first[None, :], 0.0, sin)
    return cos, sina, sinb


def _inproj(x2, g, w_bf16, tables, seq, tm):
    n = x2.shape[0]
    nblk_s = seq // tm
    cos, sina, sinb = tables
    tab_spec = pl.BlockSpec((tm, LANES), lambda i: (i % nblk_s, 0))
    row = lambda w: pl.BlockSpec((tm, w), lambda i: (i, 0))
    hgw = 5 * HG_WIDTH
    return pl.pallas_call(
        _inproj_kernel,
        grid=(n // tm,),
        in_specs=[row(D_MODEL),
                  pl.BlockSpec((1, D_MODEL), lambda i: (0, 0)),
                  pl.BlockSpec((D_MODEL, D_PROJ), lambda i: (0, 0)),
                  tab_spec, tab_spec, tab_spec],
        out_specs=[row(2 * CONV_WIDTH), row(DIFF_WIDTH), row(DIFF_WIDTH), row(2 * DIFF_WIDTH), row(hgw)],
        out_shape=[jax.ShapeDtypeStruct((n, 2 * CONV_WIDTH), BF16),
                   jax.ShapeDtypeStruct((n, DIFF_WIDTH), BF16),
                   jax.ShapeDtypeStruct((n, DIFF_WIDTH), BF16),
                   jax.ShapeDtypeStruct((n, 2 * DIFF_WIDTH), BF16),
                   jax.ShapeDtypeStruct((n, hgw), F32)],
        compiler_params=_params("parallel"),
        name="inproj",
    )(x2, g.reshape(1, D_MODEL), w_bf16, cos, sina, sinb)


def _attn_kernel(q_ref, k_ref, v_ref, lam_ref, g_ref, o_ref, *, tk, lam_init):
    q = q_ref[0]
    tq = q.shape[0]
    seq = k_ref.shape[1]
    nblk = seq // tk
    d = DIFF_HEAD_DIM
    hd = 2 * d
    lane = lax.broadcasted_iota(I32, q.shape, 1)
    zero = jnp.zeros_like(q)
    qs = (jnp.where(lane < d, q, zero), jnp.where(lane >= d, q, zero))

    def scores(j):
        ks = k_ref[0, j * tk:(j + 1) * tk, :]
        out = []
        for h in range(2):
            s = _dot_nt(qs[h], ks)
            out.append((s, jnp.max(s, axis=1, keepdims=True)))
        return out

    def consume(j, sc, state):
        vs = v_ref[0, j * tk:(j + 1) * tk, :]
        out = []
        for h in range(2):
            m, acc = state[h]
            s, bm = sc[h]
            mn = jnp.maximum(m, bm)
            p = jnp.exp2(s - mn).astype(BF16)
            acc = jnp.exp2(m - mn) * acc + _dot(p, vs)
            out.append((mn, acc))
        return out

    init = (jnp.full((tq, 1), -1e30, F32), jnp.zeros((tq, 2 * hd), F32))
    state = [init, init]
    sc = scores(0)
    for j in range(nblk):
        sc_next = scores(j + 1) if j + 1 < nblk else None
        state = consume(j, sc, state)
        sc = sc_next
    (_, a1), (_, a2) = state

    lp = lam_ref[...]
    lam = (jnp.exp(jnp.sum(lp[0:1] * lp[1:2], axis=1, keepdims=True))
           - jnp.exp(jnp.sum(lp[2:3] * lp[3:4], axis=1, keepdims=True)) + lam_init)
    o = a1[:, :hd] / a1[:, hd:] - lam * (a2[:, :hd] / a2[:, hd:])
    ms = jnp.mean(o * o, axis=-1, keepdims=True)
    y = o * lax.rsqrt(ms + NORM_EPS) * g_ref[...] * (1.0 - lam_init)
    o_ref[0] = y.astype(BF16)


def _attention(q3, k3, v3, lam_params, subln_g, layer, tq, tk):
    bsz, seq, _ = q3.shape
    lam_init = 0.8 - 0.6 * math.exp(-0.3 * layer)
    hd = 2 * DIFF_HEAD_DIM
    return pl.pallas_call(
        functools.partial(_attn_kernel, tk=tk, lam_init=lam_init),
        grid=(bsz, DIFF_HEADS, seq // tq),
        in_specs=[pl.BlockSpec((1, tq, hd), lambda b, h, i: (b, i, h)),
                  pl.BlockSpec((1, seq, hd), lambda b, h, i: (b, 0, h)),
                  pl.BlockSpec((1, seq, 2 * hd), lambda b, h, i: (b, 0, h)),
                  pl.BlockSpec((4, DIFF_HEAD_DIM), lambda b, h, i: (0, 0)),
                  pl.BlockSpec((1, hd), lambda b, h, i: (0, 0))],
        out_specs=pl.BlockSpec((1, tq, hd), lambda b, h, i: (b, i, h)),
        out_shape=jax.ShapeDtypeStruct((bsz, seq, DIFF_WIDTH), BF16),
        compiler_params=_params("parallel", "parallel", "parallel"),
        name="attn",
    )(q3, k3, v3, lam_params.astype(F32), subln_g.reshape(1, hd).astype(F32))


def _hgrn_kernel(qf_ref, ff_ref, if_ref, qb_ref, fb_ref, ib_ref, lbp_ref,
                 of_ref, ob_ref, stf_ref, stb_ref, *, layer):
    c_len = HG_CHUNK
    w = HG_WIDTH
    hd = HG_HEAD_DIM
    t_len = qf_ref.shape[1]
    nchunk = t_len // c_len

    @pl.when(pl.program_id(1) == 0)
    def _():
        stf_ref[...] = jnp.zeros_like(stf_ref)
        stb_ref[...] = jnp.zeros_like(stb_ref)

    lbp = lbp_ref[...]
    e = jnp.exp(lbp - jnp.max(lbp, axis=0, keepdims=True))
    sm = e / jnp.sum(e, axis=0, keepdims=True)
    lb = jnp.zeros((1, 2 * w), F32)
    for l in range(1, layer + 1):
        lb = lb + sm[l:l + 1]
    lb_f, lb_b = lb[:, :w], lb[:, w:]

    r = lax.broadcasted_iota(I32, (c_len, c_len), 0)
    c = lax.broadcasted_iota(I32, (c_len, c_len), 1)
    low = r >= c
    upp = r <= c
    low_m = jnp.where(low, 1.0, 0.0).astype(BF16)
    upp_m = jnp.where(upp, 1.0, 0.0).astype(BF16)
    lane = lax.broadcasted_iota(I32, (1, w), 1)
    heads = [(lane >= h * hd) & (lane < (h + 1) * hd) for h in range(HG_HEADS)]
    br = lax.broadcasted_iota(I32, (w, w), 0) // hd
    bc = lax.broadcasted_iota(I32, (w, w), 1) // hd
    same_head = br == bc
    eye = jnp.where(lax.broadcasted_iota(I32, (w, w), 0) == lax.broadcasted_iota(I32, (w, w), 1),
                    1.0, 0.0).astype(BF16)
    mid = c_len // 2

    def bmm(a, b):
        return jnp.einsum('nik,nkj->nij', a, b, preferred_element_type=F32)

    def bmm_nt(a, b):
        return jnp.einsum('nik,njk->nij', a, b, preferred_element_type=F32)

    def scan_block(xq, xf, xi, lbd, tri_m, tri, last, order, st_ref):
        c3 = lambda a: a.reshape(nchunk, c_len, w)
        qh = c3(xq * _sigmoid(xq) * (hd ** -0.5))
        f = lbd + (1.0 - lbd) * _sigmoid(xf)
        kk = c3(1.0 - f)
        tri_b = jnp.broadcast_to(tri_m[None], (nchunk, c_len, c_len))
        g_hi, g_mid, g_lo = _split3(c3(jnp.log(f)))
        G = bmm(tri_b, g_hi) + bmm(tri_b, g_mid) + bmm(tri_b, g_lo)
        g_ref_row = G[:, mid:mid + 1, :]
        g_last = G[:, last:last + 1, :]
        qt = (qh * jnp.exp(G - g_ref_row)).astype(BF16)
        kt = (kk * jnp.exp(g_ref_row - G)).astype(BF16)
        q_in = (qh * jnp.exp(G)).astype(BF16)
        k_out = (kk * jnp.exp(g_last - G)).astype(BF16)
        vb = c3(xi).astype(BF16)
        zq = jnp.zeros_like(qt)
        o = jnp.zeros((nchunk, c_len, w), F32)
        for h in range(HG_HEADS):
            a = bmm_nt(jnp.where(heads[h][None], qt, zq), kt)
            a = jnp.where(tri[None], a, 0.0).astype(BF16)
            o = o + jnp.where(heads[h][None], bmm(a, vb), 0.0)
        eye_b = jnp.broadcast_to(eye[None], (nchunk, w, w))
        v_t = bmm_nt(eye_b, vb).astype(BF16)
        upd = jnp.where(same_head[None], bmm(v_t, k_out), 0.0)
        decay = jnp.exp(g_last)
        st = st_ref[...]
        inter = [None] * nchunk
        for ci in order:
            inter[ci] = _dot_nt(q_in[ci], st.astype(BF16))
            st = st * decay[ci] + upd[ci]
        st_ref[...] = st
        return o.reshape(nchunk * c_len, w) + jnp.concatenate(inter, axis=0)

    fwd_order = list(range(nchunk))
    of_ref[0] = scan_block(qf_ref[0], ff_ref[0], if_ref[0], lb_f, low_m, low, c_len - 1,
                           fwd_order, stf_ref)
    ob_ref[0] = scan_block(qb_ref[0], fb_ref[0], ib_ref[0], lb_b, upp_m, upp, 0,
                           fwd_order[::-1], stb_ref)


def _hgrn(hg3, hgrn_lb, layer, t_len):
    bsz, seq, _ = hg3.shape
    nt = seq // t_len
    w = HG_WIDTH
    fwd = lambda col: pl.BlockSpec((1, t_len, w), lambda b, j: (b, j, col))
    bwd = lambda col: pl.BlockSpec((1, t_len, w), lambda b, j: (b, nt - 1 - j, col))
    depth = hgrn_lb.shape[0]
    return pl.pallas_call(
        functools.partial(_hgrn_kernel, layer=layer),
        grid=(bsz, nt),
        in_specs=[fwd(0), fwd(1), fwd(3), bwd(0), bwd(2), bwd(3),
                  pl.BlockSpec((depth, 2 * w), lambda b, j: (0, 0))],
        out_specs=[pl.BlockSpec((1, t_len, w), lambda b, j: (b, j, 0)),
                   pl.BlockSpec((1, t_len, w), lambda b, j: (b, nt - 1 - j, 0))],
        out_shape=[jax.ShapeDtypeStruct((bsz, seq, w), F32),
                   jax.ShapeDtypeStruct((bsz, seq, w), F32)],
        scratch_shapes=[pltpu.VMEM((w, w), F32), pltpu.VMEM((w, w), F32)],
        compiler_params=_params("parallel", "arbitrary"),
        name="hgrn",
    )(hg3, hg3, hg3, hg3, hg3, hg3, hgrn_lb.astype(F32))


def _outproj_kernel(x_ref, bz_ref, zp_ref, zn_ref, yd_ref, of_ref, ob_ref, hgate_ref,
                    cw_ref, hn_ref, wo_ref, nf_ref, rw_ref,
                    x1_ref, h2_ref, aff_ref, *, blocks_per_seq):
    tm = x_ref.shape[0]
    cwid = CONV_WIDTH
    i = pl.program_id(0)
    pos = i % blocks_per_seq
    halo = zp_ref.shape[0]

    b = bz_ref[:, 0:cwid].astype(F32)
    z = bz_ref[:, cwid:2 * cwid].astype(F32)
    zprev = zp_ref[:, cwid:2 * cwid].astype(F32)[halo - 1:halo]
    znext = zn_ref[:, cwid:2 * cwid].astype(F32)[0:1]
    zprev = jnp.where(pos == 0, 0.0, zprev)
    znext = jnp.where(pos == blocks_per_seq - 1, 0.0, znext)
    row = lax.broadcasted_iota(I32, (tm, cwid), 0)
    z_up = jnp.where(row == 0, zprev, pltpu.roll(z, 1, axis=0))
    z_dn = jnp.where(row == tm - 1, znext, pltpu.roll(z, tm - 1, axis=0))
    cw = cw_ref[...]
    y_conv = b * (cw[0:1] * z_up + cw[1:2] * z + cw[2:3] * z_dn)

    o = of_ref[...] + ob_ref[...]
    w = HG_WIDTH
    br = lax.broadcasted_iota(I32, (w, w), 0) // HG_HEAD_DIM
    bc = lax.broadcasted_iota(I32, (w, w), 1) // HG_HEAD_DIM
    pool = jnp.where(br == bc, 1.0, 0.0).astype(BF16)
    sq = o * o
    sq_hi = sq.astype(BF16)
    sq_lo = (sq - sq_hi.astype(F32)).astype(BF16)
    ms = (_dot(sq_hi, pool) + _dot(sq_lo, pool)) * (1.0 / HG_HEAD_DIM)
    gt = hgate_ref[...]
    y_h = o * lax.rsqrt(ms + NORM_EPS) * hn_ref[...] * (gt * _sigmoid(gt))

    mixed = (_dot(y_conv.astype(BF16), wo_ref[0:cwid, :])
             + _dot(yd_ref[...], wo_ref[cwid:cwid + DIFF_WIDTH, :])
             + _dot(y_h.astype(BF16), wo_ref[cwid + DIFF_WIDTH:, :]))
    x1 = x_ref[...] + mixed
    x1_ref[...] = x1

    ms2 = jnp.mean(x1 * x1, axis=-1, keepdims=True)
    h2 = x1 * lax.rsqrt(ms2 + NORM_EPS) * nf_ref[...]
    h_hi = h2.astype(BF16)
    h2_ref[...] = h_hi

    h_lo = (h2 - h_hi.astype(F32)).astype(BF16)
    rw = rw_ref[...]
    d_hi = _dot(h_hi, rw)
    d_lo = _dot(h_lo, rw)
    ne = N_EXPERTS
    logits = d_hi[:, :ne] + (d_hi[:, ne:] + d_lo[:, :ne]) + d_lo[:, ne:]
    ex = jnp.exp(logits - jnp.max(logits, axis=-1, keepdims=True))
    aff_ref[...] = ex / jnp.sum(ex, axis=-1, keepdims=True)


def _outproj(x2, bz, ydiff, of2, ob2, hg2, conv_w, hgrn_norm, wo_bf16, norm_ffn, rw3, seq, tm):
    n = x2.shape[0]
    halo = 16
    hb = tm // halo
    nhalo = n // halo
    row = lambda w: pl.BlockSpec((tm, w), lambda i: (i, 0))
    full = lambda a, b: pl.BlockSpec((a, b), lambda i: (0, 0))
    hn = jnp.tile(hgrn_norm.astype(F32), HG_HEADS).reshape(1, HG_WIDTH)
    return pl.pallas_call(
        functools.partial(_outproj_kernel, blocks_per_seq=seq // tm),
        grid=(n // tm,),
        in_specs=[row(D_MODEL), row(2 * CONV_WIDTH),
                  pl.BlockSpec((halo, 2 * CONV_WIDTH), lambda i: (jnp.maximum(i * hb - 1, 0), 0)),
                  pl.BlockSpec((halo, 2 * CONV_WIDTH), lambda i: (jnp.minimum((i + 1) * hb, nhalo - 1), 0)),
                  row(DIFF_WIDTH), row(HG_WIDTH), row(HG_WIDTH),
                  pl.BlockSpec((tm, HG_WIDTH), lambda i: (i, 4)),
                  full(3, CONV_WIDTH), full(1, HG_WIDTH), full(D_MODEL, D_MODEL), full(1, D_MODEL),
                  full(D_MODEL, 2 * N_EXPERTS)],
        out_specs=[row(D_MODEL), row(D_MODEL), row(N_EXPERTS)],
        out_shape=[jax.ShapeDtypeStruct((n, D_MODEL), F32),
                   jax.ShapeDtypeStruct((n, D_MODEL), BF16),
                   jax.ShapeDtypeStruct((n, N_EXPERTS), F32)],
        compiler_params=_params("parallel"),
        name="outproj",
    )(x2, bz, bz, bz, ydiff, of2, ob2, hg2, conv_w.astype(F32), hn, wo_bf16,
      norm_ffn.reshape(1, D_MODEL).astype(F32), rw3)


def _select_kernel(aff_ref, rankm_ref, rankx_ref, pre_ref, tot_ref, *, cap):
    nblk = aff_ref.shape[0]
    ne = N_EXPERTS
    bits = pltpu.bitcast(aff_ref[...], I32)

    def count(mask):
        part = jnp.sum(jnp.where(mask, 1.0, 0.0), axis=0)
        return jnp.sum(part, axis=1, keepdims=True)

    def search(i, tau):
        cand = tau | jnp.left_shift(jnp.int32(1), 30 - i)
        return jnp.where(count(bits >= cand[None]) >= cap, cand, tau)

    tau = lax.fori_loop(0, 31, search, jnp.zeros((ne, 1), I32))[None]

    r = lax.broadcasted_iota(I32, (LANES, LANES), 0)
    c = lax.broadcasted_iota(I32, (LANES, LANES), 1)
    incl = jnp.where(r <= c, 1.0, 0.0).astype(BF16)
    ones = jnp.ones((LANES, LANES), BF16)

    def prefix(mask):
        m = jnp.where(mask, 1.0, 0.0)
        m2 = m.reshape(nblk * ne, LANES).astype(BF16)
        pre_ref[...] = _dot(m2, incl).reshape(nblk, ne, LANES) - m
        tot_ref[...] = _dot(m2, ones).reshape(nblk, ne, LANES)

        def carry(b, run):
            pre_ref[b] = pre_ref[b] + run
            return run + tot_ref[b]

        lax.fori_loop(0, nblk, carry, jnp.zeros((ne, LANES), F32))
        return pre_ref[...]

    above = bits > tau
    tie = bits == tau
    need = cap - count(above)
    sel = above | (tie & (prefix(tie) < need[None]))
    rank = prefix(sel).astype(I32)
    rankx_ref[...] = rank
    rankm_ref[...] = jnp.where(sel, rank, -1)


def _select(aff3, cap):
    nblk = aff3.shape[0]
    shp = (nblk, N_EXPERTS, LANES)
    spec = pl.BlockSpec(shp, lambda i: (0, 0, 0))
    return pl.pallas_call(
        functools.partial(_select_kernel, cap=cap),
        grid=(1,),
        in_specs=[spec],
        out_specs=[spec, spec],
        out_shape=[jax.ShapeDtypeStruct(shp, I32), jax.ShapeDtypeStruct(shp, I32)],
        scratch_shapes=[pltpu.VMEM(shp, F32), pltpu.VMEM(shp, F32)],
        compiler_params=_params("arbitrary"),
        name="select",
    )(aff3)


def _expert_kernel(base_ref, rank_ref, h_ref, wg_ref, wu_ref, wd_ref, ye_ref, xe_ref,
                   *, nt, tt, sj, cap, win):
    e = pl.program_id(0)
    s = pl.program_id(1)

    @pl.when(s == 0)
    def _():
        xe_ref[...] = jnp.zeros_like(xe_ref)

    @pl.when(s < nt)
    def _():
        base = base_ref[e * (nt + 1) + s]
        count = base_ref[e * (nt + 1) + s + 1] - base
        wb = (base // 8) * 8
        nwin = jnp.where(count > 0, (base - wb + count + win - 1) // win, 0)
        rk = rank_ref[0, 0]
        rel = jnp.concatenate([rk[a:a + 1, :] for a in range(tt // LANES)], axis=1) - wb
        slot = lax.broadcasted_iota(I32, (win, tt), 0)

        def fill(w, carry):
            onehot = jnp.where(slot == rel - w * win, 1.0, 0.0).astype(BF16)
            r0 = pl.multiple_of(wb + w * win, 8)
            xe_ref[pl.ds(r0, win), :] += _dot(onehot, h_ref[...])
            return carry

        lax.fori_loop(0, nwin, fill, 0)

    @pl.when(s >= nt)
    def _():
        j0 = pl.multiple_of((s - nt) * sj, sj)
        xj = xe_ref[pl.ds(j0, sj), :].astype(BF16)
        a = _dot(xj, wg_ref[0])
        u = _dot(xj, wu_ref[0])
        hid = (a * _sigmoid(a) * u).astype(BF16)
        ye_ref[0] = _dot(hid, wd_ref[0]).astype(BF16)


def _experts(base_flat, rank4, h2, wg, wu, wd, layer, cap, tt, sj):
    n = h2.shape[0]
    nt = n // tt
    nj = cap // sj
    win = GATHER_ROWS
    wspec = pl.BlockSpec((pl.Squeezed(), 1, D_MODEL, D_MODEL), lambda e, s, b: (layer, e, 0, 0))
    grid_spec = pltpu.PrefetchScalarGridSpec(
        num_scalar_prefetch=1,
        grid=(N_EXPERTS, nt + nj),
        in_specs=[pl.BlockSpec((1, 1, tt // LANES, LANES),
                               lambda e, s, b: (e, jnp.minimum(s, nt - 1), 0, 0)),
                  pl.BlockSpec((tt, D_MODEL), lambda e, s, b: (jnp.minimum(s, nt - 1), 0)),
                  wspec, wspec, wspec],
        out_specs=pl.BlockSpec((1, sj, D_MODEL), lambda e, s, b: (e, jnp.maximum(s - nt, 0), 0)),
        scratch_shapes=[pltpu.VMEM((cap + win + 8, D_MODEL), F32)],
    )
    return pl.pallas_call(
        functools.partial(_expert_kernel, nt=nt, tt=tt, sj=sj, cap=cap, win=win),
        grid_spec=grid_spec,
        out_shape=jax.ShapeDtypeStruct((N_EXPERTS, cap, D_MODEL), BF16),
        compiler_params=_params("arbitrary", "arbitrary"),
        name="experts",
    )(base_flat, rank4, h2, wg, wu, wd)


def _main_start(base, cap, w1):
    return pl.multiple_of(jnp.minimum((base // 16) * 16, cap - w1), 16)


def _over_start(base, nxt, cap, w1, w2):
    wb = _main_start(base, cap, w1)
    need = nxt - wb > w1
    return need, pl.multiple_of(jnp.where(need, jnp.minimum(wb + w1, cap - w2), 0), 16)


def _combine_kernel(base_ref, x_ref, rank_ref, gate_ref, nrm_ref, *rest, nt, cap, w1, w2, final):
    ne = N_EXPERTS
    main_refs = rest[:ne]
    over_refs = rest[ne:2 * ne]
    out_ref = rest[2 * ne]
    t = pl.program_id(0)
    tt = x_ref.shape[0]
    rank = rank_ref[...]
    gate = gate_ref[...]
    bases = [base_ref[e * (nt + 1) + t] for e in range(ne)]
    nexts = [base_ref[e * (nt + 1) + t + 1] for e in range(ne)]

    col = lax.broadcasted_iota(I32, (1, ne), 1)
    wb_row = jnp.zeros((1, ne), I32)
    for e in range(ne):
        wb_row = jnp.where(col == e, _main_start(bases[e], cap, w1), wb_row)
    rel = rank - wb_row
    adj = jnp.where((rank >= 0) & (rel < w1), rel, -1)
    g_hi = gate.astype(BF16).astype(F32)
    g_lo = gate - g_hi
    per_vreg = LANES // w1
    lane = lax.broadcasted_iota(I32, (tt, LANES), 1)
    zero = jnp.zeros((tt, LANES), F32)
    p_hi, p_lo = [], []
    for i in range(ne // per_vreg):
        target = jnp.full((tt, LANES), -1, I32)
        hi = zero
        lo = zero
        for a in range(per_vreg):
            e = i * per_vreg + a
            inside = (lane >= a * w1) & (lane < (a + 1) * w1)
            target = jnp.where(inside, adj[:, e:e + 1] + a * w1, target)
            hi = jnp.where(inside, g_hi[:, e:e + 1], hi)
            lo = jnp.where(inside, g_lo[:, e:e + 1], lo)
        match = lane == target
        p_hi.append(jnp.where(match, hi, zero).astype(BF16))
        p_lo.append(jnp.where(match, lo, zero).astype(BF16))
    y_all = jnp.concatenate([r[...] for r in main_refs], axis=0)
    out_ref[...] = (x_ref[...] + _dot(jnp.concatenate(p_hi, axis=1), y_all)
                    + _dot(jnp.concatenate(p_lo, axis=1), y_all))

    overs = [_over_start(bases[e], nexts[e], cap, w1, w2) for e in range(ne)]
    any_need = overs[0][0]
    for e in range(1, ne):
        any_need = any_need | overs[e][0]

    @pl.when(any_need)
    def _():
        lane2 = lax.broadcasted_iota(I32, (tt, w2), 1)
        acc = jnp.zeros((tt, D_MODEL), F32)
        for e in range(ne):
            need, wo = overs[e]
            first = _main_start(bases[e], cap, w1) + w1
            r = rank[:, e:e + 1]
            match = (lane2 == r - wo) & (r >= first) & need
            onehot = jnp.where(match, 1.0, 0.0).astype(BF16)
            acc = acc + gate[:, e:e + 1] * _dot(onehot, over_refs[e][...])
        out_ref[...] += acc

    if final:
        o = out_ref[...]
        ms = jnp.mean(o * o, axis=-1, keepdims=True)
        out_ref[...] = o * lax.rsqrt(ms + NORM_EPS) * nrm_ref[...]


def _combine(base_flat, x1, rank_tok, gate, ye, nrm, cap, tt, final):
    n = x1.shape[0]
    nt = n // tt
    w1 = COMBINE_MAIN_ROWS
    w2 = tt + 16 - w1
    assert cap >= max(w1, w2) and cap % 16 == 0 and LANES % w1 == 0
    row = lambda w: pl.BlockSpec((tt, w), lambda t, b: (t, 0))

    def main_spec(e):
        return pl.BlockSpec(
            (pl.Squeezed(), pl.Element(w1), pl.Element(D_MODEL)),
            lambda t, b: (e, _main_start(b[e * (nt + 1) + t], cap, w1), 0))

    def over_spec(e):
        return pl.BlockSpec(
            (pl.Squeezed(), pl.Element(w2), pl.Element(D_MODEL)),
            lambda t, b: (e, _over_start(b[e * (nt + 1) + t], b[e * (nt + 1) + t + 1], cap, w1, w2)[1], 0))

    grid_spec = pltpu.PrefetchScalarGridSpec(
        num_scalar_prefetch=1,
        grid=(nt,),
        in_specs=[row(D_MODEL), row(N_EXPERTS), row(N_EXPERTS),
                  pl.BlockSpec((1, D_MODEL), lambda t, b: (0, 0))]
                 + [main_spec(e) for e in range(N_EXPERTS)]
                 + [over_spec(e) for e in range(N_EXPERTS)],
        out_specs=row(D_MODEL),
    )
    return pl.pallas_call(
        functools.partial(_combine_kernel, nt=nt, cap=cap, w1=w1, w2=w2, final=final),
        grid_spec=grid_spec,
        out_shape=jax.ShapeDtypeStruct((n, D_MODEL), F32),
        compiler_params=_params("arbitrary"),
        name="combine",
    )(base_flat, x1, rank_tok, gate, nrm.reshape(1, D_MODEL).astype(F32), *([ye] * (2 * N_EXPERTS)))


def _tile(n, want):
    t = min(n, want)
    assert n % t == 0, (n, want)
    return t


def _routed_ffn(x1, h2, aff, wg, wu, wd, layer, nrm, final):
    n = x1.shape[0]
    cap = (EC_CAPACITY_FACTOR * n) // N_EXPERTS
    nblk = n // LANES
    tt_e = _tile(n, 1024)
    sj = _tile(cap, 512)
    tt_c = _tile(n, 256)

    aff3 = aff.reshape(nblk, LANES, N_EXPERTS).transpose(0, 2, 1)
    rankm3, rankx3 = _select(aff3, cap)
    rank_tok = rankm3.transpose(0, 2, 1).reshape(n, N_EXPERTS)
    rank4 = rankm3.transpose(1, 0, 2).reshape(N_EXPERTS, n // tt_e, tt_e // LANES, LANES)
    base128 = rankx3[:, :, 0].T
    last = jnp.full((N_EXPERTS, 1), cap, I32)
    base_e = jnp.concatenate([base128[:, ::tt_e // LANES], last], axis=1).reshape(-1)
    base_c = jnp.concatenate([base128[:, ::tt_c // LANES], last], axis=1).reshape(-1)

    ye = _experts(base_e, rank4, h2, wg, wu, wd, layer, cap, tt_e, sj)
    return _combine(base_c, x1, rank_tok, aff, ye, nrm, cap, tt_c, final)


def _trunk(x, p):
    bsz, seq, _ = x.shape
    n = bsz * seq
    depth = p["w_in"].shape[0]
    tm = _tile(seq, 512)
    tq = _tile(seq, 512)
    tk = _tile(seq, 2048)
    t_hg = _tile(seq, 512)
    tables = _rope_tables(seq)

    x2 = x.reshape(n, D_MODEL)
    for l in range(depth):
        bz, q, k, v, hg = _inproj(x2, p["norm_mix"][l], p["w_in"][l], tables, seq, tm)
        to3 = lambda a: a.reshape(bsz, seq, a.shape[-1])
        ydiff = _attention(to3(q), to3(k), to3(v), p["diff_lambda"][l], p["diff_subln"][l], l, tq, tk)
        o_f, o_b = _hgrn(to3(hg), p["hgrn_lb"], l, t_hg)
        x1, h2, aff = _outproj(x2, bz, ydiff.reshape(n, DIFF_WIDTH), o_f.reshape(n, HG_WIDTH),
                               o_b.reshape(n, HG_WIDTH), hg, p["conv_w"][l], p["hgrn_norm"][l],
                               p["w_out"][l], p["norm_ffn"][l], p["router_w"][l], seq, tm)
        x2 = _routed_ffn(x1, h2, aff, p["w_gate"], p["w_up"], p["w_down"], l,
                         p["norm_final"], l == depth - 1)
    return x2.reshape(bsz, seq, D_MODEL)


def _split2_param(w):
    w = w.astype(F32)
    hi = w.astype(BF16)
    lo = (w - hi.astype(F32)).astype(BF16)
    return jnp.concatenate([hi, lo], axis=-1)


def kernel(x_prompt, x_sample, norm_mix, w_in, conv_w, diff_lambda, diff_subln, hgrn_lb,
           hgrn_norm, w_out, norm_ffn, router_w, w_gate, w_up, w_down, norm_final):
    p = dict(
        norm_mix=norm_mix.astype(F32), w_in=w_in.astype(BF16), conv_w=conv_w,
        diff_lambda=diff_lambda, diff_subln=diff_subln, hgrn_lb=hgrn_lb, hgrn_norm=hgrn_norm,
        w_out=w_out.astype(BF16), norm_ffn=norm_ffn,
        router_w=_split2_param(router_w),
        w_gate=w_gate.astype(BF16), w_up=w_up.astype(BF16), w_down=w_down.astype(BF16),
        norm_final=norm_final,
    )
    return _trunk(x_prompt, p), _trunk(x_sample, p)
```

```python
import functools
import math

import jax
import jax.numpy as jnp
from jax import lax
from jax.experimental import pallas as pl
from jax.experimental.pallas import tpu as pltpu

F32 = jnp.float32
BF16 = jnp.bfloat16
I32 = jnp.int32

D_MODEL = 1024
CONV_WIDTH = 256
DIFF_WIDTH = 512
DIFF_HEADS = 4
DIFF_HEAD_DIM = 64
HG_WIDTH = 256
HG_HEADS = 4
HG_HEAD_DIM = 64
HG_CHUNK = 64
D_PROJ = 3584
ROPE_THETA = 10000.0
N_EXPERTS = 16
EC_CAPACITY_FACTOR = 2
NORM_EPS = 1e-6

VMEM_LIMIT_BYTES = 56 * 1024 * 1024
LANES = 128
GATHER_ROWS = 256
COMBINE_MAIN_ROWS = 64

NT_DIMS = (((1,), (1,)), ((), ()))
Q_SCALE = DIFF_HEAD_DIM ** -0.5 * math.log2(math.e)


def _params(*sem):
    return pltpu.CompilerParams(dimension_semantics=sem, vmem_limit_bytes=VMEM_LIMIT_BYTES)


def _split3(x):
    hi = x.astype(BF16)
    r1 = x - hi.astype(F32)
    mid = r1.astype(BF16)
    lo = (r1 - mid.astype(F32)).astype(BF16)
    return hi, mid, lo


def _dot(a, b):
    return jnp.dot(a, b, preferred_element_type=F32)


def _dot_nt(a, b):
    return lax.dot_general(a, b, NT_DIMS, preferred_element_type=F32)


def _dot_exact_lhs(a_bf16, x_f32):
    hi, mid, lo = _split3(x_f32)
    return _dot(a_bf16, hi) + _dot(a_bf16, mid) + _dot(a_bf16, lo)


def _sigmoid(x):
    return 1.0 / (1.0 + jnp.exp(-x))


def _inproj_kernel(x_ref, g_ref, w_ref, cos_ref, sina_ref, sinb_ref,
                   bz_ref, q_ref, k_ref, v_ref, hg_ref):
    x = x_ref[...]
    ms = jnp.mean(x * x, axis=-1, keepdims=True)
    h = (x * lax.rsqrt(ms + NORM_EPS) * g_ref[...]).astype(BF16)

    def proj(a, b):
        return _dot(h, w_ref[:, a:b])

    cw = CONV_WIDTH
    bz_ref[:, 0:cw] = proj(0, cw).astype(BF16)
    bz_ref[:, cw:2 * cw] = (proj(cw, 2 * cw) * proj(2 * cw, 3 * cw)).astype(BF16)

    reps = DIFF_WIDTH // LANES
    cos = jnp.concatenate([cos_ref[...]] * reps, axis=1)
    sina = jnp.concatenate([sina_ref[...]] * reps, axis=1)
    sinb = jnp.concatenate([sinb_ref[...]] * reps, axis=1)
    half = DIFF_HEAD_DIM // 2

    def rope(t):
        return (t * cos + pltpu.roll(t, DIFF_WIDTH - half, axis=1) * sina
                + pltpu.roll(t, half, axis=1) * sinb)

    o = 3 * cw
    q_ref[...] = (rope(proj(o, o + DIFF_WIDTH)) * Q_SCALE).astype(BF16)
    o += DIFF_WIDTH
    k_ref[...] = rope(proj(o, o + DIFF_WIDTH)).astype(BF16)
    o += DIFF_WIDTH
    v = proj(o, o + DIFF_WIDTH).astype(BF16)
    hd = 2 * DIFF_HEAD_DIM
    ones = jnp.ones((v.shape[0], hd), BF16)
    for head in range(DIFF_HEADS):
        v_ref[:, 2 * head * hd:(2 * head + 1) * hd] = v[:, head * hd:(head + 1) * hd]
        v_ref[:, (2 * head + 1) * hd:(2 * head + 2) * hd] = ones
    o += DIFF_WIDTH
    hg_ref[...] = proj(o, D_PROJ)


def _rope_tables(seq):
    d = DIFF_HEAD_DIM
    inv = 1.0 / (ROPE_THETA ** (jnp.arange(0, d, 2, dtype=F32) / d))
    ang = jnp.arange(seq, dtype=F32)[:, None] * inv[None, :]
    ang = jnp.concatenate([ang, ang, ang, ang], axis=-1)
    cos, sin = jnp.cos(ang), jnp.sin(ang)
    first = (jnp.arange(LANES) % d) < (d // 2)
    sina = jnp.where(first[None, :], -sin, 0.0)
    sinb = jnp.where(first[None, :], 0.0, sin)
    return cos, sina, sinb


def _inproj(x2, g, w_bf16, tables, seq, tm):
    n = x2.shape[0]
    nblk_s = seq // tm
    cos, sina, sinb = tables
    tab_spec = pl.BlockSpec((tm, LANES), lambda i: (i % nblk_s, 0))
    row = lambda w: pl.BlockSpec((tm, w), lambda i: (i, 0))
    hgw = 5 * HG_WIDTH
    return pl.pallas_call(
        _inproj_kernel,
        grid=(n // tm,),
        in_specs=[row(D_MODEL),
                  pl.BlockSpec((1, D_MODEL), lambda i: (0, 0)),
                  pl.BlockSpec((D_MODEL, D_PROJ), lambda i: (0, 0)),
                  tab_spec, tab_spec, tab_spec],
        out_specs=[row(2 * CONV_WIDTH), row(DIFF_WIDTH), row(DIFF_WIDTH), row(2 * DIFF_WIDTH), row(hgw)],
        out_shape=[jax.ShapeDtypeStruct((n, 2 * CONV_WIDTH), BF16),
                   jax.ShapeDtypeStruct((n, DIFF_WIDTH), BF16),
                   jax.ShapeDtypeStruct((n, DIFF_WIDTH), BF16),
                   jax.ShapeDtypeStruct((n, 2 * DIFF_WIDTH), BF16),
                   jax.ShapeDtypeStruct((n, hgw), F32)],
        compiler_params=_params("parallel"),
        name="inproj",
    )(x2, g.reshape(1, D_MODEL), w_bf16, cos, sina, sinb)


def _attn_kernel(q_ref, k_ref, v_ref, lam_ref, g_ref, o_ref, *, tk, lam_init):
    q = q_ref[0]
    tq = q.shape[0]
    seq = k_ref.shape[1]
    nblk = seq // tk
    d = DIFF_HEAD_DIM
    hd = 2 * d
    lane = lax.broadcasted_iota(I32, q.shape, 1)
    zero = jnp.zeros_like(q)
    qs = (jnp.where(lane < d, q, zero), jnp.where(lane >= d, q, zero))

    def scores(j):
        ks = k_ref[0, j * tk:(j + 1) * tk, :]
        out = []
        for h in range(2):
            s = _dot_nt(qs[h], ks)
            out.append((s, jnp.max(s, axis=1, keepdims=True)))
        return out

    def consume(j, sc, state):
        vs = v_ref[0, j * tk:(j + 1) * tk, :]
        out = []
        for h in range(2):
            m, acc = state[h]
            s, bm = sc[h]
            mn = jnp.maximum(m, bm)
            p = jnp.exp2(s - mn).astype(BF16)
            acc = jnp.exp2(m - mn) * acc + _dot(p, vs)
            out.append((mn, acc))
        return out

    init = (jnp.full((tq, 1), -1e30, F32), jnp.zeros((tq, 2 * hd), F32))
    state = [init, init]
    sc = scores(0)
    for j in range(nblk):
        sc_next = scores(j + 1) if j + 1 < nblk else None
        state = consume(j, sc, state)
        sc = sc_next
    (_, a1), (_, a2) = state

    lp = lam_ref[...]
    lam = (jnp.exp(jnp.sum(lp[0:1] * lp[1:2], axis=1, keepdims=True))
           - jnp.exp(jnp.sum(lp[2:3] * lp[3:4], axis=1, keepdims=True)) + lam_init)
    o = a1[:, :hd] / a1[:, hd:] - lam * (a2[:, :hd] / a2[:, hd:])
    ms = jnp.mean(o * o, axis=-1, keepdims=True)
    y = o * lax.rsqrt(ms + NORM_EPS) * g_ref[...] * (1.0 - lam_init)
    o_ref[0] = y.astype(BF16)


def _attention(q3, k3, v3, lam_params, subln_g, layer, tq, tk):
    bsz, seq, _ = q3.shape
    lam_init = 0.8 - 0.6 * math.exp(-0.3 * layer)
    hd = 2 * DIFF_HEAD_DIM
    return pl.pallas_call(
        functools.partial(_attn_kernel, tk=tk, lam_init=lam_init),
        grid=(bsz, DIFF_HEADS, seq // tq),
        in_specs=[pl.BlockSpec((1, tq, hd), lambda b, h, i: (b, i, h)),
                  pl.BlockSpec((1, seq, hd), lambda b, h, i: (b, 0, h)),
                  pl.BlockSpec((1, seq, 2 * hd), lambda b, h, i: (b, 0, h)),
                  pl.BlockSpec((4, DIFF_HEAD_DIM), lambda b, h, i: (0, 0)),
                  pl.BlockSpec((1, hd), lambda b, h, i: (0, 0))],
        out_specs=pl.BlockSpec((1, tq, hd), lambda b, h, i: (b, i, h)),
        out_shape=jax.ShapeDtypeStruct((bsz, seq, DIFF_WIDTH), BF16),
        compiler_params=_params("parallel", "parallel", "parallel"),
        name="attn",
    )(q3, k3, v3, lam_params.astype(F32), subln_g.reshape(1, hd).astype(F32))


def _hgrn_kernel(qf_ref, ff_ref, if_ref, qb_ref, fb_ref, ib_ref, lbp_ref,
                 of_ref, ob_ref, stf_ref, stb_ref, *, layer):
    c_len = HG_CHUNK
    w = HG_WIDTH
    hd = HG_HEAD_DIM
    t_len = qf_ref.shape[1]
    nchunk = t_len // c_len

    @pl.when(pl.program_id(1) == 0)
    def _():
        stf_ref[...] = jnp.zeros_like(stf_ref)
        stb_ref[...] = jnp.zeros_like(stb_ref)

    lbp = lbp_ref[...]
    e = jnp.exp(lbp - jnp.max(lbp, axis=0, keepdims=True))
    sm = e / jnp.sum(e, axis=0, keepdims=True)
    lb = jnp.zeros((1, 2 * w), F32)
    for l in range(1, layer + 1):
        lb = lb + sm[l:l + 1]
    lb_f, lb_b = lb[:, :w], lb[:, w:]

    r = lax.broadcasted_iota(I32, (c_len, c_len), 0)
    c = lax.broadcasted_iota(I32, (c_len, c_len), 1)
    low = r >= c
    upp = r <= c
    low_m = jnp.where(low, 1.0, 0.0).astype(BF16)
    upp_m = jnp.where(upp, 1.0, 0.0).astype(BF16)
    lane = lax.broadcasted_iota(I32, (1, w), 1)
    heads = [(lane >= h * hd) & (lane < (h + 1) * hd) for h in range(HG_HEADS)]
    br = lax.broadcasted_iota(I32, (w, w), 0) // hd
    bc = lax.broadcasted_iota(I32, (w, w), 1) // hd
    same_head = br == bc
    eye = jnp.where(lax.broadcasted_iota(I32, (w, w), 0) == lax.broadcasted_iota(I32, (w, w), 1),
                    1.0, 0.0).astype(BF16)
    mid = c_len // 2

    def bmm(a, b):
        return jnp.einsum('nik,nkj->nij', a, b, preferred_element_type=F32)

    def bmm_nt(a, b):
        return jnp.einsum('nik,njk->nij', a, b, preferred_element_type=F32)

    def scan_block(xq, xf, xi, lbd, tri_m, tri, last, order, st_ref):
        c3 = lambda a: a.reshape(nchunk, c_len, w)
        qh = c3(xq * _sigmoid(xq) * (hd ** -0.5))
        f = lbd + (1.0 - lbd) * _sigmoid(xf)
        kk = c3(1.0 - f)
        tri_b = jnp.broadcast_to(tri_m[None], (nchunk, c_len, c_len))
        g_hi, g_mid, g_lo = _split3(c3(jnp.log(f)))
        G = bmm(tri_b, g_hi) + bmm(tri_b, g_mid) + bmm(tri_b, g_lo)
        g_ref_row = G[:, mid:mid + 1, :]
        g_last = G[:, last:last + 1, :]
        qt = (qh * jnp.exp(G - g_ref_row)).astype(BF16)
        kt = (kk * jnp.exp(g_ref_row - G)).astype(BF16)
        q_in = (qh * jnp.exp(G)).astype(BF16)
        k_out = (kk * jnp.exp(g_last - G)).astype(BF16)
        vb = c3(xi).astype(BF16)
        zq = jnp.zeros_like(qt)
        o = jnp.zeros((nchunk, c_len, w), F32)
        for h in range(HG_HEADS):
            a = bmm_nt(jnp.where(heads[h][None], qt, zq), kt)
            a = jnp.where(tri[None], a, 0.0).astype(BF16)
            o = o + jnp.where(heads[h][None], bmm(a, vb), 0.0)
        eye_b = jnp.broadcast_to(eye[None], (nchunk, w, w))
        v_t = bmm_nt(eye_b, vb).astype(BF16)
        upd = jnp.where(same_head[None], bmm(v_t, k_out), 0.0)
        decay = jnp.exp(g_last)
        st = st_ref[...]
        inter = [None] * nchunk
        for ci in order:
            inter[ci] = _dot_nt(q_in[ci], st.astype(BF16))
            st = st * decay[ci] + upd[ci]
        st_ref[...] = st
        return o.reshape(nchunk * c_len, w) + jnp.concatenate(inter, axis=0)

    fwd_order = list(range(nchunk))
    of_ref[0] = scan_block(qf_ref[0], ff_ref[0], if_ref[0], lb_f, low_m, low, c_len - 1,
                           fwd_order, stf_ref)
    ob_ref[0] = scan_block(qb_ref[0], fb_ref[0], ib_ref[0], lb_b, upp_m, upp, 0,
                           fwd_order[::-1], stb_ref)


def _hgrn(hg3, hgrn_lb, layer, t_len):
    bsz, seq, _ = hg3.shape
    nt = seq // t_len
    w = HG_WIDTH
    fwd = lambda col: pl.BlockSpec((1, t_len, w), lambda b, j: (b, j, col))
    bwd = lambda col: pl.BlockSpec((1, t_len, w), lambda b, j: (b, nt - 1 - j, col))
    depth = hgrn_lb.shape[0]
    return pl.pallas_call(
        functools.partial(_hgrn_kernel, layer=layer),
        grid=(bsz, nt),
        in_specs=[fwd(0), fwd(1), fwd(3), bwd(0), bwd(2), bwd(3),
                  pl.BlockSpec((depth, 2 * w), lambda b, j: (0, 0))],
        out_specs=[pl.BlockSpec((1, t_len, w), lambda b, j: (b, j, 0)),
                   pl.BlockSpec((1, t_len, w), lambda b, j: (b, nt - 1 - j, 0))],
        out_shape=[jax.ShapeDtypeStruct((bsz, seq, w), F32),
                   jax.ShapeDtypeStruct((bsz, seq, w), F32)],
        scratch_shapes=[pltpu.VMEM((w, w), F32), pltpu.VMEM((w, w), F32)],
        compiler_params=_params("parallel", "arbitrary"),
        name="hgrn",
    )(hg3, hg3, hg3, hg3, hg3, hg3, hgrn_lb.astype(F32))


def _outproj_kernel(x_ref, bz_ref, zp_ref, zn_ref, yd_ref, of_ref, ob_ref, hgate_ref,
                    cw_ref, hn_ref, wo_ref, nf_ref, rw_ref,
                    x1_ref, h2_ref, aff_ref, affp_ref, *, blocks_per_seq):
    tm = x_ref.shape[0]
    cwid = CONV_WIDTH
    i = pl.program_id(0)
    pos = i % blocks_per_seq
    halo = zp_ref.shape[0]

    b = bz_ref[:, 0:cwid].astype(F32)
    z = bz_ref[:, cwid:2 * cwid].astype(F32)
    zprev = zp_ref[:, cwid:2 * cwid].astype(F32)[halo - 1:halo]
    znext = zn_ref[:, cwid:2 * cwid].astype(F32)[0:1]
    zprev = jnp.where(pos == 0, 0.0, zprev)
    znext = jnp.where(pos == blocks_per_seq - 1, 0.0, znext)
    row = lax.broadcasted_iota(I32, (tm, cwid), 0)
    z_up = jnp.where(row == 0, zprev, pltpu.roll(z, 1, axis=0))
    z_dn = jnp.where(row == tm - 1, znext, pltpu.roll(z, tm - 1, axis=0))
    cw = cw_ref[...]
    y_conv = b * (cw[0:1] * z_up + cw[1:2] * z + cw[2:3] * z_dn)

    o = of_ref[...] + ob_ref[...]
    w = HG_WIDTH
    br = lax.broadcasted_iota(I32, (w, w), 0) // HG_HEAD_DIM
    bc = lax.broadcasted_iota(I32, (w, w), 1) // HG_HEAD_DIM
    pool = jnp.where(br == bc, 1.0, 0.0).astype(BF16)
    sq = o * o
    sq_hi = sq.astype(BF16)
    sq_lo = (sq - sq_hi.astype(F32)).astype(BF16)
    ms = (_dot(sq_hi, pool) + _dot(sq_lo, pool)) * (1.0 / HG_HEAD_DIM)
    gt = hgate_ref[...]
    y_h = o * lax.rsqrt(ms + NORM_EPS) * hn_ref[...] * (gt * _sigmoid(gt))

    mixed = (_dot(y_conv.astype(BF16), wo_ref[0:cwid, :])
             + _dot(yd_ref[...], wo_ref[cwid:cwid + DIFF_WIDTH, :])
             + _dot(y_h.astype(BF16), wo_ref[cwid + DIFF_WIDTH:, :]))
    x1 = x_ref[...] + mixed
    x1_ref[...] = x1

    ms2 = jnp.mean(x1 * x1, axis=-1, keepdims=True)
    h2 = x1 * lax.rsqrt(ms2 + NORM_EPS) * nf_ref[...]
    h_hi = h2.astype(BF16)
    h2_ref[...] = h_hi

    h_lo = (h2 - h_hi.astype(F32)).astype(BF16)
    rw = rw_ref[...]
    d_hi = _dot(h_hi, rw)
    d_lo = _dot(h_lo, rw)
    ne = N_EXPERTS
    logits = d_hi[:, :ne] + (d_hi[:, ne:] + d_lo[:, :ne]) + d_lo[:, ne:]
    ex = jnp.exp(logits - jnp.max(logits, axis=-1, keepdims=True))
    aff = ex / jnp.sum(ex, axis=-1, keepdims=True)
    aff_ref[...] = aff
    a_hi = aff.astype(BF16)
    a_lo = (aff - a_hi.astype(F32)).astype(BF16)
    pad = jnp.zeros((aff.shape[0], LANES - 2 * ne), BF16)
    affp_ref[...] = jnp.concatenate([a_hi, a_lo, pad], axis=1)


def _outproj(x2, bz, ydiff, of2, ob2, hg2, conv_w, hgrn_norm, wo_bf16, norm_ffn, rw3, seq, tm):
    n = x2.shape[0]
    halo = 16
    hb = tm // halo
    nhalo = n // halo
    row = lambda w: pl.BlockSpec((tm, w), lambda i: (i, 0))
    full = lambda a, b: pl.BlockSpec((a, b), lambda i: (0, 0))
    hn = jnp.tile(hgrn_norm.astype(F32), HG_HEADS).reshape(1, HG_WIDTH)
    return pl.pallas_call(
        functools.partial(_outproj_kernel, blocks_per_seq=seq // tm),
        grid=(n // tm,),
        in_specs=[row(D_MODEL), row(2 * CONV_WIDTH),
                  pl.BlockSpec((halo, 2 * CONV_WIDTH), lambda i: (jnp.maximum(i * hb - 1, 0), 0)),
                  pl.BlockSpec((halo, 2 * CONV_WIDTH), lambda i: (jnp.minimum((i + 1) * hb, nhalo - 1), 0)),
                  row(DIFF_WIDTH), row(HG_WIDTH), row(HG_WIDTH),
                  pl.BlockSpec((tm, HG_WIDTH), lambda i: (i, 4)),
                  full(3, CONV_WIDTH), full(1, HG_WIDTH), full(D_MODEL, D_MODEL), full(1, D_MODEL),
                  full(D_MODEL, 2 * N_EXPERTS)],
        out_specs=[row(D_MODEL), row(D_MODEL), row(N_EXPERTS), row(LANES)],
        out_shape=[jax.ShapeDtypeStruct((n, D_MODEL), F32),
                   jax.ShapeDtypeStruct((n, D_MODEL), BF16),
                   jax.ShapeDtypeStruct((n, N_EXPERTS), F32),
                   jax.ShapeDtypeStruct((n, LANES), BF16)],
        compiler_params=_params("parallel"),
        name="outproj",
    )(x2, bz, bz, bz, ydiff, of2, ob2, hg2, conv_w.astype(F32), hn, wo_bf16,
      norm_ffn.reshape(1, D_MODEL).astype(F32), rw3)


def _select_kernel(aff_ref, rankm_ref, rankx_ref, pre_ref, tot_ref, *, cap):
    nblk = aff_ref.shape[0]
    ne = N_EXPERTS
    bits = pltpu.bitcast(aff_ref[...], I32)

    def count(mask):
        part = jnp.sum(jnp.where(mask, 1.0, 0.0), axis=0)
        return jnp.sum(part, axis=1, keepdims=True)

    def search(i, tau):
        cand = tau | jnp.left_shift(jnp.int32(1), 30 - i)
        return jnp.where(count(bits >= cand[None]) >= cap, cand, tau)

    tau = lax.fori_loop(0, 31, search, jnp.zeros((ne, 1), I32))[None]

    r = lax.broadcasted_iota(I32, (LANES, LANES), 0)
    c = lax.broadcasted_iota(I32, (LANES, LANES), 1)
    incl = jnp.where(r <= c, 1.0, 0.0).astype(BF16)
    ones = jnp.ones((LANES, LANES), BF16)

    def prefix(mask):
        m = jnp.where(mask, 1.0, 0.0)
        m2 = m.reshape(nblk * ne, LANES).astype(BF16)
        pre_ref[...] = _dot(m2, incl).reshape(nblk, ne, LANES) - m
        tot_ref[...] = _dot(m2, ones).reshape(nblk, ne, LANES)

        def carry(b, run):
            pre_ref[b] = pre_ref[b] + run
            return run + tot_ref[b]

        lax.fori_loop(0, nblk, carry, jnp.zeros((ne, LANES), F32))
        return pre_ref[...]

    above = bits > tau
    tie = bits == tau
    need = cap - count(above)
    sel = above | (tie & (prefix(tie) < need[None]))
    rank = prefix(sel).astype(I32)
    rankx_ref[...] = rank
    rankm_ref[...] = jnp.where(sel, rank, -1)


def _select(aff3, cap):
    nblk = aff3.shape[0]
    shp = (nblk, N_EXPERTS, LANES)
    spec = pl.BlockSpec(shp, lambda i: (0, 0, 0))
    return pl.pallas_call(
        functools.partial(_select_kernel, cap=cap),
        grid=(1,),
        in_specs=[spec],
        out_specs=[spec, spec],
        out_shape=[jax.ShapeDtypeStruct(shp, I32), jax.ShapeDtypeStruct(shp, I32)],
        scratch_shapes=[pltpu.VMEM(shp, F32), pltpu.VMEM(shp, F32)],
        compiler_params=_params("arbitrary"),
        name="select",
    )(aff3)


def _expert_kernel(base_ref, rank_ref, h_ref, g_ref, wg_ref, wu_ref, wd_ref, ye_ref, xe_ref, xg_ref,
                   *, nt, tt, sj, cap, win):
    e = pl.program_id(0)
    s = pl.program_id(1)

    @pl.when(s == 0)
    def _():
        xe_ref[...] = jnp.zeros_like(xe_ref)
        xg_ref[...] = jnp.zeros_like(xg_ref)

    @pl.when(s < nt)
    def _():
        base = base_ref[e * (nt + 1) + s]
        count = base_ref[e * (nt + 1) + s + 1] - base
        wb = (base // 8) * 8
        nwin = jnp.where(count > 0, (base - wb + count + win - 1) // win, 0)
        rk = rank_ref[0, 0]
        rel = jnp.concatenate([rk[a:a + 1, :] for a in range(tt // LANES)], axis=1) - wb
        slot = lax.broadcasted_iota(I32, (win, tt), 0)

        def fill(w, carry):
            onehot = jnp.where(slot == rel - w * win, 1.0, 0.0).astype(BF16)
            r0 = pl.multiple_of(wb + w * win, 8)
            xe_ref[pl.ds(r0, win), :] += _dot(onehot, h_ref[...])
            xg_ref[pl.ds(r0, win), :] += _dot(onehot, g_ref[...])
            return carry

        lax.fori_loop(0, nwin, fill, 0)

    @pl.when(s >= nt)
    def _():
        j0 = pl.multiple_of((s - nt) * sj, sj)
        xj = xe_ref[pl.ds(j0, sj), :].astype(BF16)
        a = _dot(xj, wg_ref[0])
        u = _dot(xj, wu_ref[0])
        hid = (a * _sigmoid(a) * u).astype(BF16)
        lane = lax.broadcasted_iota(I32, (sj, LANES), 1)
        mine = (lane == e) | (lane == N_EXPERTS + e)
        gate = jnp.sum(jnp.where(mine, xg_ref[pl.ds(j0, sj), :], 0.0), axis=1, keepdims=True)
        ye_ref[0] = (_dot(hid, wd_ref[0]) * gate).astype(BF16)


def _experts(base_flat, rank4, h2, affp, wg, wu, wd, layer, cap, tt, sj):
    n = h2.shape[0]
    nt = n // tt
    nj = cap // sj
    win = GATHER_ROWS
    wspec = pl.BlockSpec((pl.Squeezed(), 1, D_MODEL, D_MODEL), lambda e, s, b: (layer, e, 0, 0))
    grid_spec = pltpu.PrefetchScalarGridSpec(
        num_scalar_prefetch=1,
        grid=(N_EXPERTS, nt + nj),
        in_specs=[pl.BlockSpec((1, 1, tt // LANES, LANES),
                               lambda e, s, b: (e, jnp.minimum(s, nt - 1), 0, 0)),
                  pl.BlockSpec((tt, D_MODEL), lambda e, s, b: (jnp.minimum(s, nt - 1), 0)),
                  pl.BlockSpec((tt, LANES), lambda e, s, b: (jnp.minimum(s, nt - 1), 0)),
                  wspec, wspec, wspec],
        out_specs=pl.BlockSpec((1, sj, D_MODEL), lambda e, s, b: (e, jnp.maximum(s - nt, 0), 0)),
        scratch_shapes=[pltpu.VMEM((cap + win + 8, D_MODEL), F32),
                        pltpu.VMEM((cap + win + 8, LANES), F32)],
    )
    return pl.pallas_call(
        functools.partial(_expert_kernel, nt=nt, tt=tt, sj=sj, cap=cap, win=win),
        grid_spec=grid_spec,
        out_shape=jax.ShapeDtypeStruct((N_EXPERTS, cap, D_MODEL), BF16),
        compiler_params=_params("arbitrary", "arbitrary"),
        name="experts",
    )(base_flat, rank4, h2, affp, wg, wu, wd)


def _main_start(base, cap, w1):
    return pl.multiple_of(jnp.minimum((base // 16) * 16, cap - w1), 16)


def _over_start(base, nxt, cap, w1, w2):
    wb = _main_start(base, cap, w1)
    need = nxt - wb > w1
    return need, pl.multiple_of(jnp.where(need, jnp.minimum(wb + w1, cap - w2), 0), 16)


def _combine_kernel(base_ref, x_ref, rank_ref, nrm_ref, *rest, nt, cap, w1, w2, final):
    ne = N_EXPERTS
    main_refs = rest[:ne]
    over_refs = rest[ne:2 * ne]
    out_ref = rest[2 * ne]
    t = pl.program_id(0)
    tt = x_ref.shape[0]
    rank = rank_ref[...]
    bases = [base_ref[e * (nt + 1) + t] for e in range(ne)]
    nexts = [base_ref[e * (nt + 1) + t + 1] for e in range(ne)]

    col = lax.broadcasted_iota(I32, (1, ne), 1)
    wb_row = jnp.zeros((1, ne), I32)
    for e in range(ne):
        wb_row = jnp.where(col == e, _main_start(bases[e], cap, w1), wb_row)
    rel = rank - wb_row
    adj = jnp.where((rank >= 0) & (rel < w1), rel, -1)
    per_vreg = LANES // w1
    lane = lax.broadcasted_iota(I32, (tt, LANES), 1)
    onehots = []
    for i in range(ne // per_vreg):
        target = jnp.full((tt, LANES), -1, I32)
        for a in range(per_vreg):
            e = i * per_vreg + a
            inside = (lane >= a * w1) & (lane < (a + 1) * w1)
            target = jnp.where(inside, adj[:, e:e + 1] + a * w1, target)
        onehots.append(jnp.where(lane == target, 1.0, 0.0).astype(BF16))
    y_all = jnp.concatenate([r[...] for r in main_refs], axis=0)
    out_ref[...] = x_ref[...] + _dot(jnp.concatenate(onehots, axis=1), y_all)

    overs = [_over_start(bases[e], nexts[e], cap, w1, w2) for e in range(ne)]
    any_need = overs[0][0]
    for e in range(1, ne):
        any_need = any_need | overs[e][0]

    @pl.when(any_need)
    def _():
        lane2 = lax.broadcasted_iota(I32, (tt, w2), 1)
        acc = jnp.zeros((tt, D_MODEL), F32)
        for e in range(ne):
            need, wo = overs[e]
            first = _main_start(bases[e], cap, w1) + w1
            r = rank[:, e:e + 1]
            match = (lane2 == r - wo) & (r >= first) & need
            onehot = jnp.where(match, 1.0, 0.0).astype(BF16)
            acc = acc + _dot(onehot, over_refs[e][...])
        out_ref[...] += acc

    if final:
        o = out_ref[...]
        ms = jnp.mean(o * o, axis=-1, keepdims=True)
        out_ref[...] = o * lax.rsqrt(ms + NORM_EPS) * nrm_ref[...]


def _combine(base_flat, x1, rank_tok, ye, nrm, cap, tt, final):
    n = x1.shape[0]
    nt = n // tt
    w1 = COMBINE_MAIN_ROWS
    w2 = tt + 16 - w1
    assert cap >= max(w1, w2) and cap % 16 == 0 and LANES % w1 == 0
    row = lambda w: pl.BlockSpec((tt, w), lambda t, b: (t, 0))

    def main_spec(e):
        return pl.BlockSpec(
            (pl.Squeezed(), pl.Element(w1), pl.Element(D_MODEL)),
            lambda t, b: (e, _main_start(b[e * (nt + 1) + t], cap, w1), 0))

    def over_spec(e):
        return pl.BlockSpec(
            (pl.Squeezed(), pl.Element(w2), pl.Element(D_MODEL)),
            lambda t, b: (e, _over_start(b[e * (nt + 1) + t], b[e * (nt + 1) + t + 1], cap, w1, w2)[1], 0))

    grid_spec = pltpu.PrefetchScalarGridSpec(
        num_scalar_prefetch=1,
        grid=(nt,),
        in_specs=[row(D_MODEL), row(N_EXPERTS),
                  pl.BlockSpec((1, D_MODEL), lambda t, b: (0, 0))]
                 + [main_spec(e) for e in range(N_EXPERTS)]
                 + [over_spec(e) for e in range(N_EXPERTS)],
        out_specs=row(D_MODEL),
    )
    return pl.pallas_call(
        functools.partial(_combine_kernel, nt=nt, cap=cap, w1=w1, w2=w2, final=final),
        grid_spec=grid_spec,
        out_shape=jax.ShapeDtypeStruct((n, D_MODEL), F32),
        compiler_params=_params("arbitrary"),
        name="combine",
    )(base_flat, x1, rank_tok, nrm.reshape(1, D_MODEL).astype(F32), *([ye] * (2 * N_EXPERTS)))


def _tile(n, want):
    t = min(n, want)
    assert n % t == 0, (n, want)
    return t


def _routed_ffn(x1, h2, aff, affp, wg, wu, wd, layer, nrm, final):
    n = x1.shape[0]
    cap = (EC_CAPACITY_FACTOR * n) // N_EXPERTS
    nblk = n // LANES
    tt_e = _tile(n, 1024)
    sj = _tile(cap, 512)
    tt_c = _tile(n, 256)

    aff3 = aff.reshape(nblk, LANES, N_EXPERTS).transpose(0, 2, 1)
    rankm3, rankx3 = _select(aff3, cap)
    rank_tok = rankm3.transpose(0, 2, 1).reshape(n, N_EXPERTS)
    rank4 = rankm3.transpose(1, 0, 2).reshape(N_EXPERTS, n // tt_e, tt_e // LANES, LANES)
    base128 = rankx3[:, :, 0].T
    last = jnp.full((N_EXPERTS, 1), cap, I32)
    base_e = jnp.concatenate([base128[:, ::tt_e // LANES], last], axis=1).reshape(-1)
    base_c = jnp.concatenate([base128[:, ::tt_c // LANES], last], axis=1).reshape(-1)

    ye = _experts(base_e, rank4, h2, affp, wg, wu, wd, layer, cap, tt_e, sj)
    return _combine(base_c, x1, rank_tok, ye, nrm, cap, tt_c, final)


def _trunk(x, p):
    bsz, seq, _ = x.shape
    n = bsz * seq
    depth = p["w_in"].shape[0]
    tm = _tile(seq, 512)
    tq = _tile(seq, 512)
    tk = _tile(seq, 2048)
    t_hg = _tile(seq, 512)
    tables = _rope_tables(seq)

    x2 = x.reshape(n, D_MODEL)
    for l in range(depth):
        bz, q, k, v, hg = _inproj(x2, p["norm_mix"][l], p["w_in"][l], tables, seq, tm)
        to3 = lambda a: a.reshape(bsz, seq, a.shape[-1])
        ydiff = _attention(to3(q), to3(k), to3(v), p["diff_lambda"][l], p["diff_subln"][l], l, tq, tk)
        o_f, o_b = _hgrn(to3(hg), p["hgrn_lb"], l, t_hg)
        x1, h2, aff, affp = _outproj(x2, bz, ydiff.reshape(n, DIFF_WIDTH), o_f.reshape(n, HG_WIDTH),
                               o_b.reshape(n, HG_WIDTH), hg, p["conv_w"][l], p["hgrn_norm"][l],
                               p["w_out"][l], p["norm_ffn"][l], p["router_w"][l], seq, tm)
        x2 = _routed_ffn(x1, h2, aff, affp, p["w_gate"], p["w_up"], p["w_down"], l,
                         p["norm_final"], l == depth - 1)
    return x2.reshape(bsz, seq, D_MODEL)


def _split2_param(w):
    w = w.astype(F32)
    hi = w.astype(BF16)
    lo = (w - hi.astype(F32)).astype(BF16)
    return jnp.concatenate([hi, lo], axis=-1)


def kernel(x_prompt, x_sample, norm_mix, w_in, conv_w, diff_lambda, diff_subln, hgrn_lb,
           hgrn_norm, w_out, norm_ffn, router_w, w_gate, w_up, w_down, norm_final):
    p = dict(
        norm_mix=norm_mix.astype(F32), w_in=w_in.astype(BF16), conv_w=conv_w,
        diff_lambda=diff_lambda, diff_subln=diff_subln, hgrn_lb=hgrn_lb, hgrn_norm=hgrn_norm,
        w_out=w_out.astype(BF16), norm_ffn=norm_ffn,
        router_w=_split2_param(router_w),
        w_gate=w_gate.astype(BF16), w_up=w_up.astype(BF16), w_down=w_down.astype(BF16),
        norm_final=norm_final,
    )
    return _trunk(x_prompt, p), _trunk(x_sample, p)
```

```python
import functools
import math

import jax
import jax.numpy as jnp
from jax import lax
from jax.experimental import pallas as pl
from jax.experimental.pallas import tpu as pltpu

F32 = jnp.float32
BF16 = jnp.bfloat16
I32 = jnp.int32

D_MODEL = 1024
CONV_WIDTH = 256
DIFF_WIDTH = 512
DIFF_HEADS = 4
DIFF_HEAD_DIM = 64
HG_WIDTH = 256
HG_HEADS = 4
HG_HEAD_DIM = 64
HG_CHUNK = 64
D_PROJ = 3584
ROPE_THETA = 10000.0
N_EXPERTS = 16
EC_CAPACITY_FACTOR = 2
NORM_EPS = 1e-6

VMEM_LIMIT_BYTES = 56 * 1024 * 1024
LANES = 128
GATHER_ROWS = 256
COMBINE_MAIN_ROWS = 64

NT_DIMS = (((1,), (1,)), ((), ()))
Q_SCALE = DIFF_HEAD_DIM ** -0.5 * math.log2(math.e)


def _params(*sem):
    return pltpu.CompilerParams(dimension_semantics=sem, vmem_limit_bytes=VMEM_LIMIT_BYTES)


def _split3(x):
    hi = x.astype(BF16)
    r1 = x - hi.astype(F32)
    mid = r1.astype(BF16)
    lo = (r1 - mid.astype(F32)).astype(BF16)
    return hi, mid, lo


def _dot(a, b):
    return jnp.dot(a, b, preferred_element_type=F32)


def _dot_nt(a, b):
    return lax.dot_general(a, b, NT_DIMS, preferred_element_type=F32)


def _dot_exact_lhs(a_bf16, x_f32):
    hi, mid, lo = _split3(x_f32)
    return _dot(a_bf16, hi) + _dot(a_bf16, mid) + _dot(a_bf16, lo)


def _sigmoid(x):
    return 1.0 / (1.0 + jnp.exp(-x))


def _inproj_kernel(x_ref, g_ref, w_ref, cos_ref, sina_ref, sinb_ref,
                   bz_ref, q_ref, k_ref, v_ref, hg_ref):
    x = x_ref[...]
    ms = jnp.mean(x * x, axis=-1, keepdims=True)
    h = (x * lax.rsqrt(ms + NORM_EPS) * g_ref[...]).astype(BF16)

    def proj(a, b):
        return _dot(h, w_ref[:, a:b])

    cw = CONV_WIDTH
    bz_ref[:, 0:cw] = proj(0, cw).astype(BF16)
    bz_ref[:, cw:2 * cw] = (proj(cw, 2 * cw) * proj(2 * cw, 3 * cw)).astype(BF16)

    reps = DIFF_WIDTH // LANES
    cos = jnp.concatenate([cos_ref[...]] * reps, axis=1)
    sina = jnp.concatenate([sina_ref[...]] * reps, axis=1)
    sinb = jnp.concatenate([sinb_ref[...]] * reps, axis=1)
    half = DIFF_HEAD_DIM // 2

    def rope(t):
        return (t * cos + pltpu.roll(t, DIFF_WIDTH - half, axis=1) * sina
                + pltpu.roll(t, half, axis=1) * sinb)

    o = 3 * cw
    q_ref[...] = (rope(proj(o, o + DIFF_WIDTH)) * Q_SCALE).astype(BF16)
    o += DIFF_WIDTH
    k_ref[...] = rope(proj(o, o + DIFF_WIDTH)).astype(BF16)
    o += DIFF_WIDTH
    v = proj(o, o + DIFF_WIDTH).astype(BF16)
    hd = 2 * DIFF_HEAD_DIM
    ones = jnp.ones((v.shape[0], hd), BF16)
    for head in range(DIFF_HEADS):
        v_ref[:, 2 * head * hd:(2 * head + 1) * hd] = v[:, head * hd:(head + 1) * hd]
        v_ref[:, (2 * head + 1) * hd:(2 * head + 2) * hd] = ones
    o += DIFF_WIDTH
    hg_ref[...] = proj(o, D_PROJ)


def _rope_tables(seq):
    d = DIFF_HEAD_DIM
    inv = 1.0 / (ROPE_THETA ** (jnp.arange(0, d, 2, dtype=F32) / d))
    ang = jnp.arange(seq, dtype=F32)[:, None] * inv[None, :]
    ang = jnp.concatenate([ang, ang, ang, ang], axis=-1)
    cos, sin = jnp.cos(ang), jnp.sin(ang)
    first = (jnp.arange(LANES) % d) < (d // 2)
    sina = jnp.where(first[None, :], -sin, 0.0)
    sinb = jnp.where(first[None, :], 0.0, sin)
    return cos, sina, sinb


def _inproj(x2, g, w_bf16, tables, seq, tm):
    n = x2.shape[0]
    nblk_s = seq // tm
    cos, sina, sinb = tables
    tab_spec = pl.BlockSpec((tm, LANES), lambda i: (i % nblk_s, 0))
    row = lambda w: pl.BlockSpec((tm, w), lambda i: (i, 0))
    hgw = 5 * HG_WIDTH
    return pl.pallas_call(
        _inproj_kernel,
        grid=(n // tm,),
        in_specs=[row(D_MODEL),
                  pl.BlockSpec((1, D_MODEL), lambda i: (0, 0)),
                  pl.BlockSpec((D_MODEL, D_PROJ), lambda i: (0, 0)),
                  tab_spec, tab_spec, tab_spec],
        out_specs=[row(2 * CONV_WIDTH), row(DIFF_WIDTH), row(DIFF_WIDTH), row(2 * DIFF_WIDTH), row(hgw)],
        out_shape=[jax.ShapeDtypeStruct((n, 2 * CONV_WIDTH), BF16),
                   jax.ShapeDtypeStruct((n, DIFF_WIDTH), BF16),
                   jax.ShapeDtypeStruct((n, DIFF_WIDTH), BF16),
                   jax.ShapeDtypeStruct((n, 2 * DIFF_WIDTH), BF16),
                   jax.ShapeDtypeStruct((n, hgw), F32)],
        compiler_params=_params("parallel"),
        name="inproj",
    )(x2, g.reshape(1, D_MODEL), w_bf16, cos, sina, sinb)


def _attn_kernel(q_ref, k_ref, v_ref, lam_ref, g_ref, o_ref, *, tk, lam_init):
    q = q_ref[0]
    tq = q.shape[0]
    seq = k_ref.shape[1]
    nblk = seq // tk
    d = DIFF_HEAD_DIM
    hd = 2 * d
    lane = lax.broadcasted_iota(I32, q.shape, 1)
    zero = jnp.zeros_like(q)
    qs = (jnp.where(lane < d, q, zero), jnp.where(lane >= d, q, zero))

    def scores(j):
        ks = k_ref[0, j * tk:(j + 1) * tk, :]
        out = []
        for h in range(2):
            s = _dot_nt(qs[h], ks)
            out.append((s, jnp.max(s, axis=1, keepdims=True)))
        return out

    def consume(j, sc, state):
        vs = v_ref[0, j * tk:(j + 1) * tk, :]
        out = []
        for h in range(2):
            m, acc = state[h]
            s, bm = sc[h]
            mn = jnp.maximum(m, bm)
            p = jnp.exp2(s - mn).astype(BF16)
            acc = jnp.exp2(m - mn) * acc + _dot(p, vs)
            out.append((mn, acc))
        return out

    init = (jnp.full((tq, 1), -1e30, F32), jnp.zeros((tq, 2 * hd), F32))
    state = [init, init]
    sc = scores(0)
    for j in range(nblk):
        sc_next = scores(j + 1) if j + 1 < nblk else None
        state = consume(j, sc, state)
        sc = sc_next
    (_, a1), (_, a2) = state

    lp = lam_ref[...]
    lam = (jnp.exp(jnp.sum(lp[0:1] * lp[1:2], axis=1, keepdims=True))
           - jnp.exp(jnp.sum(lp[2:3] * lp[3:4], axis=1, keepdims=True)) + lam_init)
    o = a1[:, :hd] / a1[:, hd:] - lam * (a2[:, :hd] / a2[:, hd:])
    ms = jnp.mean(o * o, axis=-1, keepdims=True)
    y = o * lax.rsqrt(ms + NORM_EPS) * g_ref[...] * (1.0 - lam_init)
    o_ref[0] = y.astype(BF16)


def _attention(q3, k3, v3, lam_params, subln_g, layer, tq, tk):
    bsz, seq, _ = q3.shape
    lam_init = 0.8 - 0.6 * math.exp(-0.3 * layer)
    hd = 2 * DIFF_HEAD_DIM
    return pl.pallas_call(
        functools.partial(_attn_kernel, tk=tk, lam_init=lam_init),
        grid=(bsz, DIFF_HEADS, seq // tq),
        in_specs=[pl.BlockSpec((1, tq, hd), lambda b, h, i: (b, i, h)),
                  pl.BlockSpec((1, seq, hd), lambda b, h, i: (b, 0, h)),
                  pl.BlockSpec((1, seq, 2 * hd), lambda b, h, i: (b, 0, h)),
                  pl.BlockSpec((4, DIFF_HEAD_DIM), lambda b, h, i: (0, 0)),
                  pl.BlockSpec((1, hd), lambda b, h, i: (0, 0))],
        out_specs=pl.BlockSpec((1, tq, hd), lambda b, h, i: (b, i, h)),
        out_shape=jax.ShapeDtypeStruct((bsz, seq, DIFF_WIDTH), BF16),
        compiler_params=_params("parallel", "parallel", "parallel"),
        name="attn",
    )(q3, k3, v3, lam_params.astype(F32), subln_g.reshape(1, hd).astype(F32))


def _hgrn_kernel(qf_ref, ff_ref, if_ref, qb_ref, fb_ref, ib_ref, lbp_ref,
                 of_ref, ob_ref, stf_ref, stb_ref, *, layer):
    c_len = HG_CHUNK
    w = HG_WIDTH
    hd = HG_HEAD_DIM
    t_len = qf_ref.shape[1]
    nchunk = t_len // c_len

    @pl.when(pl.program_id(1) == 0)
    def _():
        stf_ref[...] = jnp.zeros_like(stf_ref)
        stb_ref[...] = jnp.zeros_like(stb_ref)

    lbp = lbp_ref[...]
    e = jnp.exp(lbp - jnp.max(lbp, axis=0, keepdims=True))
    sm = e / jnp.sum(e, axis=0, keepdims=True)
    lb = jnp.zeros((1, 2 * w), F32)
    for l in range(1, layer + 1):
        lb = lb + sm[l:l + 1]
    lb_f, lb_b = lb[:, :w], lb[:, w:]

    r = lax.broadcasted_iota(I32, (c_len, c_len), 0)
    c = lax.broadcasted_iota(I32, (c_len, c_len), 1)
    low = r >= c
    upp = r <= c
    low_m = jnp.where(low, 1.0, 0.0).astype(BF16)
    upp_m = jnp.where(upp, 1.0, 0.0).astype(BF16)
    lane = lax.broadcasted_iota(I32, (1, w), 1)
    heads = [(lane >= h * hd) & (lane < (h + 1) * hd) for h in range(HG_HEADS)]
    br = lax.broadcasted_iota(I32, (w, w), 0) // hd
    bc = lax.broadcasted_iota(I32, (w, w), 1) // hd
    same_head = br == bc
    eye = jnp.where(lax.broadcasted_iota(I32, (w, w), 0) == lax.broadcasted_iota(I32, (w, w), 1),
                    1.0, 0.0).astype(BF16)
    mid = c_len // 2

    def bmm(a, b):
        return jnp.einsum('nik,nkj->nij', a, b, preferred_element_type=F32)

    def bmm_nt(a, b):
        return jnp.einsum('nik,njk->nij', a, b, preferred_element_type=F32)

    def scan_block(xq, xf, xi, lbd, tri_m, tri, last, order, st_ref):
        c3 = lambda a: a.reshape(nchunk, c_len, w)
        qh = c3(xq * _sigmoid(xq) * (hd ** -0.5))
        f = lbd + (1.0 - lbd) * _sigmoid(xf)
        kk = c3(1.0 - f)
        tri_b = jnp.broadcast_to(tri_m[None], (nchunk, c_len, c_len))
        g_hi, g_mid, g_lo = _split3(c3(jnp.log(f)))
        G = bmm(tri_b, g_hi) + bmm(tri_b, g_mid) + bmm(tri_b, g_lo)
        g_ref_row = G[:, mid:mid + 1, :]
        g_last = G[:, last:last + 1, :]
        qt = (qh * jnp.exp(G - g_ref_row)).astype(BF16)
        kt = (kk * jnp.exp(g_ref_row - G)).astype(BF16)
        q_in = (qh * jnp.exp(G)).astype(BF16)
        k_out = (kk * jnp.exp(g_last - G)).astype(BF16)
        vb = c3(xi).astype(BF16)
        zq = jnp.zeros_like(qt)
        o = jnp.zeros((nchunk, c_len, w), F32)
        for h in range(HG_HEADS):
            a = bmm_nt(jnp.where(heads[h][None], qt, zq), kt)
            a = jnp.where(tri[None], a, 0.0).astype(BF16)
            o = o + jnp.where(heads[h][None], bmm(a, vb), 0.0)
        eye_b = jnp.broadcast_to(eye[None], (nchunk, w, w))
        v_t = bmm_nt(eye_b, vb).astype(BF16)
        upd = jnp.where(same_head[None], bmm(v_t, k_out), 0.0)
        decay = jnp.exp(g_last)
        st = st_ref[...]
        inter = [None] * nchunk
        for ci in order:
            inter[ci] = _dot_nt(q_in[ci], st.astype(BF16))
            st = st * decay[ci] + upd[ci]
        st_ref[...] = st
        return o.reshape(nchunk * c_len, w) + jnp.concatenate(inter, axis=0)

    fwd_order = list(range(nchunk))
    of_ref[0] = scan_block(qf_ref[0], ff_ref[0], if_ref[0], lb_f, low_m, low, c_len - 1,
                           fwd_order, stf_ref)
    ob_ref[0] = scan_block(qb_ref[0], fb_ref[0], ib_ref[0], lb_b, upp_m, upp, 0,
                           fwd_order[::-1], stb_ref)


def _hgrn(hg3, hgrn_lb, layer, t_len):
    bsz, seq, _ = hg3.shape
    nt = seq // t_len
    w = HG_WIDTH
    fwd = lambda col: pl.BlockSpec((1, t_len, w), lambda b, j: (b, j, col))
    bwd = lambda col: pl.BlockSpec((1, t_len, w), lambda b, j: (b, nt - 1 - j, col))
    depth = hgrn_lb.shape[0]
    return pl.pallas_call(
        functools.partial(_hgrn_kernel, layer=layer),
        grid=(bsz, nt),
        in_specs=[fwd(0), fwd(1), fwd(3), bwd(0), bwd(2), bwd(3),
                  pl.BlockSpec((depth, 2 * w), lambda b, j: (0, 0))],
        out_specs=[pl.BlockSpec((1, t_len, w), lambda b, j: (b, j, 0)),
                   pl.BlockSpec((1, t_len, w), lambda b, j: (b, nt - 1 - j, 0))],
        out_shape=[jax.ShapeDtypeStruct((bsz, seq, w), F32),
                   jax.ShapeDtypeStruct((bsz, seq, w), F32)],
        scratch_shapes=[pltpu.VMEM((w, w), F32), pltpu.VMEM((w, w), F32)],
        compiler_params=_params("parallel", "arbitrary"),
        name="hgrn",
    )(hg3, hg3, hg3, hg3, hg3, hg3, hgrn_lb.astype(F32))


def _outproj_kernel(x_ref, bz_ref, zp_ref, zn_ref, yd_ref, of_ref, ob_ref, hgate_ref,
                    cw_ref, hn_ref, wo_ref, nf_ref, rw_ref,
                    x1_ref, h2_ref, aff_ref, affp_ref, *, blocks_per_seq):
    tm = x_ref.shape[0]
    cwid = CONV_WIDTH
    i = pl.program_id(0)
    pos = i % blocks_per_seq
    halo = zp_ref.shape[0]

    b = bz_ref[:, 0:cwid].astype(F32)
    z = bz_ref[:, cwid:2 * cwid].astype(F32)
    zprev = zp_ref[:, cwid:2 * cwid].astype(F32)[halo - 1:halo]
    znext = zn_ref[:, cwid:2 * cwid].astype(F32)[0:1]
    zprev = jnp.where(pos == 0, 0.0, zprev)
    znext = jnp.where(pos == blocks_per_seq - 1, 0.0, znext)
    row = lax.broadcasted_iota(I32, (tm, cwid), 0)
    z_up = jnp.where(row == 0, zprev, pltpu.roll(z, 1, axis=0))
    z_dn = jnp.where(row == tm - 1, znext, pltpu.roll(z, tm - 1, axis=0))
    cw = cw_ref[...]
    y_conv = b * (cw[0:1] * z_up + cw[1:2] * z + cw[2:3] * z_dn)

    o = of_ref[...] + ob_ref[...]
    w = HG_WIDTH
    br = lax.broadcasted_iota(I32, (w, w), 0) // HG_HEAD_DIM
    bc = lax.broadcasted_iota(I32, (w, w), 1) // HG_HEAD_DIM
    pool = jnp.where(br == bc, 1.0, 0.0).astype(BF16)
    sq = o * o
    sq_hi = sq.astype(BF16)
    sq_lo = (sq - sq_hi.astype(F32)).astype(BF16)
    ms = (_dot(sq_hi, pool) + _dot(sq_lo, pool)) * (1.0 / HG_HEAD_DIM)
    gt = hgate_ref[...]
    y_h = o * lax.rsqrt(ms + NORM_EPS) * hn_ref[...] * (gt * _sigmoid(gt))

    mixed = (_dot(y_conv.astype(BF16), wo_ref[0:cwid, :])
             + _dot(yd_ref[...], wo_ref[cwid:cwid + DIFF_WIDTH, :])
             + _dot(y_h.astype(BF16), wo_ref[cwid + DIFF_WIDTH:, :]))
    x1 = x_ref[...] + mixed
    x1_ref[...] = x1

    ms2 = jnp.mean(x1 * x1, axis=-1, keepdims=True)
    h2 = x1 * lax.rsqrt(ms2 + NORM_EPS) * nf_ref[...]
    h_hi = h2.astype(BF16)
    h2_ref[...] = h_hi

    h_lo = (h2 - h_hi.astype(F32)).astype(BF16)
    rw = rw_ref[...]
    d_hi = _dot(h_hi, rw)
    d_lo = _dot(h_lo, rw)
    ne = N_EXPERTS
    logits = d_hi[:, :ne] + (d_hi[:, ne:] + d_lo[:, :ne]) + d_lo[:, ne:]
    ex = jnp.exp(logits - jnp.max(logits, axis=-1, keepdims=True))
    aff = ex / jnp.sum(ex, axis=-1, keepdims=True)
    aff_ref[...] = aff
    a_hi = aff.astype(BF16)
    a_lo = (aff - a_hi.astype(F32)).astype(BF16)
    pad = jnp.zeros((aff.shape[0], LANES - 2 * ne), BF16)
    affp_ref[...] = jnp.concatenate([a_hi, a_lo, pad], axis=1)


def _outproj(x2, bz, ydiff, of2, ob2, hg2, conv_w, hgrn_norm, wo_bf16, norm_ffn, rw3, seq, tm):
    n = x2.shape[0]
    halo = 16
    hb = tm // halo
    nhalo = n // halo
    row = lambda w: pl.BlockSpec((tm, w), lambda i: (i, 0))
    full = lambda a, b: pl.BlockSpec((a, b), lambda i: (0, 0))
    hn = jnp.tile(hgrn_norm.astype(F32), HG_HEADS).reshape(1, HG_WIDTH)
    return pl.pallas_call(
        functools.partial(_outproj_kernel, blocks_per_seq=seq // tm),
        grid=(n // tm,),
        in_specs=[row(D_MODEL), row(2 * CONV_WIDTH),
                  pl.BlockSpec((halo, 2 * CONV_WIDTH), lambda i: (jnp.maximum(i * hb - 1, 0), 0)),
                  pl.BlockSpec((halo, 2 * CONV_WIDTH), lambda i: (jnp.minimum((i + 1) * hb, nhalo - 1), 0)),
                  row(DIFF_WIDTH), row(HG_WIDTH), row(HG_WIDTH),
                  pl.BlockSpec((tm, HG_WIDTH), lambda i: (i, 4)),
                  full(3, CONV_WIDTH), full(1, HG_WIDTH), full(D_MODEL, D_MODEL), full(1, D_MODEL),
                  full(D_MODEL, 2 * N_EXPERTS)],
        out_specs=[row(D_MODEL), row(D_MODEL), row(N_EXPERTS), row(LANES)],
        out_shape=[jax.ShapeDtypeStruct((n, D_MODEL), F32),
                   jax.ShapeDtypeStruct((n, D_MODEL), BF16),
                   jax.ShapeDtypeStruct((n, N_EXPERTS), F32),
                   jax.ShapeDtypeStruct((n, LANES), BF16)],
        compiler_params=_params("parallel"),
        name="outproj",
    )(x2, bz, bz, bz, ydiff, of2, ob2, hg2, conv_w.astype(F32), hn, wo_bf16,
      norm_ffn.reshape(1, D_MODEL).astype(F32), rw3)


def _select_kernel(aff_ref, rankm_ref, rankx_ref, pre_ref, tot_ref, *, cap):
    nblk = aff_ref.shape[0]
    ne = N_EXPERTS
    bits = pltpu.bitcast(aff_ref[...], I32)

    def count(mask):
        part = jnp.sum(jnp.where(mask, 1.0, 0.0), axis=0)
        return jnp.sum(part, axis=1, keepdims=True)

    def search(i, tau):
        cand = tau | jnp.left_shift(jnp.int32(1), 30 - i)
        return jnp.where(count(bits >= cand[None]) >= cap, cand, tau)

    tau = lax.fori_loop(0, 31, search, jnp.zeros((ne, 1), I32))[None]

    r = lax.broadcasted_iota(I32, (LANES, LANES), 0)
    c = lax.broadcasted_iota(I32, (LANES, LANES), 1)
    incl = jnp.where(r <= c, 1.0, 0.0).astype(BF16)
    ones = jnp.ones((LANES, LANES), BF16)

    def prefix(mask):
        m = jnp.where(mask, 1.0, 0.0)
        m2 = m.reshape(nblk * ne, LANES).astype(BF16)
        pre_ref[...] = _dot(m2, incl).reshape(nblk, ne, LANES) - m
        tot_ref[...] = _dot(m2, ones).reshape(nblk, ne, LANES)

        def carry(b, run):
            pre_ref[b] = pre_ref[b] + run
            return run + tot_ref[b]

        lax.fori_loop(0, nblk, carry, jnp.zeros((ne, LANES), F32))
        return pre_ref[...]

    above = bits > tau
    tie = bits == tau
    need = cap - count(above)
    sel = above | (tie & (prefix(tie) < need[None]))
    rank = prefix(sel).astype(I32)
    rankx_ref[...] = rank
    rankm_ref[...] = jnp.where(sel, rank, -1)


def _select(aff3, cap):
    nblk = aff3.shape[0]
    shp = (nblk, N_EXPERTS, LANES)
    spec = pl.BlockSpec(shp, lambda i: (0, 0, 0))
    return pl.pallas_call(
        functools.partial(_select_kernel, cap=cap),
        grid=(1,),
        in_specs=[spec],
        out_specs=[spec, spec],
        out_shape=[jax.ShapeDtypeStruct(shp, I32), jax.ShapeDtypeStruct(shp, I32)],
        scratch_shapes=[pltpu.VMEM(shp, F32), pltpu.VMEM(shp, F32)],
        compiler_params=_params("arbitrary"),
        name="select",
    )(aff3)


def _expert_kernel(base_ref, rank_ref, h_ref, g_ref, wg_ref, wu_ref, wd_ref, ye_ref, xe_ref, xg_ref,
                   *, nt, tt, sj, cap, win):
    e = pl.program_id(0)
    s = pl.program_id(1)

    @pl.when(s == 0)
    def _():
        xe_ref[...] = jnp.zeros_like(xe_ref)
        xg_ref[...] = jnp.zeros_like(xg_ref)

    @pl.when(s < nt)
    def _():
        base = base_ref[e * (nt + 1) + s]
        count = base_ref[e * (nt + 1) + s + 1] - base
        wb = (base // 8) * 8
        nwin = jnp.where(count > 0, (base - wb + count + win - 1) // win, 0)
        rk = rank_ref[0, 0]
        rel = jnp.concatenate([rk[a:a + 1, :] for a in range(tt // LANES)], axis=1) - wb
        slot = lax.broadcasted_iota(I32, (win, tt), 0)

        def fill(w, carry):
            onehot = jnp.where(slot == rel - w * win, 1.0, 0.0).astype(BF16)
            r0 = pl.multiple_of(wb + w * win, 8)
            xe_ref[pl.ds(r0, win), :] += _dot(onehot, h_ref[...])
            xg_ref[pl.ds(r0, win), :] += _dot(onehot, g_ref[...])
            return carry

        lax.fori_loop(0, nwin, fill, 0)

    @pl.when(s >= nt)
    def _():
        j0 = pl.multiple_of((s - nt) * sj, sj)
        xj = xe_ref[pl.ds(j0, sj), :].astype(BF16)
        a = _dot(xj, wg_ref[0])
        u = _dot(xj, wu_ref[0])
        hid = (a * _sigmoid(a) * u).astype(BF16)
        lane = lax.broadcasted_iota(I32, (sj, LANES), 1)
        mine = (lane == e) | (lane == N_EXPERTS + e)
        gate = jnp.sum(jnp.where(mine, xg_ref[pl.ds(j0, sj), :], 0.0), axis=1, keepdims=True)
        ye_ref[0] = (_dot(hid, wd_ref[0]) * gate).astype(BF16)


def _experts(base_flat, rank4, h2, affp, wg, wu, wd, layer, cap, tt, sj):
    n = h2.shape[0]
    nt = n // tt
    nj = cap // sj
    win = GATHER_ROWS
    wspec = pl.BlockSpec((pl.Squeezed(), 1, D_MODEL, D_MODEL), lambda e, s, b: (layer, e, 0, 0))
    grid_spec = pltpu.PrefetchScalarGridSpec(
        num_scalar_prefetch=1,
        grid=(N_EXPERTS, nt + nj),
        in_specs=[pl.BlockSpec((1, 1, tt // LANES, LANES),
                               lambda e, s, b: (e, jnp.minimum(s, nt - 1), 0, 0)),
                  pl.BlockSpec((tt, D_MODEL), lambda e, s, b: (jnp.minimum(s, nt - 1), 0)),
                  pl.BlockSpec((tt, LANES), lambda e, s, b: (jnp.minimum(s, nt - 1), 0)),
                  wspec, wspec, wspec],
        out_specs=pl.BlockSpec((1, sj, D_MODEL), lambda e, s, b: (e, jnp.maximum(s - nt, 0), 0)),
        scratch_shapes=[pltpu.VMEM((cap + win + 8, D_MODEL), F32),
                        pltpu.VMEM((cap + win + 8, LANES), F32)],
    )
    return pl.pallas_call(
        functools.partial(_expert_kernel, nt=nt, tt=tt, sj=sj, cap=cap, win=win),
        grid_spec=grid_spec,
        out_shape=jax.ShapeDtypeStruct((N_EXPERTS, cap, D_MODEL), BF16),
        compiler_params=_params("arbitrary", "arbitrary"),
        name="experts",
    )(base_flat, rank4, h2, affp, wg, wu, wd)


def _main_start(base, cap, w1):
    return pl.multiple_of(jnp.minimum((base // 16) * 16, cap - w1), 16)


def _over_start(base, nxt, cap, w1, w2):
    wb = _main_start(base, cap, w1)
    need = nxt - wb > w1
    return need, pl.multiple_of(jnp.where(need, jnp.minimum(wb + w1, cap - w2), 0), 16)


def _combine_kernel(base_ref, x_ref, rank_ref, nrm_ref, *rest, nt, cap, w1, w2, final):
    ne = N_EXPERTS
    main_refs = rest[:ne]
    over_refs = rest[ne:2 * ne]
    out_ref = rest[2 * ne]
    t = pl.program_id(0)
    tt = x_ref.shape[0]
    rank = rank_ref[...]
    bases = [base_ref[e * (nt + 1) + t] for e in range(ne)]
    nexts = [base_ref[e * (nt + 1) + t + 1] for e in range(ne)]

    col = lax.broadcasted_iota(I32, (1, ne), 1)
    wb_row = jnp.zeros((1, ne), I32)
    for e in range(ne):
        wb_row = jnp.where(col == e, _main_start(bases[e], cap, w1), wb_row)
    rel = rank - wb_row
    adj = jnp.where((rank >= 0) & (rel < w1), rel, -1)
    per_vreg = LANES // w1
    lane = lax.broadcasted_iota(I32, (tt, LANES), 1)
    onehots = []
    for i in range(ne // per_vreg):
        target = jnp.full((tt, LANES), -1, I32)
        for a in range(per_vreg):
            e = i * per_vreg + a
            inside = (lane >= a * w1) & (lane < (a + 1) * w1)
            target = jnp.where(inside, adj[:, e:e + 1] + a * w1, target)
        onehots.append(jnp.where(lane == target, 1.0, 0.0).astype(BF16))
    y_all = jnp.concatenate([r[...] for r in main_refs], axis=0)
    out_ref[...] = x_ref[...] + _dot(jnp.concatenate(onehots, axis=1), y_all)

    for e in range(ne):
        need, wo = _over_start(bases[e], nexts[e], cap, w1, w2)

        @pl.when(need)
        def _(e=e, wo=wo):
            lane2 = lax.broadcasted_iota(I32, (tt, w2), 1)
            first = _main_start(bases[e], cap, w1) + w1
            r = rank_ref[:, e:e + 1]
            match = (lane2 == r - wo) & (r >= first)
            onehot = jnp.where(match, 1.0, 0.0).astype(BF16)
            out_ref[...] += _dot(onehot, over_refs[e][...])

    if final:
        o = out_ref[...]
        ms = jnp.mean(o * o, axis=-1, keepdims=True)
        out_ref[...] = o * lax.rsqrt(ms + NORM_EPS) * nrm_ref[...]


def _combine(base_flat, x1, rank_tok, ye, nrm, cap, tt, final):
    n = x1.shape[0]
    nt = n // tt
    w1 = COMBINE_MAIN_ROWS
    w2 = tt + 16 - w1
    assert cap >= max(w1, w2) and cap % 16 == 0 and LANES % w1 == 0
    row = lambda w: pl.BlockSpec((tt, w), lambda t, b: (t, 0))

    def main_spec(e):
        return pl.BlockSpec(
            (pl.Squeezed(), pl.Element(w1), pl.Element(D_MODEL)),
            lambda t, b: (e, _main_start(b[e * (nt + 1) + t], cap, w1), 0))

    def over_spec(e):
        return pl.BlockSpec(
            (pl.Squeezed(), pl.Element(w2), pl.Element(D_MODEL)),
            lambda t, b: (e, _over_start(b[e * (nt + 1) + t], b[e * (nt + 1) + t + 1], cap, w1, w2)[1], 0))

    grid_spec = pltpu.PrefetchScalarGridSpec(
        num_scalar_prefetch=1,
        grid=(nt,),
        in_specs=[row(D_MODEL), row(N_EXPERTS),
                  pl.BlockSpec((1, D_MODEL), lambda t, b: (0, 0))]
                 + [main_spec(e) for e in range(N_EXPERTS)]
                 + [over_spec(e) for e in range(N_EXPERTS)],
        out_specs=row(D_MODEL),
    )
    return pl.pallas_call(
        functools.partial(_combine_kernel, nt=nt, cap=cap, w1=w1, w2=w2, final=final),
        grid_spec=grid_spec,
        out_shape=jax.ShapeDtypeStruct((n, D_MODEL), F32),
        compiler_params=_params("arbitrary"),
        name="combine",
    )(base_flat, x1, rank_tok, nrm.reshape(1, D_MODEL).astype(F32), *([ye] * (2 * N_EXPERTS)))


def _tile(n, want):
    t = min(n, want)
    assert n % t == 0, (n, want)
    return t


def _routed_ffn(x1, h2, aff, affp, wg, wu, wd, layer, nrm, final):
    n = x1.shape[0]
    cap = (EC_CAPACITY_FACTOR * n) // N_EXPERTS
    nblk = n // LANES
    tt_e = _tile(n, 1024)
    sj = _tile(cap, 512)
    tt_c = _tile(n, 256)

    aff3 = aff.reshape(nblk, LANES, N_EXPERTS).transpose(0, 2, 1)
    rankm3, rankx3 = _select(aff3, cap)
    rank_tok = rankm3.transpose(0, 2, 1).reshape(n, N_EXPERTS)
    rank4 = rankm3.transpose(1, 0, 2).reshape(N_EXPERTS, n // tt_e, tt_e // LANES, LANES)
    base128 = rankx3[:, :, 0].T
    last = jnp.full((N_EXPERTS, 1), cap, I32)
    base_e = jnp.concatenate([base128[:, ::tt_e // LANES], last], axis=1).reshape(-1)
    base_c = jnp.concatenate([base128[:, ::tt_c // LANES], last], axis=1).reshape(-1)

    ye = _experts(base_e, rank4, h2, affp, wg, wu, wd, layer, cap, tt_e, sj)
    return _combine(base_c, x1, rank_tok, ye, nrm, cap, tt_c, final)


def _trunk(x, p):
    bsz, seq, _ = x.shape
    n = bsz * seq
    depth = p["w_in"].shape[0]
    tm = _tile(seq, 512)
    tq = _tile(seq, 512)
    tk = _tile(seq, 2048)
    t_hg = _tile(seq, 512)
    tables = _rope_tables(seq)

    x2 = x.reshape(n, D_MODEL)
    for l in range(depth):
        bz, q, k, v, hg = _inproj(x2, p["norm_mix"][l], p["w_in"][l], tables, seq, tm)
        to3 = lambda a: a.reshape(bsz, seq, a.shape[-1])
        ydiff = _attention(to3(q), to3(k), to3(v), p["diff_lambda"][l], p["diff_subln"][l], l, tq, tk)
        o_f, o_b = _hgrn(to3(hg), p["hgrn_lb"], l, t_hg)
        x1, h2, aff, affp = _outproj(x2, bz, ydiff.reshape(n, DIFF_WIDTH), o_f.reshape(n, HG_WIDTH),
                               o_b.reshape(n, HG_WIDTH), hg, p["conv_w"][l], p["hgrn_norm"][l],
                               p["w_out"][l], p["norm_ffn"][l], p["router_w"][l], seq, tm)
        x2 = _routed_ffn(x1, h2, aff, affp, p["w_gate"], p["w_up"], p["w_down"], l,
                         p["norm_final"], l == depth - 1)
    return x2.reshape(bsz, seq, D_MODEL)


def _split2_param(w):
    w = w.astype(F32)
    hi = w.astype(BF16)
    lo = (w - hi.astype(F32)).astype(BF16)
    return jnp.concatenate([hi, lo], axis=-1)


def kernel(x_prompt, x_sample, norm_mix, w_in, conv_w, diff_lambda, diff_subln, hgrn_lb,
           hgrn_norm, w_out, norm_ffn, router_w, w_gate, w_up, w_down, norm_final):
    p = dict(
        norm_mix=norm_mix.astype(F32), w_in=w_in.astype(BF16), conv_w=conv_w,
        diff_lambda=diff_lambda, diff_subln=diff_subln, hgrn_lb=hgrn_lb, hgrn_norm=hgrn_norm,
        w_out=w_out.astype(BF16), norm_ffn=norm_ffn,
        router_w=_split2_param(router_w),
        w_gate=w_gate.astype(BF16), w_up=w_up.astype(BF16), w_down=w_down.astype(BF16),
        norm_final=norm_final,
    )
    return _trunk(x_prompt, p), _trunk(x_sample, p)
```

```python
import functools
import math

import jax
import jax.numpy as jnp
from jax import lax
from jax.experimental import pallas as pl
from jax.experimental.pallas import tpu as pltpu

F32 = jnp.float32
BF16 = jnp.bfloat16
I32 = jnp.int32

D_MODEL = 1024
CONV_WIDTH = 256
DIFF_WIDTH = 512
DIFF_HEADS = 4
DIFF_HEAD_DIM = 64
HG_WIDTH = 256
HG_HEADS = 4
HG_HEAD_DIM = 64
HG_CHUNK = 64
D_PROJ = 3584
ROPE_THETA = 10000.0
N_EXPERTS = 16
EC_CAPACITY_FACTOR = 2
NORM_EPS = 1e-6

VMEM_LIMIT_BYTES = 56 * 1024 * 1024
LANES = 128
GATHER_ROWS = 256
COMBINE_MAIN_ROWS = 64

NT_DIMS = (((1,), (1,)), ((), ()))
Q_SCALE = DIFF_HEAD_DIM ** -0.5 * math.log2(math.e)


def _params(*sem):
    return pltpu.CompilerParams(dimension_semantics=sem, vmem_limit_bytes=VMEM_LIMIT_BYTES)


def _split3(x):
    hi = x.astype(BF16)
    r1 = x - hi.astype(F32)
    mid = r1.astype(BF16)
    lo = (r1 - mid.astype(F32)).astype(BF16)
    return hi, mid, lo


def _dot(a, b):
    return jnp.dot(a, b, preferred_element_type=F32)


def _dot_nt(a, b):
    return lax.dot_general(a, b, NT_DIMS, preferred_element_type=F32)


def _dot_exact_lhs(a_bf16, x_f32):
    hi, mid, lo = _split3(x_f32)
    return _dot(a_bf16, hi) + _dot(a_bf16, mid) + _dot(a_bf16, lo)


def _sigmoid(x):
    return 1.0 / (1.0 + jnp.exp(-x))


def _inproj_kernel(x_ref, g_ref, w_ref, cos_ref, sina_ref, sinb_ref,
                   bz_ref, q_ref, k_ref, v_ref, hg_ref):
    x = x_ref[...]
    ms = jnp.mean(x * x, axis=-1, keepdims=True)
    h = (x * lax.rsqrt(ms + NORM_EPS) * g_ref[...]).astype(BF16)

    def proj(a, b):
        return _dot(h, w_ref[:, a:b])

    cw = CONV_WIDTH
    bz_ref[:, 0:cw] = proj(0, cw).astype(BF16)
    bz_ref[:, cw:2 * cw] = (proj(cw, 2 * cw) * proj(2 * cw, 3 * cw)).astype(BF16)

    reps = DIFF_WIDTH // LANES
    cos = jnp.concatenate([cos_ref[...]] * reps, axis=1)
    sina = jnp.concatenate([sina_ref[...]] * reps, axis=1)
    sinb = jnp.concatenate([sinb_ref[...]] * reps, axis=1)
    half = DIFF_HEAD_DIM // 2

    def rope(t):
        return (t * cos + pltpu.roll(t, DIFF_WIDTH - half, axis=1) * sina
                + pltpu.roll(t, half, axis=1) * sinb)

    o = 3 * cw
    q_ref[...] = (rope(proj(o, o + DIFF_WIDTH)) * Q_SCALE).astype(BF16)
    o += DIFF_WIDTH
    k_ref[...] = rope(proj(o, o + DIFF_WIDTH)).astype(BF16)
    o += DIFF_WIDTH
    v = proj(o, o + DIFF_WIDTH).astype(BF16)
    hd = 2 * DIFF_HEAD_DIM
    ones = jnp.ones((v.shape[0], hd), BF16)
    for head in range(DIFF_HEADS):
        v_ref[:, 2 * head * hd:(2 * head + 1) * hd] = v[:, head * hd:(head + 1) * hd]
        v_ref[:, (2 * head + 1) * hd:(2 * head + 2) * hd] = ones
    o += DIFF_WIDTH
    hg_ref[...] = proj(o, D_PROJ)


def _rope_tables(seq):
    d = DIFF_HEAD_DIM
    inv = 1.0 / (ROPE_THETA ** (jnp.arange(0, d, 2, dtype=F32) / d))
    ang = jnp.arange(seq, dtype=F32)[:, None] * inv[None, :]
    ang = jnp.concatenate([ang, ang, ang, ang], axis=-1)
    cos, sin = jnp.cos(ang), jnp.sin(ang)
    first = (jnp.arange(LANES) % d) < (d // 2)
    sina = jnp.where(first[None, :], -sin, 0.0)
    sinb = jnp.where(first[None, :], 0.0, sin)
    return cos, sina, sinb


def _inproj(x2, g, w_bf16, tables, seq, tm):
    n = x2.shape[0]
    nblk_s = seq // tm
    cos, sina, sinb = tables
    tab_spec = pl.BlockSpec((tm, LANES), lambda i: (i % nblk_s, 0))
    row = lambda w: pl.BlockSpec((tm, w), lambda i: (i, 0))
    hgw = 5 * HG_WIDTH
    return pl.pallas_call(
        _inproj_kernel,
        grid=(n // tm,),
        in_specs=[row(D_MODEL),
                  pl.BlockSpec((1, D_MODEL), lambda i: (0, 0)),
                  pl.BlockSpec((D_MODEL, D_PROJ), lambda i: (0, 0)),
                  tab_spec, tab_spec, tab_spec],
        out_specs=[row(2 * CONV_WIDTH), row(DIFF_WIDTH), row(DIFF_WIDTH), row(2 * DIFF_WIDTH), row(hgw)],
        out_shape=[jax.ShapeDtypeStruct((n, 2 * CONV_WIDTH), BF16),
                   jax.ShapeDtypeStruct((n, DIFF_WIDTH), BF16),
                   jax.ShapeDtypeStruct((n, DIFF_WIDTH), BF16),
                   jax.ShapeDtypeStruct((n, 2 * DIFF_WIDTH), BF16),
                   jax.ShapeDtypeStruct((n, hgw), F32)],
        compiler_params=_params("parallel"),
        name="inproj",
    )(x2, g.reshape(1, D_MODEL), w_bf16, cos, sina, sinb)


def _attn_kernel(q_ref, k_ref, v_ref, lam_ref, g_ref, o_ref, *, tk, lam_init):
    q = q_ref[0]
    tq = q.shape[0]
    seq = k_ref.shape[1]
    nblk = seq // tk
    d = DIFF_HEAD_DIM
    hd = 2 * d
    lane = lax.broadcasted_iota(I32, q.shape, 1)
    zero = jnp.zeros_like(q)
    qs = (jnp.where(lane < d, q, zero), jnp.where(lane >= d, q, zero))

    def scores(j):
        ks = k_ref[0, j * tk:(j + 1) * tk, :]
        out = []
        for h in range(2):
            s = _dot_nt(qs[h], ks)
            out.append((s, jnp.max(s, axis=1, keepdims=True)))
        return out

    def consume(j, sc, state):
        vs = v_ref[0, j * tk:(j + 1) * tk, :]
        out = []
        for h in range(2):
            m, acc = state[h]
            s, bm = sc[h]
            mn = jnp.maximum(m, bm)
            p = jnp.exp2(s - mn).astype(BF16)
            acc = jnp.exp2(m - mn) * acc + _dot(p, vs)
            out.append((mn, acc))
        return out

    init = (jnp.full((tq, 1), -1e30, F32), jnp.zeros((tq, 2 * hd), F32))
    state = [init, init]
    sc = scores(0)
    for j in range(nblk):
        sc_next = scores(j + 1) if j + 1 < nblk else None
        state = consume(j, sc, state)
        sc = sc_next
    (_, a1), (_, a2) = state

    lp = lam_ref[...]
    lam = (jnp.exp(jnp.sum(lp[0:1] * lp[1:2], axis=1, keepdims=True))
           - jnp.exp(jnp.sum(lp[2:3] * lp[3:4], axis=1, keepdims=True)) + lam_init)
    o = a1[:, :hd] / a1[:, hd:] - lam * (a2[:, :hd] / a2[:, hd:])
    ms = jnp.mean(o * o, axis=-1, keepdims=True)
    y = o * lax.rsqrt(ms + NORM_EPS) * g_ref[...] * (1.0 - lam_init)
    o_ref[0] = y.astype(BF16)


def _attention(q3, k3, v3, lam_params, subln_g, layer, tq, tk):
    bsz, seq, _ = q3.shape
    lam_init = 0.8 - 0.6 * math.exp(-0.3 * layer)
    hd = 2 * DIFF_HEAD_DIM
    return pl.pallas_call(
        functools.partial(_attn_kernel, tk=tk, lam_init=lam_init),
        grid=(bsz, DIFF_HEADS, seq // tq),
        in_specs=[pl.BlockSpec((1, tq, hd), lambda b, h, i: (b, i, h)),
                  pl.BlockSpec((1, seq, hd), lambda b, h, i: (b, 0, h)),
                  pl.BlockSpec((1, seq, 2 * hd), lambda b, h, i: (b, 0, h)),
                  pl.BlockSpec((4, DIFF_HEAD_DIM), lambda b, h, i: (0, 0)),
                  pl.BlockSpec((1, hd), lambda b, h, i: (0, 0))],
        out_specs=pl.BlockSpec((1, tq, hd), lambda b, h, i: (b, i, h)),
        out_shape=jax.ShapeDtypeStruct((bsz, seq, DIFF_WIDTH), BF16),
        compiler_params=_params("parallel", "parallel", "parallel"),
        name="attn",
    )(q3, k3, v3, lam_params.astype(F32), subln_g.reshape(1, hd).astype(F32))


def _hgrn_kernel(qf_ref, ff_ref, if_ref, qb_ref, fb_ref, ib_ref, lbp_ref,
                 of_ref, ob_ref, stf_ref, stb_ref, *, layer):
    c_len = HG_CHUNK
    w = HG_WIDTH
    hd = HG_HEAD_DIM
    t_len = qf_ref.shape[1]
    nchunk = t_len // c_len

    @pl.when(pl.program_id(1) == 0)
    def _():
        stf_ref[...] = jnp.zeros_like(stf_ref)
        stb_ref[...] = jnp.zeros_like(stb_ref)

    lbp = lbp_ref[...]
    e = jnp.exp(lbp - jnp.max(lbp, axis=0, keepdims=True))
    sm = e / jnp.sum(e, axis=0, keepdims=True)
    lb = jnp.zeros((1, 2 * w), F32)
    for l in range(1, layer + 1):
        lb = lb + sm[l:l + 1]
    lb_f, lb_b = lb[:, :w], lb[:, w:]

    r = lax.broadcasted_iota(I32, (c_len, c_len), 0)
    c = lax.broadcasted_iota(I32, (c_len, c_len), 1)
    low = r >= c
    upp = r <= c
    low_m = jnp.where(low, 1.0, 0.0).astype(BF16)
    upp_m = jnp.where(upp, 1.0, 0.0).astype(BF16)
    lane = lax.broadcasted_iota(I32, (1, w), 1)
    heads = [(lane >= h * hd) & (lane < (h + 1) * hd) for h in range(HG_HEADS)]
    br = lax.broadcasted_iota(I32, (w, w), 0) // hd
    bc = lax.broadcasted_iota(I32, (w, w), 1) // hd
    same_head = br == bc
    mid = c_len // 2

    def bmm(a, b):
        return jnp.einsum('nik,nkj->nij', a, b, preferred_element_type=F32)

    def bmm_nt(a, b):
        return jnp.einsum('nik,njk->nij', a, b, preferred_element_type=F32)

    def scan_block(xq, xf, xi, lbd, tri_m, tri, last, order, st_ref):
        c3 = lambda a: a.reshape(nchunk, c_len, w)
        qh = c3(xq * _sigmoid(xq) * (hd ** -0.5))
        f = lbd + (1.0 - lbd) * _sigmoid(xf)
        kk = c3(1.0 - f)
        tri_b = jnp.broadcast_to(tri_m[None], (nchunk, c_len, c_len))
        g_hi, g_mid, g_lo = _split3(c3(jnp.log(f)))
        G = bmm(tri_b, g_hi) + bmm(tri_b, g_mid) + bmm(tri_b, g_lo)
        g_ref_row = G[:, mid:mid + 1, :]
        g_last = G[:, last:last + 1, :]
        qt = (qh * jnp.exp(G - g_ref_row)).astype(BF16)
        kt = (kk * jnp.exp(g_ref_row - G)).astype(BF16)
        q_in = (qh * jnp.exp(G)).astype(BF16)
        k_out = (kk * jnp.exp(g_last - G)).astype(BF16)
        vb = c3(xi).astype(BF16)
        zq = jnp.zeros_like(qt)
        o = jnp.zeros((nchunk, c_len, w), F32)
        for h in range(HG_HEADS):
            a = bmm_nt(jnp.where(heads[h][None], qt, zq), kt)
            a = jnp.where(tri[None], a, 0.0).astype(BF16)
            o = o + jnp.where(heads[h][None], bmm(a, vb), 0.0)
        upd = jnp.einsum('ncv,nck->nvk', vb, k_out, preferred_element_type=F32)
        upd = jnp.where(same_head[None], upd, 0.0)
        decay = jnp.exp(g_last)
        st = st_ref[...]
        inter = [None] * nchunk
        for ci in order:
            inter[ci] = _dot_nt(q_in[ci], st.astype(BF16))
            st = st * decay[ci] + upd[ci]
        st_ref[...] = st
        return o.reshape(nchunk * c_len, w) + jnp.concatenate(inter, axis=0)

    fwd_order = list(range(nchunk))
    of_ref[0] = scan_block(qf_ref[0], ff_ref[0], if_ref[0], lb_f, low_m, low, c_len - 1,
                           fwd_order, stf_ref)
    ob_ref[0] = scan_block(qb_ref[0], fb_ref[0], ib_ref[0], lb_b, upp_m, upp, 0,
                           fwd_order[::-1], stb_ref)


def _hgrn(hg3, hgrn_lb, layer, t_len):
    bsz, seq, _ = hg3.shape
    nt = seq // t_len
    w = HG_WIDTH
    fwd = lambda col: pl.BlockSpec((1, t_len, w), lambda b, j: (b, j, col))
    bwd = lambda col: pl.BlockSpec((1, t_len, w), lambda b, j: (b, nt - 1 - j, col))
    depth = hgrn_lb.shape[0]
    return pl.pallas_call(
        functools.partial(_hgrn_kernel, layer=layer),
        grid=(bsz, nt),
        in_specs=[fwd(0), fwd(1), fwd(3), bwd(0), bwd(2), bwd(3),
                  pl.BlockSpec((depth, 2 * w), lambda b, j: (0, 0))],
        out_specs=[pl.BlockSpec((1, t_len, w), lambda b, j: (b, j, 0)),
                   pl.BlockSpec((1, t_len, w), lambda b, j: (b, nt - 1 - j, 0))],
        out_shape=[jax.ShapeDtypeStruct((bsz, seq, w), F32),
                   jax.ShapeDtypeStruct((bsz, seq, w), F32)],
        scratch_shapes=[pltpu.VMEM((w, w), F32), pltpu.VMEM((w, w), F32)],
        compiler_params=_params("parallel", "arbitrary"),
        name="hgrn",
    )(hg3, hg3, hg3, hg3, hg3, hg3, hgrn_lb.astype(F32))


def _outproj_kernel(x_ref, bz_ref, zp_ref, zn_ref, yd_ref, of_ref, ob_ref, hgate_ref,
                    cw_ref, hn_ref, wo_ref, nf_ref, rw_ref,
                    x1_ref, h2_ref, aff_ref, affp_ref, *, blocks_per_seq):
    tm = x_ref.shape[0]
    cwid = CONV_WIDTH
    i = pl.program_id(0)
    pos = i % blocks_per_seq
    halo = zp_ref.shape[0]

    b = bz_ref[:, 0:cwid].astype(F32)
    z = bz_ref[:, cwid:2 * cwid].astype(F32)
    zprev = zp_ref[:, cwid:2 * cwid].astype(F32)[halo - 1:halo]
    znext = zn_ref[:, cwid:2 * cwid].astype(F32)[0:1]
    zprev = jnp.where(pos == 0, 0.0, zprev)
    znext = jnp.where(pos == blocks_per_seq - 1, 0.0, znext)
    row = lax.broadcasted_iota(I32, (tm, cwid), 0)
    z_up = jnp.where(row == 0, zprev, pltpu.roll(z, 1, axis=0))
    z_dn = jnp.where(row == tm - 1, znext, pltpu.roll(z, tm - 1, axis=0))
    cw = cw_ref[...]
    y_conv = b * (cw[0:1] * z_up + cw[1:2] * z + cw[2:3] * z_dn)

    o = of_ref[...] + ob_ref[...]
    w = HG_WIDTH
    br = lax.broadcasted_iota(I32, (w, w), 0) // HG_HEAD_DIM
    bc = lax.broadcasted_iota(I32, (w, w), 1) // HG_HEAD_DIM
    pool = jnp.where(br == bc, 1.0, 0.0).astype(BF16)
    sq = o * o
    sq_hi = sq.astype(BF16)
    sq_lo = (sq - sq_hi.astype(F32)).astype(BF16)
    ms = (_dot(sq_hi, pool) + _dot(sq_lo, pool)) * (1.0 / HG_HEAD_DIM)
    gt = hgate_ref[...]
    y_h = o * lax.rsqrt(ms + NORM_EPS) * hn_ref[...] * (gt * _sigmoid(gt))

    mixed = (_dot(y_conv.astype(BF16), wo_ref[0:cwid, :])
             + _dot(yd_ref[...], wo_ref[cwid:cwid + DIFF_WIDTH, :])
             + _dot(y_h.astype(BF16), wo_ref[cwid + DIFF_WIDTH:, :]))
    x1 = x_ref[...] + mixed
    x1_ref[...] = x1

    ms2 = jnp.mean(x1 * x1, axis=-1, keepdims=True)
    h2 = x1 * lax.rsqrt(ms2 + NORM_EPS) * nf_ref[...]
    h_hi = h2.astype(BF16)
    h2_ref[...] = h_hi

    h_lo = (h2 - h_hi.astype(F32)).astype(BF16)
    rw = rw_ref[...]
    d_hi = _dot(h_hi, rw)
    d_lo = _dot(h_lo, rw)
    ne = N_EXPERTS
    logits = d_hi[:, :ne] + (d_hi[:, ne:] + d_lo[:, :ne]) + d_lo[:, ne:]
    ex = jnp.exp(logits - jnp.max(logits, axis=-1, keepdims=True))
    aff = ex / jnp.sum(ex, axis=-1, keepdims=True)
    aff_ref[...] = aff
    a_hi = aff.astype(BF16)
    a_lo = (aff - a_hi.astype(F32)).astype(BF16)
    pad = jnp.zeros((aff.shape[0], LANES - 2 * ne), BF16)
    affp_ref[...] = jnp.concatenate([a_hi, a_lo, pad], axis=1)


def _outproj(x2, bz, ydiff, of2, ob2, hg2, conv_w, hgrn_norm, wo_bf16, norm_ffn, rw3, seq, tm):
    n = x2.shape[0]
    halo = 16
    hb = tm // halo
    nhalo = n // halo
    row = lambda w: pl.BlockSpec((tm, w), lambda i: (i, 0))
    full = lambda a, b: pl.BlockSpec((a, b), lambda i: (0, 0))
    hn = jnp.tile(hgrn_norm.astype(F32), HG_HEADS).reshape(1, HG_WIDTH)
    return pl.pallas_call(
        functools.partial(_outproj_kernel, blocks_per_seq=seq // tm),
        grid=(n // tm,),
        in_specs=[row(D_MODEL), row(2 * CONV_WIDTH),
                  pl.BlockSpec((halo, 2 * CONV_WIDTH), lambda i: (jnp.maximum(i * hb - 1, 0), 0)),
                  pl.BlockSpec((halo, 2 * CONV_WIDTH), lambda i: (jnp.minimum((i + 1) * hb, nhalo - 1), 0)),
                  row(DIFF_WIDTH), row(HG_WIDTH), row(HG_WIDTH),
                  pl.BlockSpec((tm, HG_WIDTH), lambda i: (i, 4)),
                  full(3, CONV_WIDTH), full(1, HG_WIDTH), full(D_MODEL, D_MODEL), full(1, D_MODEL),
                  full(D_MODEL, 2 * N_EXPERTS)],
        out_specs=[row(D_MODEL), row(D_MODEL), row(N_EXPERTS), row(LANES)],
        out_shape=[jax.ShapeDtypeStruct((n, D_MODEL), F32),
                   jax.ShapeDtypeStruct((n, D_MODEL), BF16),
                   jax.ShapeDtypeStruct((n, N_EXPERTS), F32),
                   jax.ShapeDtypeStruct((n, LANES), BF16)],
        compiler_params=_params("parallel"),
        name="outproj",
    )(x2, bz, bz, bz, ydiff, of2, ob2, hg2, conv_w.astype(F32), hn, wo_bf16,
      norm_ffn.reshape(1, D_MODEL).astype(F32), rw3)


def _select_kernel(aff_ref, rankm_ref, rankx_ref, pre_ref, tot_ref, *, cap):
    nblk = aff_ref.shape[0]
    ne = N_EXPERTS
    bits = pltpu.bitcast(aff_ref[...], I32)

    def count(mask):
        part = jnp.sum(jnp.where(mask, 1.0, 0.0), axis=0)
        return jnp.sum(part, axis=1, keepdims=True)

    def search(i, tau):
        cand = tau | jnp.left_shift(jnp.int32(1), 30 - i)
        return jnp.where(count(bits >= cand[None]) >= cap, cand, tau)

    tau = lax.fori_loop(0, 31, search, jnp.zeros((ne, 1), I32))[None]

    r = lax.broadcasted_iota(I32, (LANES, LANES), 0)
    c = lax.broadcasted_iota(I32, (LANES, LANES), 1)
    incl = jnp.where(r <= c, 1.0, 0.0).astype(BF16)
    ones = jnp.ones((LANES, LANES), BF16)

    def prefix(mask):
        m = jnp.where(mask, 1.0, 0.0)
        m2 = m.reshape(nblk * ne, LANES).astype(BF16)
        pre_ref[...] = _dot(m2, incl).reshape(nblk, ne, LANES) - m
        tot_ref[...] = _dot(m2, ones).reshape(nblk, ne, LANES)

        def carry(b, run):
            pre_ref[b] = pre_ref[b] + run
            return run + tot_ref[b]

        lax.fori_loop(0, nblk, carry, jnp.zeros((ne, LANES), F32))
        return pre_ref[...]

    above = bits > tau
    tie = bits == tau
    need = cap - count(above)
    sel = above | (tie & (prefix(tie) < need[None]))
    rank = prefix(sel).astype(I32)
    rankx_ref[...] = rank
    rankm_ref[...] = jnp.where(sel, rank, -1)


def _select(aff3, cap):
    nblk = aff3.shape[0]
    shp = (nblk, N_EXPERTS, LANES)
    spec = pl.BlockSpec(shp, lambda i: (0, 0, 0))
    return pl.pallas_call(
        functools.partial(_select_kernel, cap=cap),
        grid=(1,),
        in_specs=[spec],
        out_specs=[spec, spec],
        out_shape=[jax.ShapeDtypeStruct(shp, I32), jax.ShapeDtypeStruct(shp, I32)],
        scratch_shapes=[pltpu.VMEM(shp, F32), pltpu.VMEM(shp, F32)],
        compiler_params=_params("arbitrary"),
        name="select",
    )(aff3)


def _expert_kernel(base_ref, rank_ref, h_ref, g_ref, wg_ref, wu_ref, wd_ref, ye_ref, xe_ref, xg_ref,
                   *, nt, sub, tt, sj, cap, win):
    e = pl.program_id(0)
    s = pl.program_id(1)

    @pl.when(s == 0)
    def _():
        xe_ref[...] = jnp.zeros_like(xe_ref)
        xg_ref[...] = jnp.zeros_like(xg_ref)

    @pl.when(s < nt)
    def _():
        slot = lax.broadcasted_iota(I32, (win, tt), 0)
        for sub_i in range(sub):
            tile = e * (nt * sub + 1) + s * sub + sub_i
            base = base_ref[tile]
            count = base_ref[tile + 1] - base
            wb = (base // 8) * 8
            nwin = jnp.where(count > 0, (base - wb + count + win - 1) // win, 0)
            rk = rank_ref[0, sub_i]
            rel = jnp.concatenate([rk[a:a + 1, :] for a in range(tt // LANES)], axis=1) - wb
            rows = slice(sub_i * tt, (sub_i + 1) * tt)

            def fill(w, carry, wb=wb, rel=rel, rows=rows):
                onehot = jnp.where(slot == rel - w * win, 1.0, 0.0).astype(BF16)
                r0 = pl.multiple_of(wb + w * win, 8)
                xe_ref[pl.ds(r0, win), :] += _dot(onehot, h_ref[rows, :])
                xg_ref[pl.ds(r0, win), :] += _dot(onehot, g_ref[rows, :])
                return carry

            lax.fori_loop(0, nwin, fill, 0)

    @pl.when(s >= nt)
    def _():
        j0 = pl.multiple_of((s - nt) * sj, sj)
        xj = xe_ref[pl.ds(j0, sj), :].astype(BF16)
        a = _dot(xj, wg_ref[0])
        u = _dot(xj, wu_ref[0])
        hid = (a * _sigmoid(a) * u).astype(BF16)
        lane = lax.broadcasted_iota(I32, (sj, LANES), 1)
        mine = (lane == e) | (lane == N_EXPERTS + e)
        gate = jnp.sum(jnp.where(mine, xg_ref[pl.ds(j0, sj), :], 0.0), axis=1, keepdims=True)
        ye_ref[0] = (_dot(hid, wd_ref[0]) * gate).astype(BF16)


def _experts(base_flat, rank4, h2, affp, wg, wu, wd, layer, cap, tt, sub, sj):
    n = h2.shape[0]
    nt = n // (tt * sub)
    nj = cap // sj
    win = GATHER_ROWS
    wspec = pl.BlockSpec((pl.Squeezed(), 1, D_MODEL, D_MODEL), lambda e, s, b: (layer, e, 0, 0))
    grid_spec = pltpu.PrefetchScalarGridSpec(
        num_scalar_prefetch=1,
        grid=(N_EXPERTS, nt + nj),
        in_specs=[pl.BlockSpec((1, sub, tt // LANES, LANES),
                               lambda e, s, b: (e, jnp.minimum(s, nt - 1), 0, 0)),
                  pl.BlockSpec((sub * tt, D_MODEL), lambda e, s, b: (jnp.minimum(s, nt - 1), 0)),
                  pl.BlockSpec((sub * tt, LANES), lambda e, s, b: (jnp.minimum(s, nt - 1), 0)),
                  wspec, wspec, wspec],
        out_specs=pl.BlockSpec((1, sj, D_MODEL), lambda e, s, b: (e, jnp.maximum(s - nt, 0), 0)),
        scratch_shapes=[pltpu.VMEM((cap + win + 8, D_MODEL), F32),
                        pltpu.VMEM((cap + win + 8, LANES), F32)],
    )
    return pl.pallas_call(
        functools.partial(_expert_kernel, nt=nt, sub=sub, tt=tt, sj=sj, cap=cap, win=win),
        grid_spec=grid_spec,
        out_shape=jax.ShapeDtypeStruct((N_EXPERTS, cap, D_MODEL), BF16),
        compiler_params=_params("arbitrary", "arbitrary"),
        name="experts",
    )(base_flat, rank4, h2, affp, wg, wu, wd)


def _main_start(base, cap, w1):
    return pl.multiple_of(jnp.minimum((base // 16) * 16, cap - w1), 16)


def _over_start(base, nxt, cap, w1, w2):
    wb = _main_start(base, cap, w1)
    need = nxt - wb > w1
    return need, pl.multiple_of(jnp.where(need, jnp.minimum(wb + w1, cap - w2), 0), 16)


def _combine_kernel(base_ref, x_ref, rank_ref, nrm_ref, *rest, nt, cap, w1, w2, final):
    ne = N_EXPERTS
    main_refs = rest[:ne]
    over_refs = rest[ne:2 * ne]
    out_ref = rest[2 * ne]
    t = pl.program_id(0)
    tt = x_ref.shape[0]
    rank = rank_ref[...]
    bases = [base_ref[e * (nt + 1) + t] for e in range(ne)]
    nexts = [base_ref[e * (nt + 1) + t + 1] for e in range(ne)]

    col = lax.broadcasted_iota(I32, (1, ne), 1)
    wb_row = jnp.zeros((1, ne), I32)
    for e in range(ne):
        wb_row = jnp.where(col == e, _main_start(bases[e], cap, w1), wb_row)
    rel = rank - wb_row
    adj = jnp.where((rank >= 0) & (rel < w1), rel, -1)
    per_vreg = LANES // w1
    lane = lax.broadcasted_iota(I32, (tt, LANES), 1)
    onehots = []
    for i in range(ne // per_vreg):
        target = jnp.full((tt, LANES), -1, I32)
        for a in range(per_vreg):
            e = i * per_vreg + a
            inside = (lane >= a * w1) & (lane < (a + 1) * w1)
            target = jnp.where(inside, adj[:, e:e + 1] + a * w1, target)
        onehots.append(jnp.where(lane == target, 1.0, 0.0).astype(BF16))
    y_all = jnp.concatenate([r[...] for r in main_refs], axis=0)
    out_ref[...] = x_ref[...] + _dot(jnp.concatenate(onehots, axis=1), y_all)

    for e in range(ne):
        need, wo = _over_start(bases[e], nexts[e], cap, w1, w2)

        @pl.when(need)
        def _(e=e, wo=wo):
            lane2 = lax.broadcasted_iota(I32, (tt, w2), 1)
            first = _main_start(bases[e], cap, w1) + w1
            r = rank_ref[:, e:e + 1]
            match = (lane2 == r - wo) & (r >= first)
            onehot = jnp.where(match, 1.0, 0.0).astype(BF16)
            out_ref[...] += _dot(onehot, over_refs[e][...])

    if final:
        o = out_ref[...]
        ms = jnp.mean(o * o, axis=-1, keepdims=True)
        out_ref[...] = o * lax.rsqrt(ms + NORM_EPS) * nrm_ref[...]


def _combine(base_flat, x1, rank_tok, ye, nrm, cap, tt, final):
    n = x1.shape[0]
    nt = n // tt
    w1 = COMBINE_MAIN_ROWS
    w2 = tt + 16 - w1
    assert cap >= max(w1, w2) and cap % 16 == 0 and LANES % w1 == 0
    row = lambda w: pl.BlockSpec((tt, w), lambda t, b: (t, 0))

    def main_spec(e):
        return pl.BlockSpec(
            (pl.Squeezed(), pl.Element(w1), pl.Element(D_MODEL)),
            lambda t, b: (e, _main_start(b[e * (nt + 1) + t], cap, w1), 0))

    def over_spec(e):
        return pl.BlockSpec(
            (pl.Squeezed(), pl.Element(w2), pl.Element(D_MODEL)),
            lambda t, b: (e, _over_start(b[e * (nt + 1) + t], b[e * (nt + 1) + t + 1], cap, w1, w2)[1], 0))

    grid_spec = pltpu.PrefetchScalarGridSpec(
        num_scalar_prefetch=1,
        grid=(nt,),
        in_specs=[row(D_MODEL), row(N_EXPERTS),
                  pl.BlockSpec((1, D_MODEL), lambda t, b: (0, 0))]
                 + [main_spec(e) for e in range(N_EXPERTS)]
                 + [over_spec(e) for e in range(N_EXPERTS)],
        out_specs=row(D_MODEL),
    )
    return pl.pallas_call(
        functools.partial(_combine_kernel, nt=nt, cap=cap, w1=w1, w2=w2, final=final),
        grid_spec=grid_spec,
        out_shape=jax.ShapeDtypeStruct((n, D_MODEL), F32),
        compiler_params=_params("arbitrary"),
        name="combine",
    )(base_flat, x1, rank_tok, nrm.reshape(1, D_MODEL).astype(F32), *([ye] * (2 * N_EXPERTS)))


def _tile(n, want):
    t = min(n, want)
    assert n % t == 0, (n, want)
    return t


def _routed_ffn(x1, h2, aff, affp, wg, wu, wd, layer, nrm, final):
    n = x1.shape[0]
    cap = (EC_CAPACITY_FACTOR * n) // N_EXPERTS
    nblk = n // LANES
    tt_e = _tile(n, 1024)
    sj = _tile(cap, 512)
    tt_c = _tile(n, 256)

    aff3 = aff.reshape(nblk, LANES, N_EXPERTS).transpose(0, 2, 1)
    rankm3, rankx3 = _select(aff3, cap)
    rank_tok = rankm3.transpose(0, 2, 1).reshape(n, N_EXPERTS)
    rank4 = rankm3.transpose(1, 0, 2).reshape(N_EXPERTS, n // tt_e, tt_e // LANES, LANES)
    base128 = rankx3[:, :, 0].T
    last = jnp.full((N_EXPERTS, 1), cap, I32)
    base_e = jnp.concatenate([base128[:, ::tt_e // LANES], last], axis=1).reshape(-1)
    base_c = jnp.concatenate([base128[:, ::tt_c // LANES], last], axis=1).reshape(-1)

    sub = 2 if (n // tt_e) % 2 == 0 else 1
    ye = _experts(base_e, rank4, h2, affp, wg, wu, wd, layer, cap, tt_e, sub, sj)
    return _combine(base_c, x1, rank_tok, ye, nrm, cap, tt_c, final)


def _trunk(x, p):
    bsz, seq, _ = x.shape
    n = bsz * seq
    depth = p["w_in"].shape[0]
    tm = _tile(seq, 512)
    tq = _tile(seq, 512)
    tk = _tile(seq, 2048)
    t_hg = _tile(seq, 512)
    tables = _rope_tables(seq)

    x2 = x.reshape(n, D_MODEL)
    for l in range(depth):
        bz, q, k, v, hg = _inproj(x2, p["norm_mix"][l], p["w_in"][l], tables, seq, tm)
        to3 = lambda a: a.reshape(bsz, seq, a.shape[-1])
        ydiff = _attention(to3(q), to3(k), to3(v), p["diff_lambda"][l], p["diff_subln"][l], l, tq, tk)
        o_f, o_b = _hgrn(to3(hg), p["hgrn_lb"], l, t_hg)
        x1, h2, aff, affp = _outproj(x2, bz, ydiff.reshape(n, DIFF_WIDTH), o_f.reshape(n, HG_WIDTH),
                               o_b.reshape(n, HG_WIDTH), hg, p["conv_w"][l], p["hgrn_norm"][l],
                               p["w_out"][l], p["norm_ffn"][l], p["router_w"][l], seq, tm)
        x2 = _routed_ffn(x1, h2, aff, affp, p["w_gate"], p["w_up"], p["w_down"], l,
                         p["norm_final"], l == depth - 1)
    return x2.reshape(bsz, seq, D_MODEL)


def _split2_param(w):
    w = w.astype(F32)
    hi = w.astype(BF16)
    lo = (w - hi.astype(F32)).astype(BF16)
    return jnp.concatenate([hi, lo], axis=-1)


def kernel(x_prompt, x_sample, norm_mix, w_in, conv_w, diff_lambda, diff_subln, hgrn_lb,
           hgrn_norm, w_out, norm_ffn, router_w, w_gate, w_up, w_down, norm_final):
    p = dict(
        norm_mix=norm_mix.astype(F32), w_in=w_in.astype(BF16), conv_w=conv_w,
        diff_lambda=diff_lambda, diff_subln=diff_subln, hgrn_lb=hgrn_lb, hgrn_norm=hgrn_norm,
        w_out=w_out.astype(BF16), norm_ffn=norm_ffn,
        router_w=_split2_param(router_w),
        w_gate=w_gate.astype(BF16), w_up=w_up.astype(BF16), w_down=w_down.astype(BF16),
        norm_final=norm_final,
    )
    return _trunk(x_prompt, p), _trunk(x_sample, p)
```

```python
import functools
import math

import jax
import jax.numpy as jnp
from jax import lax
from jax.experimental import pallas as pl
from jax.experimental.pallas import tpu as pltpu

F32 = jnp.float32
BF16 = jnp.bfloat16
I32 = jnp.int32

D_MODEL = 1024
CONV_WIDTH = 256
DIFF_WIDTH = 512
DIFF_HEADS = 4
DIFF_HEAD_DIM = 64
HG_WIDTH = 256
HG_HEADS = 4
HG_HEAD_DIM = 64
HG_CHUNK = 64
D_PROJ = 3584
ROPE_THETA = 10000.0
N_EXPERTS = 16
EC_CAPACITY_FACTOR = 2
NORM_EPS = 1e-6

VMEM_LIMIT_BYTES = 56 * 1024 * 1024
LANES = 128
ROW_ALIGN = 16
GATHER_ROWS = 256
COMBINE_MAIN_ROWS = 64

NT_DIMS = (((1,), (1,)), ((), ()))
Q_SCALE = DIFF_HEAD_DIM ** -0.5 * math.log2(math.e)


def _params(*sem):
    return pltpu.CompilerParams(dimension_semantics=sem, vmem_limit_bytes=VMEM_LIMIT_BYTES)


def _split3(x):
    hi = x.astype(BF16)
    r1 = x - hi.astype(F32)
    mid = r1.astype(BF16)
    lo = (r1 - mid.astype(F32)).astype(BF16)
    return hi, mid, lo


def _dot(a, b):
    return jnp.dot(a, b, preferred_element_type=F32)


def _dot_nt(a, b):
    return lax.dot_general(a, b, NT_DIMS, preferred_element_type=F32)


def _dot_exact_lhs(a_bf16, x_f32):
    hi, mid, lo = _split3(x_f32)
    return _dot(a_bf16, hi) + _dot(a_bf16, mid) + _dot(a_bf16, lo)


def _sigmoid(x):
    return 1.0 / (1.0 + jnp.exp(-x))


def _inproj_kernel(x_ref, g_ref, w_ref, cos_ref, sina_ref, sinb_ref,
                   bz_ref, q_ref, k_ref, v_ref, hg_ref):
    x = x_ref[...]
    ms = jnp.mean(x * x, axis=-1, keepdims=True)
    h = (x * lax.rsqrt(ms + NORM_EPS) * g_ref[...]).astype(BF16)

    def proj(a, b):
        return _dot(h, w_ref[:, a:b])

    cw = CONV_WIDTH
    bz_ref[:, 0:cw] = proj(0, cw).astype(BF16)
    bz_ref[:, cw:2 * cw] = (proj(cw, 2 * cw) * proj(2 * cw, 3 * cw)).astype(BF16)

    reps = DIFF_WIDTH // LANES
    cos = jnp.concatenate([cos_ref[...]] * reps, axis=1)
    sina = jnp.concatenate([sina_ref[...]] * reps, axis=1)
    sinb = jnp.concatenate([sinb_ref[...]] * reps, axis=1)
    half = DIFF_HEAD_DIM // 2

    def rope(t):
        return (t * cos + pltpu.roll(t, DIFF_WIDTH - half, axis=1) * sina
                + pltpu.roll(t, half, axis=1) * sinb)

    o = 3 * cw
    q_ref[...] = (rope(proj(o, o + DIFF_WIDTH)) * Q_SCALE).astype(BF16)
    o += DIFF_WIDTH
    k_ref[...] = rope(proj(o, o + DIFF_WIDTH)).astype(BF16)
    o += DIFF_WIDTH
    v = proj(o, o + DIFF_WIDTH).astype(BF16)
    hd = 2 * DIFF_HEAD_DIM
    ones = jnp.ones((v.shape[0], hd), BF16)
    for head in range(DIFF_HEADS):
        v_ref[:, 2 * head * hd:(2 * head + 1) * hd] = v[:, head * hd:(head + 1) * hd]
        v_ref[:, (2 * head + 1) * hd:(2 * head + 2) * hd] = ones
    o += DIFF_WIDTH
    hg_ref[...] = proj(o, D_PROJ)


def _rope_tables(seq):
    d = DIFF_HEAD_DIM
    inv = 1.0 / (ROPE_THETA ** (jnp.arange(0, d, 2, dtype=F32) / d))
    ang = jnp.arange(seq, dtype=F32)[:, None] * inv[None, :]
    ang = jnp.concatenate([ang, ang, ang, ang], axis=-1)
    cos, sin = jnp.cos(ang), jnp.sin(ang)
    first = (jnp.arange(LANES) % d) < (d // 2)
    sina = jnp.where(first[None, :], -sin, 0.0)
    sinb = jnp.where(first[None, :], 0.0, sin)
    return cos, sina, sinb


def _inproj(x2, g, w_bf16, tables, seq, tm):
    n = x2.shape[0]
    nblk_s = seq // tm
    cos, sina, sinb = tables
    tab_spec = pl.BlockSpec((tm, LANES), lambda i: (i % nblk_s, 0))
    row = lambda w: pl.BlockSpec((tm, w), lambda i: (i, 0))
    hgw = 5 * HG_WIDTH
    return pl.pallas_call(
        _inproj_kernel,
        grid=(n // tm,),
        in_specs=[row(D_MODEL),
                  pl.BlockSpec((1, D_MODEL), lambda i: (0, 0)),
                  pl.BlockSpec((D_MODEL, D_PROJ), lambda i: (0, 0)),
                  tab_spec, tab_spec, tab_spec],
        out_specs=[row(2 * CONV_WIDTH), row(DIFF_WIDTH), row(DIFF_WIDTH), row(2 * DIFF_WIDTH), row(hgw)],
        out_shape=[jax.ShapeDtypeStruct((n, 2 * CONV_WIDTH), BF16),
                   jax.ShapeDtypeStruct((n, DIFF_WIDTH), BF16),
                   jax.ShapeDtypeStruct((n, DIFF_WIDTH), BF16),
                   jax.ShapeDtypeStruct((n, 2 * DIFF_WIDTH), BF16),
                   jax.ShapeDtypeStruct((n, hgw), F32)],
        compiler_params=_params("parallel"),
        name="inproj",
    )(x2, g.reshape(1, D_MODEL), w_bf16, cos, sina, sinb)


def _attn_kernel(q_ref, k_ref, v_ref, lam_ref, g_ref, o_ref, *, tk, lam_init):
    q = q_ref[0]
    tq = q.shape[0]
    seq = k_ref.shape[1]
    nblk = seq // tk
    d = DIFF_HEAD_DIM
    hd = 2 * d
    lane = lax.broadcasted_iota(I32, q.shape, 1)
    zero = jnp.zeros_like(q)
    qs = (jnp.where(lane < d, q, zero), jnp.where(lane >= d, q, zero))

    def scores(j, h):
        s = _dot_nt(qs[h], k_ref[0, j * tk:(j + 1) * tk, :])
        return s, jnp.max(s, axis=1, keepdims=True)

    def consume(j, sc, st):
        m, acc = st
        s, bm = sc
        mn = jnp.maximum(m, bm)
        p = jnp.exp2(s - mn).astype(BF16)
        return mn, jnp.exp2(m - mn) * acc + _dot(p, v_ref[0, j * tk:(j + 1) * tk, :])

    init = (jnp.full((tq, 1), -1e30, F32), jnp.zeros((tq, 2 * hd), F32))
    state = [init, init]
    sc = [scores(0, 0), scores(0, 1)]
    for j in range(nblk):
        nxt = [None, None]
        for h in range(2):
            if j + 1 < nblk:
                nxt[h] = scores(j + 1, h)
            state[h] = consume(j, sc[h], state[h])
        sc = nxt
    (_, a1), (_, a2) = state

    lp = lam_ref[...]
    lam = (jnp.exp(jnp.sum(lp[0:1] * lp[1:2], axis=1, keepdims=True))
           - jnp.exp(jnp.sum(lp[2:3] * lp[3:4], axis=1, keepdims=True)) + lam_init)
    o = a1[:, :hd] / a1[:, hd:] - lam * (a2[:, :hd] / a2[:, hd:])
    ms = jnp.mean(o * o, axis=-1, keepdims=True)
    y = o * lax.rsqrt(ms + NORM_EPS) * g_ref[...] * (1.0 - lam_init)
    o_ref[0] = y.astype(BF16)


def _attention(q3, k3, v3, lam_params, subln_g, layer, tq, tk):
    bsz, seq, _ = q3.shape
    lam_init = 0.8 - 0.6 * math.exp(-0.3 * layer)
    hd = 2 * DIFF_HEAD_DIM
    return pl.pallas_call(
        functools.partial(_attn_kernel, tk=tk, lam_init=lam_init),
        grid=(bsz, DIFF_HEADS, seq // tq),
        in_specs=[pl.BlockSpec((1, tq, hd), lambda b, h, i: (b, i, h)),
                  pl.BlockSpec((1, seq, hd), lambda b, h, i: (b, 0, h)),
                  pl.BlockSpec((1, seq, 2 * hd), lambda b, h, i: (b, 0, h)),
                  pl.BlockSpec((4, DIFF_HEAD_DIM), lambda b, h, i: (0, 0)),
                  pl.BlockSpec((1, hd), lambda b, h, i: (0, 0))],
        out_specs=pl.BlockSpec((1, tq, hd), lambda b, h, i: (b, i, h)),
        out_shape=jax.ShapeDtypeStruct((bsz, seq, DIFF_WIDTH), BF16),
        compiler_params=_params("parallel", "parallel", "parallel"),
        name="attn",
    )(q3, k3, v3, lam_params.astype(F32), subln_g.reshape(1, hd).astype(F32))


def _hgrn_kernel(qf_ref, ff_ref, if_ref, qb_ref, fb_ref, ib_ref, lbp_ref,
                 of_ref, ob_ref, stf_ref, stb_ref, *, layer):
    c_len = HG_CHUNK
    w = HG_WIDTH
    hd = HG_HEAD_DIM
    t_len = qf_ref.shape[1]
    nchunk = t_len // c_len

    @pl.when(pl.program_id(1) == 0)
    def _():
        stf_ref[...] = jnp.zeros_like(stf_ref)
        stb_ref[...] = jnp.zeros_like(stb_ref)

    lbp = lbp_ref[...]
    e = jnp.exp(lbp - jnp.max(lbp, axis=0, keepdims=True))
    sm = e / jnp.sum(e, axis=0, keepdims=True)
    lb = jnp.zeros((1, 2 * w), F32)
    for l in range(1, layer + 1):
        lb = lb + sm[l:l + 1]
    lb_f, lb_b = lb[:, :w], lb[:, w:]

    r = lax.broadcasted_iota(I32, (c_len, c_len), 0)
    c = lax.broadcasted_iota(I32, (c_len, c_len), 1)
    low = r >= c
    upp = r <= c
    low_m = jnp.where(low, 1.0, 0.0).astype(BF16)
    upp_m = jnp.where(upp, 1.0, 0.0).astype(BF16)
    lane = lax.broadcasted_iota(I32, (1, w), 1)
    heads = [(lane >= h * hd) & (lane < (h + 1) * hd) for h in range(HG_HEADS)]
    br = lax.broadcasted_iota(I32, (w, w), 0) // hd
    bc = lax.broadcasted_iota(I32, (w, w), 1) // hd
    same_head = br == bc
    mid = c_len // 2

    def bmm(a, b):
        return jnp.einsum('nik,nkj->nij', a, b, preferred_element_type=F32)

    def bmm_nt(a, b):
        return jnp.einsum('nik,njk->nij', a, b, preferred_element_type=F32)

    def scan_block(xq, xf, xi, lbd, tri_m, tri, last, order, st_ref):
        c3 = lambda a: a.reshape(nchunk, c_len, w)
        qh = c3(xq * _sigmoid(xq) * (hd ** -0.5))
        f = lbd + (1.0 - lbd) * _sigmoid(xf)
        kk = c3(1.0 - f)
        tri_b = jnp.broadcast_to(tri_m[None], (nchunk, c_len, c_len))
        g_hi, g_mid, g_lo = _split3(c3(jnp.log(f)))
        G = bmm(tri_b, g_hi) + bmm(tri_b, g_mid) + bmm(tri_b, g_lo)
        g_ref_row = G[:, mid:mid + 1, :]
        g_last = G[:, last:last + 1, :]
        qt = (qh * jnp.exp(G - g_ref_row)).astype(BF16)
        kt = (kk * jnp.exp(g_ref_row - G)).astype(BF16)
        q_in = (qh * jnp.exp(G)).astype(BF16)
        k_out = (kk * jnp.exp(g_last - G)).astype(BF16)
        vb = c3(xi).astype(BF16)
        zq = jnp.zeros_like(qt)
        o = jnp.zeros((nchunk, c_len, w), F32)
        for h in range(HG_HEADS):
            a = bmm_nt(jnp.where(heads[h][None], qt, zq), kt)
            a = jnp.where(tri[None], a, 0.0).astype(BF16)
            o = o + jnp.where(heads[h][None], bmm(a, vb), 0.0)
        upd = jnp.einsum('ncv,nck->nvk', vb, k_out, preferred_element_type=F32)
        upd = jnp.where(same_head[None], upd, 0.0)
        decay = jnp.exp(g_last)
        st = st_ref[...]
        inter = [None] * nchunk
        for ci in order:
            inter[ci] = _dot_nt(q_in[ci], st.astype(BF16))
            st = st * decay[ci] + upd[ci]
        st_ref[...] = st
        return o.reshape(nchunk * c_len, w) + jnp.concatenate(inter, axis=0)

    fwd_order = list(range(nchunk))
    of_ref[0] = scan_block(qf_ref[0], ff_ref[0], if_ref[0], lb_f, low_m, low, c_len - 1,
                           fwd_order, stf_ref)
    ob_ref[0] = scan_block(qb_ref[0], fb_ref[0], ib_ref[0], lb_b, upp_m, upp, 0,
                           fwd_order[::-1], stb_ref)


def _hgrn(hg3, hgrn_lb, layer, t_len):
    bsz, seq, _ = hg3.shape
    nt = seq // t_len
    w = HG_WIDTH
    fwd = lambda col: pl.BlockSpec((1, t_len, w), lambda b, j: (b, j, col))
    bwd = lambda col: pl.BlockSpec((1, t_len, w), lambda b, j: (b, nt - 1 - j, col))
    depth = hgrn_lb.shape[0]
    return pl.pallas_call(
        functools.partial(_hgrn_kernel, layer=layer),
        grid=(bsz, nt),
        in_specs=[fwd(0), fwd(1), fwd(3), bwd(0), bwd(2), bwd(3),
                  pl.BlockSpec((depth, 2 * w), lambda b, j: (0, 0))],
        out_specs=[pl.BlockSpec((1, t_len, w), lambda b, j: (b, j, 0)),
                   pl.BlockSpec((1, t_len, w), lambda b, j: (b, nt - 1 - j, 0))],
        out_shape=[jax.ShapeDtypeStruct((bsz, seq, w), F32),
                   jax.ShapeDtypeStruct((bsz, seq, w), F32)],
        scratch_shapes=[pltpu.VMEM((w, w), F32), pltpu.VMEM((w, w), F32)],
        compiler_params=_params("parallel", "arbitrary"),
        name="hgrn",
    )(hg3, hg3, hg3, hg3, hg3, hg3, hgrn_lb.astype(F32))


def _outproj_kernel(x_ref, bz_ref, zp_ref, zn_ref, yd_ref, of_ref, ob_ref, hgate_ref,
                    cw_ref, hn_ref, wo_ref, nf_ref, rw_ref,
                    x1_ref, h2_ref, aff_ref, affp_ref, *, blocks_per_seq):
    tm = x_ref.shape[0]
    cwid = CONV_WIDTH
    i = pl.program_id(0)
    pos = i % blocks_per_seq
    halo = zp_ref.shape[0]

    b = bz_ref[:, 0:cwid].astype(F32)
    z = bz_ref[:, cwid:2 * cwid].astype(F32)
    zprev = zp_ref[:, cwid:2 * cwid].astype(F32)[halo - 1:halo]
    znext = zn_ref[:, cwid:2 * cwid].astype(F32)[0:1]
    zprev = jnp.where(pos == 0, 0.0, zprev)
    znext = jnp.where(pos == blocks_per_seq - 1, 0.0, znext)
    row = lax.broadcasted_iota(I32, (tm, cwid), 0)
    z_up = jnp.where(row == 0, zprev, pltpu.roll(z, 1, axis=0))
    z_dn = jnp.where(row == tm - 1, znext, pltpu.roll(z, tm - 1, axis=0))
    cw = cw_ref[...]
    y_conv = b * (cw[0:1] * z_up + cw[1:2] * z + cw[2:3] * z_dn)

    o = of_ref[...] + ob_ref[...]
    w = HG_WIDTH
    br = lax.broadcasted_iota(I32, (w, w), 0) // HG_HEAD_DIM
    bc = lax.broadcasted_iota(I32, (w, w), 1) // HG_HEAD_DIM
    pool = jnp.where(br == bc, 1.0, 0.0).astype(BF16)
    sq = o * o
    sq_hi = sq.astype(BF16)
    sq_lo = (sq - sq_hi.astype(F32)).astype(BF16)
    ms = (_dot(sq_hi, pool) + _dot(sq_lo, pool)) * (1.0 / HG_HEAD_DIM)
    gt = hgate_ref[...]
    y_h = o * lax.rsqrt(ms + NORM_EPS) * hn_ref[...] * (gt * _sigmoid(gt))

    mixed = (_dot(y_conv.astype(BF16), wo_ref[0:cwid, :])
             + _dot(yd_ref[...], wo_ref[cwid:cwid + DIFF_WIDTH, :])
             + _dot(y_h.astype(BF16), wo_ref[cwid + DIFF_WIDTH:, :]))
    x1 = x_ref[...] + mixed
    x1_ref[...] = x1

    ms2 = jnp.mean(x1 * x1, axis=-1, keepdims=True)
    h2 = x1 * lax.rsqrt(ms2 + NORM_EPS) * nf_ref[...]
    h_hi = h2.astype(BF16)
    h2_ref[...] = h_hi

    h_lo = (h2 - h_hi.astype(F32)).astype(BF16)
    rw = rw_ref[...]
    d_hi = _dot(h_hi, rw)
    d_lo = _dot(h_lo, rw)
    ne = N_EXPERTS
    logits = d_hi[:, :ne] + (d_hi[:, ne:] + d_lo[:, :ne]) + d_lo[:, ne:]
    ex = jnp.exp(logits - jnp.max(logits, axis=-1, keepdims=True))
    aff = ex / jnp.sum(ex, axis=-1, keepdims=True)
    aff_ref[...] = aff
    a_hi = aff.astype(BF16)
    a_lo = (aff - a_hi.astype(F32)).astype(BF16)
    pad = jnp.zeros((aff.shape[0], LANES - 2 * ne), BF16)
    affp_ref[...] = jnp.concatenate([a_hi, a_lo, pad], axis=1)


def _outproj(x2, bz, ydiff, of2, ob2, hg2, conv_w, hgrn_norm, wo_bf16, norm_ffn, rw3, seq, tm):
    n = x2.shape[0]
    halo = 16
    hb = tm // halo
    nhalo = n // halo
    row = lambda w: pl.BlockSpec((tm, w), lambda i: (i, 0))
    full = lambda a, b: pl.BlockSpec((a, b), lambda i: (0, 0))
    hn = jnp.tile(hgrn_norm.astype(F32), HG_HEADS).reshape(1, HG_WIDTH)
    return pl.pallas_call(
        functools.partial(_outproj_kernel, blocks_per_seq=seq // tm),
        grid=(n // tm,),
        in_specs=[row(D_MODEL), row(2 * CONV_WIDTH),
                  pl.BlockSpec((halo, 2 * CONV_WIDTH), lambda i: (jnp.maximum(i * hb - 1, 0), 0)),
                  pl.BlockSpec((halo, 2 * CONV_WIDTH), lambda i: (jnp.minimum((i + 1) * hb, nhalo - 1), 0)),
                  row(DIFF_WIDTH), row(HG_WIDTH), row(HG_WIDTH),
                  pl.BlockSpec((tm, HG_WIDTH), lambda i: (i, 4)),
                  full(3, CONV_WIDTH), full(1, HG_WIDTH), full(D_MODEL, D_MODEL), full(1, D_MODEL),
                  full(D_MODEL, 2 * N_EXPERTS)],
        out_specs=[row(D_MODEL), row(D_MODEL), row(N_EXPERTS), row(LANES)],
        out_shape=[jax.ShapeDtypeStruct((n, D_MODEL), F32),
                   jax.ShapeDtypeStruct((n, D_MODEL), BF16),
                   jax.ShapeDtypeStruct((n, N_EXPERTS), F32),
                   jax.ShapeDtypeStruct((n, LANES), BF16)],
        compiler_params=_params("parallel"),
        name="outproj",
    )(x2, bz, bz, bz, ydiff, of2, ob2, hg2, conv_w.astype(F32), hn, wo_bf16,
      norm_ffn.reshape(1, D_MODEL).astype(F32), rw3)


def _select_kernel(aff_ref, rankm_ref, rankx_ref, pre_ref, tot_ref, *, cap):
    nblk = aff_ref.shape[0]
    ne = N_EXPERTS
    bits = pltpu.bitcast(aff_ref[...], I32)

    def count(mask):
        part = jnp.sum(jnp.where(mask, 1.0, 0.0), axis=0)
        return jnp.sum(part, axis=1, keepdims=True)

    def search(i, tau):
        cand = tau | jnp.left_shift(jnp.int32(1), 30 - i)
        return jnp.where(count(bits >= cand[None]) >= cap, cand, tau)

    tau = lax.fori_loop(0, 31, search, jnp.zeros((ne, 1), I32))[None]

    r = lax.broadcasted_iota(I32, (LANES, LANES), 0)
    c = lax.broadcasted_iota(I32, (LANES, LANES), 1)
    incl = jnp.where(r <= c, 1.0, 0.0).astype(BF16)
    ones = jnp.ones((LANES, LANES), BF16)

    def prefix(mask):
        m = jnp.where(mask, 1.0, 0.0)
        m2 = m.reshape(nblk * ne, LANES).astype(BF16)
        pre_ref[...] = _dot(m2, incl).reshape(nblk, ne, LANES) - m
        tot_ref[...] = _dot(m2, ones).reshape(nblk, ne, LANES)

        def carry(b, run):
            pre_ref[b] = pre_ref[b] + run
            return run + tot_ref[b]

        lax.fori_loop(0, nblk, carry, jnp.zeros((ne, LANES), F32))
        return pre_ref[...]

    above = bits > tau
    tie = bits == tau
    need = cap - count(above)
    sel = above | (tie & (prefix(tie) < need[None]))
    rank = prefix(sel).astype(I32)
    rankx_ref[...] = rank
    rankm_ref[...] = jnp.where(sel, rank, -1)


def _select(aff3, cap):
    nblk = aff3.shape[0]
    shp = (nblk, N_EXPERTS, LANES)
    spec = pl.BlockSpec(shp, lambda i: (0, 0, 0))
    return pl.pallas_call(
        functools.partial(_select_kernel, cap=cap),
        grid=(1,),
        in_specs=[spec],
        out_specs=[spec, spec],
        out_shape=[jax.ShapeDtypeStruct(shp, I32), jax.ShapeDtypeStruct(shp, I32)],
        scratch_shapes=[pltpu.VMEM(shp, F32), pltpu.VMEM(shp, F32)],
        compiler_params=_params("arbitrary"),
        name="select",
    )(aff3)


def _expert_kernel(base_ref, rank_ref, h_ref, g_ref, wg_ref, wu_ref, wd_ref, ye_ref, xe_ref, xg_ref,
                   *, nt, sub, tt, sj, cap, win):
    e = pl.program_id(0)
    s = pl.program_id(1)

    @pl.when(s == 0)
    def _():
        xe_ref[...] = jnp.zeros_like(xe_ref)
        xg_ref[...] = jnp.zeros_like(xg_ref)

    @pl.when(s < nt)
    def _():
        slot = lax.broadcasted_iota(I32, (win, tt), 0)
        for sub_i in range(sub):
            tile = e * (nt * sub + 1) + s * sub + sub_i
            base = base_ref[tile]
            count = base_ref[tile + 1] - base
            wb = (base // ROW_ALIGN) * ROW_ALIGN
            nwin = jnp.where(count > 0, (base - wb + count + win - 1) // win, 0)
            rk = rank_ref[0, sub_i]
            rel = jnp.concatenate([rk[a:a + 1, :] for a in range(tt // LANES)], axis=1) - wb
            rows = slice(sub_i * tt, (sub_i + 1) * tt)

            def fill(w, carry, wb=wb, rel=rel, rows=rows):
                onehot = jnp.where(slot == rel - w * win, 1.0, 0.0).astype(BF16)
                r0 = pl.multiple_of(wb + w * win, ROW_ALIGN)
                got = xe_ref[pl.ds(r0, win), :].astype(F32) + _dot(onehot, h_ref[rows, :])
                xe_ref[pl.ds(r0, win), :] = got.astype(BF16)
                xg_ref[pl.ds(r0, win), :] += _dot(onehot, g_ref[rows, :])
                return carry

            lax.fori_loop(0, nwin, fill, 0)

    @pl.when(s >= nt)
    def _():
        j0 = pl.multiple_of((s - nt) * sj, sj)
        xj = xe_ref[pl.ds(j0, sj), :]
        a = _dot(xj, wg_ref[0])
        u = _dot(xj, wu_ref[0])
        hid = (a * _sigmoid(a) * u).astype(BF16)
        lane = lax.broadcasted_iota(I32, (sj, LANES), 1)
        mine = (lane == e) | (lane == N_EXPERTS + e)
        gate = jnp.sum(jnp.where(mine, xg_ref[pl.ds(j0, sj), :], 0.0), axis=1, keepdims=True)
        ye_ref[0] = (_dot(hid, wd_ref[0]) * gate).astype(BF16)


def _experts(base_flat, rank4, h2, affp, wg, wu, wd, layer, cap, tt, sub, sj):
    n = h2.shape[0]
    nt = n // (tt * sub)
    nj = cap // sj
    win = GATHER_ROWS
    wspec = pl.BlockSpec((pl.Squeezed(), 1, D_MODEL, D_MODEL), lambda e, s, b: (layer, e, 0, 0))
    grid_spec = pltpu.PrefetchScalarGridSpec(
        num_scalar_prefetch=1,
        grid=(N_EXPERTS, nt + nj),
        in_specs=[pl.BlockSpec((1, sub, tt // LANES, LANES),
                               lambda e, s, b: (e, jnp.minimum(s, nt - 1), 0, 0)),
                  pl.BlockSpec((sub * tt, D_MODEL), lambda e, s, b: (jnp.minimum(s, nt - 1), 0)),
                  pl.BlockSpec((sub * tt, LANES), lambda e, s, b: (jnp.minimum(s, nt - 1), 0)),
                  wspec, wspec, wspec],
        out_specs=pl.BlockSpec((1, sj, D_MODEL), lambda e, s, b: (e, jnp.maximum(s - nt, 0), 0)),
        scratch_shapes=[pltpu.VMEM((cap + win + ROW_ALIGN, D_MODEL), BF16),
                        pltpu.VMEM((cap + win + ROW_ALIGN, LANES), F32)],
    )
    return pl.pallas_call(
        functools.partial(_expert_kernel, nt=nt, sub=sub, tt=tt, sj=sj, cap=cap, win=win),
        grid_spec=grid_spec,
        out_shape=jax.ShapeDtypeStruct((N_EXPERTS, cap, D_MODEL), BF16),
        compiler_params=_params("arbitrary", "arbitrary"),
        name="experts",
    )(base_flat, rank4, h2, affp, wg, wu, wd)


def _main_start(base, cap, w1):
    return pl.multiple_of(jnp.minimum((base // 16) * 16, cap - w1), 16)


def _over_start(base, nxt, cap, w1, w2):
    wb = _main_start(base, cap, w1)
    need = nxt - wb > w1
    return need, pl.multiple_of(jnp.where(need, jnp.minimum(wb + w1, cap - w2), 0), 16)


def _combine_kernel(base_ref, x_ref, rank_ref, nrm_ref, *rest, nt, cap, w1, w2, final):
    ne = N_EXPERTS
    main_refs = rest[:ne]
    over_refs = rest[ne:2 * ne]
    out_ref = rest[2 * ne]
    t = pl.program_id(0)
    tt = x_ref.shape[0]
    rank = rank_ref[...]
    bases = [base_ref[e * (nt + 1) + t] for e in range(ne)]
    nexts = [base_ref[e * (nt + 1) + t + 1] for e in range(ne)]

    col = lax.broadcasted_iota(I32, (1, ne), 1)
    wb_row = jnp.zeros((1, ne), I32)
    for e in range(ne):
        wb_row = jnp.where(col == e, _main_start(bases[e], cap, w1), wb_row)
    rel = rank - wb_row
    adj = jnp.where((rank >= 0) & (rel < w1), rel, -1)
    per_vreg = LANES // w1
    lane = lax.broadcasted_iota(I32, (tt, LANES), 1)
    onehots = []
    for i in range(ne // per_vreg):
        target = jnp.full((tt, LANES), -1, I32)
        for a in range(per_vreg):
            e = i * per_vreg + a
            inside = (lane >= a * w1) & (lane < (a + 1) * w1)
            target = jnp.where(inside, adj[:, e:e + 1] + a * w1, target)
        onehots.append(jnp.where(lane == target, 1.0, 0.0).astype(BF16))
    y_all = jnp.concatenate([r[...] for r in main_refs], axis=0)
    out_ref[...] = x_ref[...] + _dot(jnp.concatenate(onehots, axis=1), y_all)

    for e in range(ne):
        need, wo = _over_start(bases[e], nexts[e], cap, w1, w2)

        @pl.when(need)
        def _(e=e, wo=wo):
            lane2 = lax.broadcasted_iota(I32, (tt, w2), 1)
            first = _main_start(bases[e], cap, w1) + w1
            r = rank_ref[:, e:e + 1]
            match = (lane2 == r - wo) & (r >= first)
            onehot = jnp.where(match, 1.0, 0.0).astype(BF16)
            out_ref[...] += _dot(onehot, over_refs[e][...])

    if final:
        o = out_ref[...]
        ms = jnp.mean(o * o, axis=-1, keepdims=True)
        out_ref[...] = o * lax.rsqrt(ms + NORM_EPS) * nrm_ref[...]


def _combine(base_flat, x1, rank_tok, ye, nrm, cap, tt, final):
    n = x1.shape[0]
    nt = n // tt
    w1 = COMBINE_MAIN_ROWS
    w2 = tt + 16 - w1
    assert cap >= max(w1, w2) and cap % 16 == 0 and LANES % w1 == 0
    row = lambda w: pl.BlockSpec((tt, w), lambda t, b: (t, 0))

    def main_spec(e):
        return pl.BlockSpec(
            (pl.Squeezed(), pl.Element(w1), pl.Element(D_MODEL)),
            lambda t, b: (e, _main_start(b[e * (nt + 1) + t], cap, w1), 0))

    def over_spec(e):
        return pl.BlockSpec(
            (pl.Squeezed(), pl.Element(w2), pl.Element(D_MODEL)),
            lambda t, b: (e, _over_start(b[e * (nt + 1) + t], b[e * (nt + 1) + t + 1], cap, w1, w2)[1], 0))

    grid_spec = pltpu.PrefetchScalarGridSpec(
        num_scalar_prefetch=1,
        grid=(nt,),
        in_specs=[row(D_MODEL), row(N_EXPERTS),
                  pl.BlockSpec((1, D_MODEL), lambda t, b: (0, 0))]
                 + [main_spec(e) for e in range(N_EXPERTS)]
                 + [over_spec(e) for e in range(N_EXPERTS)],
        out_specs=row(D_MODEL),
    )
    return pl.pallas_call(
        functools.partial(_combine_kernel, nt=nt, cap=cap, w1=w1, w2=w2, final=final),
        grid_spec=grid_spec,
        out_shape=jax.ShapeDtypeStruct((n, D_MODEL), F32),
        compiler_params=_params("arbitrary"),
        name="combine",
    )(base_flat, x1, rank_tok, nrm.reshape(1, D_MODEL).astype(F32), *([ye] * (2 * N_EXPERTS)))


def _tile(n, want):
    t = min(n, want)
    assert n % t == 0, (n, want)
    return t


def _routed_ffn(x1, h2, aff, affp, wg, wu, wd, layer, nrm, final):
    n = x1.shape[0]
    cap = (EC_CAPACITY_FACTOR * n) // N_EXPERTS
    nblk = n // LANES
    tt_e = _tile(n, 1024)
    sj = _tile(cap, 512)
    tt_c = _tile(n, 256)

    aff3 = aff.reshape(nblk, LANES, N_EXPERTS).transpose(0, 2, 1)
    rankm3, rankx3 = _select(aff3, cap)
    rank_tok = rankm3.transpose(0, 2, 1).reshape(n, N_EXPERTS)
    rank4 = rankm3.transpose(1, 0, 2).reshape(N_EXPERTS, n // tt_e, tt_e // LANES, LANES)
    base128 = rankx3[:, :, 0].T
    last = jnp.full((N_EXPERTS, 1), cap, I32)
    base_e = jnp.concatenate([base128[:, ::tt_e // LANES], last], axis=1).reshape(-1)
    base_c = jnp.concatenate([base128[:, ::tt_c // LANES], last], axis=1).reshape(-1)

    sub = max(c for c in (4, 2, 1) if (n // tt_e) % c == 0)
    ye = _experts(base_e, rank4, h2, affp, wg, wu, wd, layer, cap, tt_e, sub, sj)
    return _combine(base_c, x1, rank_tok, ye, nrm, cap, tt_c, final)


def _trunk(x, p):
    bsz, seq, _ = x.shape
    n = bsz * seq
    depth = p["w_in"].shape[0]
    tm = _tile(seq, 512)
    tq = _tile(seq, 512)
    tk = _tile(seq, 2048)
    t_hg = _tile(seq, 512)
    tables = _rope_tables(seq)

    x2 = x.reshape(n, D_MODEL)
    for l in range(depth):
        bz, q, k, v, hg = _inproj(x2, p["norm_mix"][l], p["w_in"][l], tables, seq, tm)
        to3 = lambda a: a.reshape(bsz, seq, a.shape[-1])
        ydiff = _attention(to3(q), to3(k), to3(v), p["diff_lambda"][l], p["diff_subln"][l], l, tq, tk)
        o_f, o_b = _hgrn(to3(hg), p["hgrn_lb"], l, t_hg)
        x1, h2, aff, affp = _outproj(x2, bz, ydiff.reshape(n, DIFF_WIDTH), o_f.reshape(n, HG_WIDTH),
                               o_b.reshape(n, HG_WIDTH), hg, p["conv_w"][l], p["hgrn_norm"][l],
                               p["w_out"][l], p["norm_ffn"][l], p["router_w"][l], seq, tm)
        x2 = _routed_ffn(x1, h2, aff, affp, p["w_gate"], p["w_up"], p["w_down"], l,
                         p["norm_final"], l == depth - 1)
    return x2.reshape(bsz, seq, D_MODEL)


def _split2_param(w):
    w = w.astype(F32)
    hi = w.astype(BF16)
    lo = (w - hi.astype(F32)).astype(BF16)
    return jnp.concatenate([hi, lo], axis=-1)


def kernel(x_prompt, x_sample, norm_mix, w_in, conv_w, diff_lambda, diff_subln, hgrn_lb,
           hgrn_norm, w_out, norm_ffn, router_w, w_gate, w_up, w_down, norm_final):
    p = dict(
        norm_mix=norm_mix.astype(F32), w_in=w_in.astype(BF16), conv_w=conv_w,
        diff_lambda=diff_lambda, diff_subln=diff_subln, hgrn_lb=hgrn_lb, hgrn_norm=hgrn_norm,
        w_out=w_out.astype(BF16), norm_ffn=norm_ffn,
        router_w=_split2_param(router_w),
        w_gate=w_gate.astype(BF16), w_up=w_up.astype(BF16), w_down=w_down.astype(BF16),
        norm_final=norm_final,
    )
    return _trunk(x_prompt, p), _trunk(x_sample, p)
```

```python
import functools
import math

import jax
import jax.numpy as jnp
from jax import lax
from jax.experimental import pallas as pl
from jax.experimental.pallas import tpu as pltpu

F32 = jnp.float32
BF16 = jnp.bfloat16
I32 = jnp.int32

D_MODEL = 1024
CONV_WIDTH = 256
DIFF_WIDTH = 512
DIFF_HEADS = 4
DIFF_HEAD_DIM = 64
HG_WIDTH = 256
HG_HEADS = 4
HG_HEAD_DIM = 64
HG_CHUNK = 64
D_PROJ = 3584
ROPE_THETA = 10000.0
N_EXPERTS = 16
EC_CAPACITY_FACTOR = 2
NORM_EPS = 1e-6

VMEM_LIMIT_BYTES = 56 * 1024 * 1024
LANES = 128
ROW_ALIGN = 16
GATHER_ROWS = 256
COMBINE_MAIN_ROWS = 64

NT_DIMS = (((1,), (1,)), ((), ()))
Q_SCALE = DIFF_HEAD_DIM ** -0.5 * math.log2(math.e)


def _params(*sem):
    return pltpu.CompilerParams(dimension_semantics=sem, vmem_limit_bytes=VMEM_LIMIT_BYTES)


def _split3(x):
    hi = x.astype(BF16)
    r1 = x - hi.astype(F32)
    mid = r1.astype(BF16)
    lo = (r1 - mid.astype(F32)).astype(BF16)
    return hi, mid, lo


def _dot(a, b):
    return jnp.dot(a, b, preferred_element_type=F32)


def _dot_nt(a, b):
    return lax.dot_general(a, b, NT_DIMS, preferred_element_type=F32)


def _dot_exact_lhs(a_bf16, x_f32):
    hi, mid, lo = _split3(x_f32)
    return _dot(a_bf16, hi) + _dot(a_bf16, mid) + _dot(a_bf16, lo)


def _sigmoid(x):
    return 1.0 / (1.0 + jnp.exp(-x))


def _inproj_kernel(x_ref, g_ref, w_ref, cos_ref, sina_ref, sinb_ref,
                   bz_ref, q_ref, k_ref, v_ref, hg_ref):
    x = x_ref[...]
    ms = jnp.mean(x * x, axis=-1, keepdims=True)
    h = (x * lax.rsqrt(ms + NORM_EPS) * g_ref[...]).astype(BF16)

    def proj(a, b):
        return _dot(h, w_ref[:, a:b])

    cw = CONV_WIDTH
    bz_ref[:, 0:cw] = proj(0, cw).astype(BF16)
    bz_ref[:, cw:2 * cw] = (proj(cw, 2 * cw) * proj(2 * cw, 3 * cw)).astype(BF16)

    reps = DIFF_WIDTH // LANES
    cos = jnp.concatenate([cos_ref[...]] * reps, axis=1)
    sina = jnp.concatenate([sina_ref[...]] * reps, axis=1)
    sinb = jnp.concatenate([sinb_ref[...]] * reps, axis=1)
    half = DIFF_HEAD_DIM // 2

    def rope(t):
        return (t * cos + pltpu.roll(t, DIFF_WIDTH - half, axis=1) * sina
                + pltpu.roll(t, half, axis=1) * sinb)

    o = 3 * cw
    q_ref[...] = (rope(proj(o, o + DIFF_WIDTH)) * Q_SCALE).astype(BF16)
    o += DIFF_WIDTH
    k_ref[...] = rope(proj(o, o + DIFF_WIDTH)).astype(BF16)
    o += DIFF_WIDTH
    v = proj(o, o + DIFF_WIDTH).astype(BF16)
    hd = 2 * DIFF_HEAD_DIM
    ones = jnp.ones((v.shape[0], hd), BF16)
    for head in range(DIFF_HEADS):
        v_ref[:, 2 * head * hd:(2 * head + 1) * hd] = v[:, head * hd:(head + 1) * hd]
        v_ref[:, (2 * head + 1) * hd:(2 * head + 2) * hd] = ones
    o += DIFF_WIDTH
    hg_ref[...] = proj(o, D_PROJ)


def _rope_tables(seq):
    d = DIFF_HEAD_DIM
    inv = 1.0 / (ROPE_THETA ** (jnp.arange(0, d, 2, dtype=F32) / d))
    ang = jnp.arange(seq, dtype=F32)[:, None] * inv[None, :]
    ang = jnp.concatenate([ang, ang, ang, ang], axis=-1)
    cos, sin = jnp.cos(ang), jnp.sin(ang)
    first = (jnp.arange(LANES) % d) < (d // 2)
    sina = jnp.where(first[None, :], -sin, 0.0)
    sinb = jnp.where(first[None, :], 0.0, sin)
    return cos, sina, sinb


def _inproj(x2, g, w_bf16, tables, seq, tm):
    n = x2.shape[0]
    nblk_s = seq // tm
    cos, sina, sinb = tables
    tab_spec = pl.BlockSpec((tm, LANES), lambda i: (i % nblk_s, 0))
    row = lambda w: pl.BlockSpec((tm, w), lambda i: (i, 0))
    hgw = 5 * HG_WIDTH
    return pl.pallas_call(
        _inproj_kernel,
        grid=(n // tm,),
        in_specs=[row(D_MODEL),
                  pl.BlockSpec((1, D_MODEL), lambda i: (0, 0)),
                  pl.BlockSpec((D_MODEL, D_PROJ), lambda i: (0, 0)),
                  tab_spec, tab_spec, tab_spec],
        out_specs=[row(2 * CONV_WIDTH), row(DIFF_WIDTH), row(DIFF_WIDTH), row(2 * DIFF_WIDTH), row(hgw)],
        out_shape=[jax.ShapeDtypeStruct((n, 2 * CONV_WIDTH), BF16),
                   jax.ShapeDtypeStruct((n, DIFF_WIDTH), BF16),
                   jax.ShapeDtypeStruct((n, DIFF_WIDTH), BF16),
                   jax.ShapeDtypeStruct((n, 2 * DIFF_WIDTH), BF16),
                   jax.ShapeDtypeStruct((n, hgw), F32)],
        compiler_params=_params("parallel"),
        name="inproj",
    )(x2, g.reshape(1, D_MODEL), w_bf16, cos, sina, sinb)


def _attn_kernel(q_ref, k_ref, v_ref, lam_ref, g_ref, o_ref, *, tk, lam_init):
    q = q_ref[0]
    tq = q.shape[0]
    seq = k_ref.shape[1]
    nblk = seq // tk
    d = DIFF_HEAD_DIM
    hd = 2 * d
    lane = lax.broadcasted_iota(I32, q.shape, 1)
    zero = jnp.zeros_like(q)
    qs = (jnp.where(lane < d, q, zero), jnp.where(lane >= d, q, zero))

    def scores(j, h):
        s = _dot_nt(qs[h], k_ref[0, j * tk:(j + 1) * tk, :])
        return s, jnp.max(s, axis=1, keepdims=True)

    def consume(j, sc, st):
        m, acc = st
        s, bm = sc
        mn = jnp.maximum(m, bm)
        p = jnp.exp2(s - mn).astype(BF16)
        return mn, jnp.exp2(m - mn) * acc + _dot(p, v_ref[0, j * tk:(j + 1) * tk, :])

    init = (jnp.full((tq, 1), -1e30, F32), jnp.zeros((tq, 2 * hd), F32))
    state = [init, init]
    sc = [scores(0, 0), scores(0, 1)]
    for j in range(nblk):
        nxt = [None, None]
        for h in range(2):
            if j + 1 < nblk:
                nxt[h] = scores(j + 1, h)
            state[h] = consume(j, sc[h], state[h])
        sc = nxt
    (_, a1), (_, a2) = state

    lp = lam_ref[...]
    lam = (jnp.exp(jnp.sum(lp[0:1] * lp[1:2], axis=1, keepdims=True))
           - jnp.exp(jnp.sum(lp[2:3] * lp[3:4], axis=1, keepdims=True)) + lam_init)
    o = a1[:, :hd] / a1[:, hd:] - lam * (a2[:, :hd] / a2[:, hd:])
    ms = jnp.mean(o * o, axis=-1, keepdims=True)
    y = o * lax.rsqrt(ms + NORM_EPS) * g_ref[...] * (1.0 - lam_init)
    o_ref[0] = y.astype(BF16)


def _attention(q3, k3, v3, lam_params, subln_g, layer, tq, tk):
    bsz, seq, _ = q3.shape
    lam_init = 0.8 - 0.6 * math.exp(-0.3 * layer)
    hd = 2 * DIFF_HEAD_DIM
    return pl.pallas_call(
        functools.partial(_attn_kernel, tk=tk, lam_init=lam_init),
        grid=(bsz, DIFF_HEADS, seq // tq),
        in_specs=[pl.BlockSpec((1, tq, hd), lambda b, h, i: (b, i, h)),
                  pl.BlockSpec((1, seq, hd), lambda b, h, i: (b, 0, h)),
                  pl.BlockSpec((1, seq, 2 * hd), lambda b, h, i: (b, 0, h)),
                  pl.BlockSpec((4, DIFF_HEAD_DIM), lambda b, h, i: (0, 0)),
                  pl.BlockSpec((1, hd), lambda b, h, i: (0, 0))],
        out_specs=pl.BlockSpec((1, tq, hd), lambda b, h, i: (b, i, h)),
        out_shape=jax.ShapeDtypeStruct((bsz, seq, DIFF_WIDTH), BF16),
        compiler_params=_params("parallel", "parallel", "parallel"),
        name="attn",
    )(q3, k3, v3, lam_params.astype(F32), subln_g.reshape(1, hd).astype(F32))


def _hgrn_kernel(qf_ref, ff_ref, if_ref, qb_ref, fb_ref, ib_ref, lbp_ref,
                 of_ref, ob_ref, stf_ref, stb_ref, *, layer):
    c_len = HG_CHUNK
    w = HG_WIDTH
    hd = HG_HEAD_DIM
    t_len = qf_ref.shape[1]
    nchunk = t_len // c_len

    @pl.when(pl.program_id(1) == 0)
    def _():
        stf_ref[...] = jnp.zeros_like(stf_ref)
        stb_ref[...] = jnp.zeros_like(stb_ref)

    lbp = lbp_ref[...]
    e = jnp.exp(lbp - jnp.max(lbp, axis=0, keepdims=True))
    sm = e / jnp.sum(e, axis=0, keepdims=True)
    lb = jnp.zeros((1, 2 * w), F32)
    for l in range(1, layer + 1):
        lb = lb + sm[l:l + 1]
    lb_f, lb_b = lb[:, :w], lb[:, w:]

    r = lax.broadcasted_iota(I32, (c_len, c_len), 0)
    c = lax.broadcasted_iota(I32, (c_len, c_len), 1)
    low_m = jnp.where(r >= c, 1.0, 0.0).astype(BF16)
    upp_m = jnp.where(r <= c, 1.0, 0.0).astype(BF16)
    rq = lax.broadcasted_iota(I32, (c_len, HG_HEADS * c_len), 0)
    ck = lax.broadcasted_iota(I32, (c_len, HG_HEADS * c_len), 1) % c_len
    low = rq >= ck
    upp = rq <= ck
    br = lax.broadcasted_iota(I32, (w, w), 0) // hd
    bc = lax.broadcasted_iota(I32, (w, w), 1) // hd
    same_head = br == bc
    mid = c_len // 2

    def bmm(a, b):
        return jnp.einsum('nik,nkj->nij', a, b, preferred_element_type=F32)

    def bmm_nt(a, b):
        return jnp.einsum('nik,njk->nij', a, b, preferred_element_type=F32)

    def scan_block(xq, xf, xi, lbd, tri_m, tri, last, order, st_ref):
        c3 = lambda a: a.reshape(nchunk, c_len, w)
        qh = c3(xq * _sigmoid(xq) * (hd ** -0.5))
        f = lbd + (1.0 - lbd) * _sigmoid(xf)
        kk = c3(1.0 - f)
        tri_b = jnp.broadcast_to(tri_m[None], (nchunk, c_len, c_len))
        g_hi, g_mid, g_lo = _split3(c3(jnp.log(f)))
        G = bmm(tri_b, g_hi) + bmm(tri_b, g_mid) + bmm(tri_b, g_lo)
        g_ref_row = G[:, mid:mid + 1, :]
        g_last = G[:, last:last + 1, :]
        qt = (qh * jnp.exp(G - g_ref_row)).astype(BF16)
        kt = (kk * jnp.exp(g_ref_row - G)).astype(BF16)
        q_in = (qh * jnp.exp(G)).astype(BF16)
        k_out = (kk * jnp.exp(g_last - G)).astype(BF16)
        vb = c3(xi).astype(BF16)
        rep = lambda a: jnp.concatenate([a] * HG_HEADS, axis=1)
        zb = jnp.zeros((nchunk, w, w), BF16)
        a = bmm_nt(qt, jnp.where(same_head[None], rep(kt), zb))
        a = jnp.where(tri[None], a, 0.0).astype(BF16)
        o = bmm(a, jnp.where(same_head[None], rep(vb), zb))
        upd = jnp.einsum('ncv,nck->nvk', vb, k_out, preferred_element_type=F32)
        upd = jnp.where(same_head[None], upd, 0.0)
        decay = jnp.exp(g_last)
        st = st_ref[...]
        inter = [None] * nchunk
        for ci in order:
            inter[ci] = _dot_nt(q_in[ci], st.astype(BF16))
            st = st * decay[ci] + upd[ci]
        st_ref[...] = st
        return o.reshape(nchunk * c_len, w) + jnp.concatenate(inter, axis=0)

    fwd_order = list(range(nchunk))
    of_ref[0] = scan_block(qf_ref[0], ff_ref[0], if_ref[0], lb_f, low_m, low, c_len - 1,
                           fwd_order, stf_ref)
    ob_ref[0] = scan_block(qb_ref[0], fb_ref[0], ib_ref[0], lb_b, upp_m, upp, 0,
                           fwd_order[::-1], stb_ref)


def _hgrn(hg3, hgrn_lb, layer, t_len):
    bsz, seq, _ = hg3.shape
    nt = seq // t_len
    w = HG_WIDTH
    assert HG_CHUNK == HG_HEAD_DIM
    fwd = lambda col: pl.BlockSpec((1, t_len, w), lambda b, j: (b, j, col))
    bwd = lambda col: pl.BlockSpec((1, t_len, w), lambda b, j: (b, nt - 1 - j, col))
    depth = hgrn_lb.shape[0]
    return pl.pallas_call(
        functools.partial(_hgrn_kernel, layer=layer),
        grid=(bsz, nt),
        in_specs=[fwd(0), fwd(1), fwd(3), bwd(0), bwd(2), bwd(3),
                  pl.BlockSpec((depth, 2 * w), lambda b, j: (0, 0))],
        out_specs=[pl.BlockSpec((1, t_len, w), lambda b, j: (b, j, 0)),
                   pl.BlockSpec((1, t_len, w), lambda b, j: (b, nt - 1 - j, 0))],
        out_shape=[jax.ShapeDtypeStruct((bsz, seq, w), F32),
                   jax.ShapeDtypeStruct((bsz, seq, w), F32)],
        scratch_shapes=[pltpu.VMEM((w, w), F32), pltpu.VMEM((w, w), F32)],
        compiler_params=_params("parallel", "arbitrary"),
        name="hgrn",
    )(hg3, hg3, hg3, hg3, hg3, hg3, hgrn_lb.astype(F32))


def _outproj_kernel(x_ref, bz_ref, zp_ref, zn_ref, yd_ref, of_ref, ob_ref, hgate_ref,
                    cw_ref, hn_ref, wo_ref, nf_ref, rw_ref,
                    x1_ref, h2_ref, aff_ref, affp_ref, *, blocks_per_seq):
    tm = x_ref.shape[0]
    cwid = CONV_WIDTH
    i = pl.program_id(0)
    pos = i % blocks_per_seq
    halo = zp_ref.shape[0]

    b = bz_ref[:, 0:cwid].astype(F32)
    z = bz_ref[:, cwid:2 * cwid].astype(F32)
    zprev = zp_ref[:, cwid:2 * cwid].astype(F32)[halo - 1:halo]
    znext = zn_ref[:, cwid:2 * cwid].astype(F32)[0:1]
    zprev = jnp.where(pos == 0, 0.0, zprev)
    znext = jnp.where(pos == blocks_per_seq - 1, 0.0, znext)
    row = lax.broadcasted_iota(I32, (tm, cwid), 0)
    z_up = jnp.where(row == 0, zprev, pltpu.roll(z, 1, axis=0))
    z_dn = jnp.where(row == tm - 1, znext, pltpu.roll(z, tm - 1, axis=0))
    cw = cw_ref[...]
    y_conv = b * (cw[0:1] * z_up + cw[1:2] * z + cw[2:3] * z_dn)

    o = of_ref[...] + ob_ref[...]
    w = HG_WIDTH
    br = lax.broadcasted_iota(I32, (w, w), 0) // HG_HEAD_DIM
    bc = lax.broadcasted_iota(I32, (w, w), 1) // HG_HEAD_DIM
    pool = jnp.where(br == bc, 1.0, 0.0).astype(BF16)
    sq = o * o
    sq_hi = sq.astype(BF16)
    sq_lo = (sq - sq_hi.astype(F32)).astype(BF16)
    ms = (_dot(sq_hi, pool) + _dot(sq_lo, pool)) * (1.0 / HG_HEAD_DIM)
    gt = hgate_ref[...]
    y_h = o * lax.rsqrt(ms + NORM_EPS) * hn_ref[...] * (gt * _sigmoid(gt))

    mixed = (_dot(y_conv.astype(BF16), wo_ref[0:cwid, :])
             + _dot(yd_ref[...], wo_ref[cwid:cwid + DIFF_WIDTH, :])
             + _dot(y_h.astype(BF16), wo_ref[cwid + DIFF_WIDTH:, :]))
    x1 = x_ref[...] + mixed
    x1_ref[...] = x1

    ms2 = jnp.mean(x1 * x1, axis=-1, keepdims=True)
    h2 = x1 * lax.rsqrt(ms2 + NORM_EPS) * nf_ref[...]
    h_hi = h2.astype(BF16)
    h2_ref[...] = h_hi

    h_lo = (h2 - h_hi.astype(F32)).astype(BF16)
    rw = rw_ref[...]
    d_hi = _dot(h_hi, rw)
    d_lo = _dot(h_lo, rw)
    ne = N_EXPERTS
    logits = d_hi[:, :ne] + (d_hi[:, ne:] + d_lo[:, :ne]) + d_lo[:, ne:]
    ex = jnp.exp(logits - jnp.max(logits, axis=-1, keepdims=True))
    aff = ex / jnp.sum(ex, axis=-1, keepdims=True)
    aff_ref[...] = aff
    a_hi = aff.astype(BF16)
    a_lo = (aff - a_hi.astype(F32)).astype(BF16)
    pad = jnp.zeros((aff.shape[0], LANES - 2 * ne), BF16)
    affp_ref[...] = jnp.concatenate([a_hi, a_lo, pad], axis=1)


def _outproj(x2, bz, ydiff, of2, ob2, hg2, conv_w, hgrn_norm, wo_bf16, norm_ffn, rw3, seq, tm):
    n = x2.shape[0]
    halo = 16
    hb = tm // halo
    nhalo = n // halo
    row = lambda w: pl.BlockSpec((tm, w), lambda i: (i, 0))
    full = lambda a, b: pl.BlockSpec((a, b), lambda i: (0, 0))
    hn = jnp.tile(hgrn_norm.astype(F32), HG_HEADS).reshape(1, HG_WIDTH)
    return pl.pallas_call(
        functools.partial(_outproj_kernel, blocks_per_seq=seq // tm),
        grid=(n // tm,),
        in_specs=[row(D_MODEL), row(2 * CONV_WIDTH),
                  pl.BlockSpec((halo, 2 * CONV_WIDTH), lambda i: (jnp.maximum(i * hb - 1, 0), 0)),
                  pl.BlockSpec((halo, 2 * CONV_WIDTH), lambda i: (jnp.minimum((i + 1) * hb, nhalo - 1), 0)),
                  row(DIFF_WIDTH), row(HG_WIDTH), row(HG_WIDTH),
                  pl.BlockSpec((tm, HG_WIDTH), lambda i: (i, 4)),
                  full(3, CONV_WIDTH), full(1, HG_WIDTH), full(D_MODEL, D_MODEL), full(1, D_MODEL),
                  full(D_MODEL, 2 * N_EXPERTS)],
        out_specs=[row(D_MODEL), row(D_MODEL), row(N_EXPERTS), row(LANES)],
        out_shape=[jax.ShapeDtypeStruct((n, D_MODEL), F32),
                   jax.ShapeDtypeStruct((n, D_MODEL), BF16),
                   jax.ShapeDtypeStruct((n, N_EXPERTS), F32),
                   jax.ShapeDtypeStruct((n, LANES), BF16)],
        compiler_params=_params("parallel"),
        name="outproj",
    )(x2, bz, bz, bz, ydiff, of2, ob2, hg2, conv_w.astype(F32), hn, wo_bf16,
      norm_ffn.reshape(1, D_MODEL).astype(F32), rw3)


def _select_kernel(aff_ref, rankm_ref, rankx_ref, pre_ref, tot_ref, *, cap):
    nblk = aff_ref.shape[0]
    ne = N_EXPERTS
    bits = pltpu.bitcast(aff_ref[...], I32)

    def count(mask):
        part = jnp.sum(jnp.where(mask, 1.0, 0.0), axis=0)
        return jnp.sum(part, axis=1, keepdims=True)

    def search(i, tau):
        cand = tau | jnp.left_shift(jnp.int32(1), 30 - i)
        return jnp.where(count(bits >= cand[None]) >= cap, cand, tau)

    tau = lax.fori_loop(0, 31, search, jnp.zeros((ne, 1), I32))[None]

    r = lax.broadcasted_iota(I32, (LANES, LANES), 0)
    c = lax.broadcasted_iota(I32, (LANES, LANES), 1)
    incl = jnp.where(r <= c, 1.0, 0.0).astype(BF16)
    ones = jnp.ones((LANES, LANES), BF16)

    def prefix(mask):
        m = jnp.where(mask, 1.0, 0.0)
        m2 = m.reshape(nblk * ne, LANES).astype(BF16)
        pre_ref[...] = _dot(m2, incl).reshape(nblk, ne, LANES) - m
        tot_ref[...] = _dot(m2, ones).reshape(nblk, ne, LANES)

        def carry(b, run):
            pre_ref[b] = pre_ref[b] + run
            return run + tot_ref[b]

        lax.fori_loop(0, nblk, carry, jnp.zeros((ne, LANES), F32))
        return pre_ref[...]

    above = bits > tau
    tie = bits == tau
    need = cap - count(above)
    sel = above | (tie & (prefix(tie) < need[None]))
    rank = prefix(sel).astype(I32)
    rankx_ref[...] = rank
    rankm_ref[...] = jnp.where(sel, rank, -1)


def _select(aff3, cap):
    nblk = aff3.shape[0]
    shp = (nblk, N_EXPERTS, LANES)
    spec = pl.BlockSpec(shp, lambda i: (0, 0, 0))
    return pl.pallas_call(
        functools.partial(_select_kernel, cap=cap),
        grid=(1,),
        in_specs=[spec],
        out_specs=[spec, spec],
        out_shape=[jax.ShapeDtypeStruct(shp, I32), jax.ShapeDtypeStruct(shp, I32)],
        scratch_shapes=[pltpu.VMEM(shp, F32), pltpu.VMEM(shp, F32)],
        compiler_params=_params("arbitrary"),
        name="select",
    )(aff3)


def _expert_kernel(base_ref, rank_ref, h_ref, g_ref, wg_ref, wu_ref, wd_ref, ye_ref, xe_ref, xg_ref,
                   *, nt, sub, tt, sj, cap, win):
    e = pl.program_id(0)
    s = pl.program_id(1)

    @pl.when(s == 0)
    def _():
        xe_ref[...] = jnp.zeros_like(xe_ref)
        xg_ref[...] = jnp.zeros_like(xg_ref)

    @pl.when(s < nt)
    def _():
        slot = lax.broadcasted_iota(I32, (win, tt), 0)
        for sub_i in range(sub):
            tile = e * (nt * sub + 1) + s * sub + sub_i
            base = base_ref[tile]
            count = base_ref[tile + 1] - base
            wb = (base // ROW_ALIGN) * ROW_ALIGN
            nwin = jnp.where(count > 0, (base - wb + count + win - 1) // win, 0)
            rk = rank_ref[0, sub_i]
            rel = jnp.concatenate([rk[a:a + 1, :] for a in range(tt // LANES)], axis=1) - wb
            rows = slice(sub_i * tt, (sub_i + 1) * tt)

            def fill(w, carry, wb=wb, rel=rel, rows=rows):
                onehot = jnp.where(slot == rel - w * win, 1.0, 0.0).astype(BF16)
                r0 = pl.multiple_of(wb + w * win, ROW_ALIGN)
                got = xe_ref[pl.ds(r0, win), :].astype(F32) + _dot(onehot, h_ref[rows, :])
                xe_ref[pl.ds(r0, win), :] = got.astype(BF16)
                xg_ref[pl.ds(r0, win), :] += _dot(onehot, g_ref[rows, :])
                return carry

            lax.fori_loop(0, nwin, fill, 0)

    @pl.when(s >= nt)
    def _():
        j0 = pl.multiple_of((s - nt) * sj, sj)
        xj = xe_ref[pl.ds(j0, sj), :]
        a = _dot(xj, wg_ref[0])
        u = _dot(xj, wu_ref[0])
        hid = (a * _sigmoid(a) * u).astype(BF16)
        lane = lax.broadcasted_iota(I32, (sj, LANES), 1)
        mine = (lane == e) | (lane == N_EXPERTS + e)
        gate = jnp.sum(jnp.where(mine, xg_ref[pl.ds(j0, sj), :], 0.0), axis=1, keepdims=True)
        ye_ref[0] = (_dot(hid, wd_ref[0]) * gate).astype(BF16)


def _experts(base_flat, rank4, h2, affp, wg, wu, wd, layer, cap, tt, sub, sj):
    n = h2.shape[0]
    nt = n // (tt * sub)
    nj = cap // sj
    win = GATHER_ROWS
    wspec = pl.BlockSpec((pl.Squeezed(), 1, D_MODEL, D_MODEL), lambda e, s, b: (layer, e, 0, 0))
    grid_spec = pltpu.PrefetchScalarGridSpec(
        num_scalar_prefetch=1,
        grid=(N_EXPERTS, nt + nj),
        in_specs=[pl.BlockSpec((1, sub, tt // LANES, LANES),
                               lambda e, s, b: (e, jnp.minimum(s, nt - 1), 0, 0)),
                  pl.BlockSpec((sub * tt, D_MODEL), lambda e, s, b: (jnp.minimum(s, nt - 1), 0)),
                  pl.BlockSpec((sub * tt, LANES), lambda e, s, b: (jnp.minimum(s, nt - 1), 0)),
                  wspec, wspec, wspec],
        out_specs=pl.BlockSpec((1, sj, D_MODEL), lambda e, s, b: (e, jnp.maximum(s - nt, 0), 0)),
        scratch_shapes=[pltpu.VMEM((cap + win + ROW_ALIGN, D_MODEL), BF16),
                        pltpu.VMEM((cap + win + ROW_ALIGN, LANES), F32)],
    )
    return pl.pallas_call(
        functools.partial(_expert_kernel, nt=nt, sub=sub, tt=tt, sj=sj, cap=cap, win=win),
        grid_spec=grid_spec,
        out_shape=jax.ShapeDtypeStruct((N_EXPERTS, cap, D_MODEL), BF16),
        compiler_params=_params("arbitrary", "arbitrary"),
        name="experts",
    )(base_flat, rank4, h2, affp, wg, wu, wd)


def _main_start(base, cap, w1):
    return pl.multiple_of(jnp.minimum((base // 16) * 16, cap - w1), 16)


def _over_start(base, nxt, cap, w1, w2):
    wb = _main_start(base, cap, w1)
    need = nxt - wb > w1
    return need, pl.multiple_of(jnp.where(need, jnp.minimum(wb + w1, cap - w2), 0), 16)


def _combine_kernel(base_ref, x_ref, rank_ref, nrm_ref, ye_hbm, out_ref,
                    main_buf, over_buf, main_sem, over_sem, *, nt, cap, w1, w2, final):
    ne = N_EXPERTS
    t = pl.program_id(0)
    tt = x_ref.shape[0]
    slot = t % 2

    def windows(tile, e):
        base = base_ref[e * (nt + 1) + tile]
        nxt = base_ref[e * (nt + 1) + tile + 1]
        need, wo = _over_start(base, nxt, cap, w1, w2)
        return _main_start(base, cap, w1), need, wo

    def main_copy(e, wb, sl):
        return pltpu.make_async_copy(ye_hbm.at[e, pl.ds(wb, w1), :],
                                     main_buf.at[sl, pl.ds(e * w1, w1), :], main_sem.at[sl])

    def over_copy(e, wo, sl):
        return pltpu.make_async_copy(ye_hbm.at[e, pl.ds(wo, w2), :], over_buf.at[sl, e],
                                     over_sem.at[sl, e])

    def fetch(tile, sl):
        for e in range(ne):
            wb, need, wo = windows(tile, e)
            main_copy(e, wb, sl).start()

            @pl.when(need)
            def _(e=e, wo=wo):
                over_copy(e, wo, sl).start()

    @pl.when(t == 0)
    def _():
        fetch(0, 0)

    @pl.when(t + 1 < nt)
    def _():
        fetch(t + 1, 1 - slot)

    rank = rank_ref[...]
    wins = [windows(t, e) for e in range(ne)]

    col = lax.broadcasted_iota(I32, (1, ne), 1)
    wb_row = jnp.zeros((1, ne), I32)
    for e in range(ne):
        wb_row = jnp.where(col == e, wins[e][0], wb_row)
    rel = rank - wb_row
    adj = jnp.where((rank >= 0) & (rel < w1), rel, -1)
    per_vreg = LANES // w1
    lane = lax.broadcasted_iota(I32, (tt, LANES), 1)
    onehots = []
    for i in range(ne // per_vreg):
        target = jnp.full((tt, LANES), -1, I32)
        for a in range(per_vreg):
            e = i * per_vreg + a
            inside = (lane >= a * w1) & (lane < (a + 1) * w1)
            target = jnp.where(inside, adj[:, e:e + 1] + a * w1, target)
        onehots.append(jnp.where(lane == target, 1.0, 0.0).astype(BF16))
    for e in range(ne):
        main_copy(e, wins[e][0], slot).wait()
    out_ref[...] = x_ref[...] + _dot(jnp.concatenate(onehots, axis=1), main_buf[slot])

    for e in range(ne):
        wb, need, wo = wins[e]

        @pl.when(need)
        def _(e=e, wb=wb, wo=wo):
            over_copy(e, wo, slot).wait()
            lane2 = lax.broadcasted_iota(I32, (tt, w2), 1)
            r = rank_ref[:, e:e + 1]
            match = (lane2 == r - wo) & (r >= wb + w1)
            onehot = jnp.where(match, 1.0, 0.0).astype(BF16)
            out_ref[...] += _dot(onehot, over_buf[slot, e])

    if final:
        o = out_ref[...]
        ms = jnp.mean(o * o, axis=-1, keepdims=True)
        out_ref[...] = o * lax.rsqrt(ms + NORM_EPS) * nrm_ref[...]


def _combine(base_flat, x1, rank_tok, ye, nrm, cap, tt, final):
    n = x1.shape[0]
    nt = n // tt
    w1 = COMBINE_MAIN_ROWS
    w2 = tt + 16 - w1
    assert cap >= max(w1, w2) and cap % 16 == 0 and LANES % w1 == 0
    row = lambda w: pl.BlockSpec((tt, w), lambda t, b: (t, 0))
    grid_spec = pltpu.PrefetchScalarGridSpec(
        num_scalar_prefetch=1,
        grid=(nt,),
        in_specs=[row(D_MODEL), row(N_EXPERTS),
                  pl.BlockSpec((1, D_MODEL), lambda t, b: (0, 0)),
                  pl.BlockSpec(memory_space=pl.ANY)],
        out_specs=row(D_MODEL),
        scratch_shapes=[pltpu.VMEM((2, N_EXPERTS * w1, D_MODEL), BF16),
                        pltpu.VMEM((2, N_EXPERTS, w2, D_MODEL), BF16),
                        pltpu.SemaphoreType.DMA((2,)),
                        pltpu.SemaphoreType.DMA((2, N_EXPERTS))],
    )
    return pl.pallas_call(
        functools.partial(_combine_kernel, nt=nt, cap=cap, w1=w1, w2=w2, final=final),
        grid_spec=grid_spec,
        out_shape=jax.ShapeDtypeStruct((n, D_MODEL), F32),
        compiler_params=_params("arbitrary"),
        name="combine",
    )(base_flat, x1, rank_tok, nrm.reshape(1, D_MODEL).astype(F32), ye)


def _tile(n, want):
    t = min(n, want)
    assert n % t == 0, (n, want)
    return t


def _routed_ffn(x1, h2, aff, affp, wg, wu, wd, layer, nrm, final):
    n = x1.shape[0]
    cap = (EC_CAPACITY_FACTOR * n) // N_EXPERTS
    nblk = n // LANES
    tt_e = _tile(n, 1024)
    sj = _tile(cap, 512)
    tt_c = _tile(n, 256)

    aff3 = aff.reshape(nblk, LANES, N_EXPERTS).transpose(0, 2, 1)
    rankm3, rankx3 = _select(aff3, cap)
    rank_tok = rankm3.transpose(0, 2, 1).reshape(n, N_EXPERTS)
    rank4 = rankm3.transpose(1, 0, 2).reshape(N_EXPERTS, n // tt_e, tt_e // LANES, LANES)
    base128 = rankx3[:, :, 0].T
    last = jnp.full((N_EXPERTS, 1), cap, I32)
    base_e = jnp.concatenate([base128[:, ::tt_e // LANES], last], axis=1).reshape(-1)
    base_c = jnp.concatenate([base128[:, ::tt_c // LANES], last], axis=1).reshape(-1)

    sub = max(c for c in (4, 2, 1) if (n // tt_e) % c == 0)
    ye = _experts(base_e, rank4, h2, affp, wg, wu, wd, layer, cap, tt_e, sub, sj)
    return _combine(base_c, x1, rank_tok, ye, nrm, cap, tt_c, final)


def _trunk(x, p):
    bsz, seq, _ = x.shape
    n = bsz * seq
    depth = p["w_in"].shape[0]
    tm = _tile(seq, 512)
    tq = _tile(seq, 512)
    tk = _tile(seq, 2048)
    t_hg = _tile(seq, 512)
    tables = _rope_tables(seq)

    x2 = x.reshape(n, D_MODEL)
    for l in range(depth):
        bz, q, k, v, hg = _inproj(x2, p["norm_mix"][l], p["w_in"][l], tables, seq, tm)
        to3 = lambda a: a.reshape(bsz, seq, a.shape[-1])
        ydiff = _attention(to3(q), to3(k), to3(v), p["diff_lambda"][l], p["diff_subln"][l], l, tq, tk)
        o_f, o_b = _hgrn(to3(hg), p["hgrn_lb"], l, t_hg)
        x1, h2, aff, affp = _outproj(x2, bz, ydiff.reshape(n, DIFF_WIDTH), o_f.reshape(n, HG_WIDTH),
                               o_b.reshape(n, HG_WIDTH), hg, p["conv_w"][l], p["hgrn_norm"][l],
                               p["w_out"][l], p["norm_ffn"][l], p["router_w"][l], seq, tm)
        x2 = _routed_ffn(x1, h2, aff, affp, p["w_gate"], p["w_up"], p["w_down"], l,
                         p["norm_final"], l == depth - 1)
    return x2.reshape(bsz, seq, D_MODEL)


def _split2_param(w):
    w = w.astype(F32)
    hi = w.astype(BF16)
    lo = (w - hi.astype(F32)).astype(BF16)
    return jnp.concatenate([hi, lo], axis=-1)


def kernel(x_prompt, x_sample, norm_mix, w_in, conv_w, diff_lambda, diff_subln, hgrn_lb,
           hgrn_norm, w_out, norm_ffn, router_w, w_gate, w_up, w_down, norm_final):
    p = dict(
        norm_mix=norm_mix.astype(F32), w_in=w_in.astype(BF16), conv_w=conv_w,
        diff_lambda=diff_lambda, diff_subln=diff_subln, hgrn_lb=hgrn_lb, hgrn_norm=hgrn_norm,
        w_out=w_out.astype(BF16), norm_ffn=norm_ffn,
        router_w=_split2_param(router_w),
        w_gate=w_gate.astype(BF16), w_up=w_up.astype(BF16), w_down=w_down.astype(BF16),
        norm_final=norm_final,
    )
    return _trunk(x_prompt, p), _trunk(x_sample, p)
```

```python
import functools
import math

import jax
import jax.numpy as jnp
from jax import lax
from jax.experimental import pallas as pl
from jax.experimental.pallas import tpu as pltpu

F32 = jnp.float32
BF16 = jnp.bfloat16
I32 = jnp.int32

D_MODEL = 1024
CONV_WIDTH = 256
DIFF_WIDTH = 512
DIFF_HEADS = 4
DIFF_HEAD_DIM = 64
HG_WIDTH = 256
HG_HEADS = 4
HG_HEAD_DIM = 64
HG_CHUNK = 64
D_PROJ = 3584
ROPE_THETA = 10000.0
N_EXPERTS = 16
EC_CAPACITY_FACTOR = 2
NORM_EPS = 1e-6

VMEM_LIMIT_BYTES = 56 * 1024 * 1024
LANES = 128
ROW_ALIGN = 16
GATHER_ROWS = 256
COMBINE_MAIN_ROWS = 64

NT_DIMS = (((1,), (1,)), ((), ()))
Q_SCALE = DIFF_HEAD_DIM ** -0.5 * math.log2(math.e)
SCORE_FLOOR = -1e30


def _params(*sem):
    return pltpu.CompilerParams(dimension_semantics=sem, vmem_limit_bytes=VMEM_LIMIT_BYTES)


def _split3(x):
    hi = x.astype(BF16)
    r1 = x - hi.astype(F32)
    mid = r1.astype(BF16)
    lo = (r1 - mid.astype(F32)).astype(BF16)
    return hi, mid, lo


def _dot(a, b):
    return jnp.dot(a, b, preferred_element_type=F32)


def _dot_nt(a, b):
    return lax.dot_general(a, b, NT_DIMS, preferred_element_type=F32)


def _sigmoid(x):
    return 1.0 / (1.0 + jnp.exp(-x))


def _inproj_kernel(x_ref, g_ref, w_ref, cos_ref, sina_ref, sinb_ref,
                   bz_ref, q_ref, k_ref, v_ref, hg_ref):
    x = x_ref[...]
    ms = jnp.mean(x * x, axis=-1, keepdims=True)
    h = (x * lax.rsqrt(ms + NORM_EPS) * g_ref[...]).astype(BF16)

    def proj(a, b):
        return _dot(h, w_ref[:, a:b])

    cw = CONV_WIDTH
    bz_ref[:, 0:cw] = proj(0, cw).astype(BF16)
    bz_ref[:, cw:2 * cw] = (proj(cw, 2 * cw) * proj(2 * cw, 3 * cw)).astype(BF16)

    reps = DIFF_WIDTH // LANES
    cos = jnp.concatenate([cos_ref[...]] * reps, axis=1)
    sina = jnp.concatenate([sina_ref[...]] * reps, axis=1)
    sinb = jnp.concatenate([sinb_ref[...]] * reps, axis=1)
    half = DIFF_HEAD_DIM // 2

    def rope(t):
        return (t * cos + pltpu.roll(t, DIFF_WIDTH - half, axis=1) * sina
                + pltpu.roll(t, half, axis=1) * sinb)

    o = 3 * cw
    q_ref[...] = (rope(proj(o, o + DIFF_WIDTH)) * Q_SCALE).astype(BF16)
    o += DIFF_WIDTH
    k_ref[...] = rope(proj(o, o + DIFF_WIDTH)).astype(BF16)
    o += DIFF_WIDTH
    v = proj(o, o + DIFF_WIDTH).astype(BF16)
    hd = 2 * DIFF_HEAD_DIM
    ones = jnp.ones((v.shape[0], hd), BF16)
    for head in range(DIFF_HEADS):
        v_ref[:, 2 * head * hd:(2 * head + 1) * hd] = v[:, head * hd:(head + 1) * hd]
        v_ref[:, (2 * head + 1) * hd:(2 * head + 2) * hd] = ones
    o += DIFF_WIDTH
    hg_ref[...] = proj(o, D_PROJ)


def _rope_tables(seq):
    d = DIFF_HEAD_DIM
    inv = 1.0 / (ROPE_THETA ** (jnp.arange(0, d, 2, dtype=F32) / d))
    ang = jnp.arange(seq, dtype=F32)[:, None] * inv[None, :]
    ang = jnp.concatenate([ang, ang, ang, ang], axis=-1)
    cos, sin = jnp.cos(ang), jnp.sin(ang)
    first = (jnp.arange(LANES) % d) < (d // 2)
    sina = jnp.where(first[None, :], -sin, 0.0)
    sinb = jnp.where(first[None, :], 0.0, sin)
    return cos, sina, sinb


def _inproj(x2, g, w_bf16, tables, seq, tm):
    n = x2.shape[0]
    nblk_s = seq // tm
    cos, sina, sinb = tables
    tab_spec = pl.BlockSpec((tm, LANES), lambda i: (i % nblk_s, 0))
    row = lambda w: pl.BlockSpec((tm, w), lambda i: (i, 0))
    hgw = 5 * HG_WIDTH
    return pl.pallas_call(
        _inproj_kernel,
        grid=(n // tm,),
        in_specs=[row(D_MODEL),
                  pl.BlockSpec((1, D_MODEL), lambda i: (0, 0)),
                  pl.BlockSpec((D_MODEL, D_PROJ), lambda i: (0, 0)),
                  tab_spec, tab_spec, tab_spec],
        out_specs=[row(2 * CONV_WIDTH), row(DIFF_WIDTH), row(DIFF_WIDTH), row(2 * DIFF_WIDTH), row(hgw)],
        out_shape=[jax.ShapeDtypeStruct((n, 2 * CONV_WIDTH), BF16),
                   jax.ShapeDtypeStruct((n, DIFF_WIDTH), BF16),
                   jax.ShapeDtypeStruct((n, DIFF_WIDTH), BF16),
                   jax.ShapeDtypeStruct((n, 2 * DIFF_WIDTH), BF16),
                   jax.ShapeDtypeStruct((n, hgw), F32)],
        compiler_params=_params("parallel"),
        name="inproj",
    )(x2, g.reshape(1, D_MODEL), w_bf16, cos, sina, sinb)


def _attn_kernel(q_ref, k_ref, v_ref, lam_ref, g_ref, o_ref, *, tk, lam_init):
    q = q_ref[0]
    tq = q.shape[0]
    seq = k_ref.shape[1]
    nblk = seq // tk
    d = DIFF_HEAD_DIM
    hd = 2 * d
    lane = lax.broadcasted_iota(I32, q.shape, 1)
    zero = jnp.zeros_like(q)
    qs = (jnp.where(lane < d, q, zero), jnp.where(lane >= d, q, zero))

    def scores(j, h):
        s = _dot_nt(qs[h], k_ref[0, j * tk:(j + 1) * tk, :])
        return s, jnp.max(s, axis=1, keepdims=True)

    def consume(j, sc, st):
        m, acc = st
        s, bm = sc
        mn = jnp.maximum(m, bm)
        p = jnp.exp2(s - mn).astype(BF16)
        return mn, jnp.exp2(m - mn) * acc + _dot(p, v_ref[0, j * tk:(j + 1) * tk, :])

    init = (jnp.full((tq, 1), SCORE_FLOOR, F32), jnp.zeros((tq, 2 * hd), F32))
    state = [init, init]
    sc = [scores(0, 0), scores(0, 1)]
    for j in range(nblk):
        nxt = [None, None]
        for h in range(2):
            if j + 1 < nblk:
                nxt[h] = scores(j + 1, h)
            state[h] = consume(j, sc[h], state[h])
        sc = nxt
    (_, a1), (_, a2) = state

    lp = lam_ref[...]
    lam = (jnp.exp(jnp.sum(lp[0:1] * lp[1:2], axis=1, keepdims=True))
           - jnp.exp(jnp.sum(lp[2:3] * lp[3:4], axis=1, keepdims=True)) + lam_init)
    o = a1[:, :hd] / a1[:, hd:] - lam * (a2[:, :hd] / a2[:, hd:])
    ms = jnp.mean(o * o, axis=-1, keepdims=True)
    y = o * lax.rsqrt(ms + NORM_EPS) * g_ref[...] * (1.0 - lam_init)
    o_ref[0] = y.astype(BF16)


def _attention(q3, k3, v3, lam_params, subln_g, layer, tq, tk):
    bsz, seq, _ = q3.shape
    lam_init = 0.8 - 0.6 * math.exp(-0.3 * layer)
    hd = 2 * DIFF_HEAD_DIM
    return pl.pallas_call(
        functools.partial(_attn_kernel, tk=tk, lam_init=lam_init),
        grid=(bsz, DIFF_HEADS, seq // tq),
        in_specs=[pl.BlockSpec((1, tq, hd), lambda b, h, i: (b, i, h)),
                  pl.BlockSpec((1, seq, hd), lambda b, h, i: (b, 0, h)),
                  pl.BlockSpec((1, seq, 2 * hd), lambda b, h, i: (b, 0, h)),
                  pl.BlockSpec((4, DIFF_HEAD_DIM), lambda b, h, i: (0, 0)),
                  pl.BlockSpec((1, hd), lambda b, h, i: (0, 0))],
        out_specs=pl.BlockSpec((1, tq, hd), lambda b, h, i: (b, i, h)),
        out_shape=jax.ShapeDtypeStruct((bsz, seq, DIFF_WIDTH), BF16),
        compiler_params=_params("parallel", "parallel", "parallel"),
        name="attn",
    )(q3, k3, v3, lam_params.astype(F32), subln_g.reshape(1, hd).astype(F32))


def _hgrn_kernel(qf_ref, ff_ref, if_ref, qb_ref, fb_ref, ib_ref, lbp_ref,
                 of_ref, ob_ref, stf_ref, stb_ref, *, layer):
    c_len = HG_CHUNK
    w = HG_WIDTH
    hd = HG_HEAD_DIM
    t_len = qf_ref.shape[1]
    nchunk = t_len // c_len

    @pl.when(pl.program_id(1) == 0)
    def _():
        stf_ref[...] = jnp.zeros_like(stf_ref)
        stb_ref[...] = jnp.zeros_like(stb_ref)

    lbp = lbp_ref[...]
    e = jnp.exp(lbp - jnp.max(lbp, axis=0, keepdims=True))
    sm = e / jnp.sum(e, axis=0, keepdims=True)
    lb = jnp.zeros((1, 2 * w), F32)
    for l in range(1, layer + 1):
        lb = lb + sm[l:l + 1]
    lb_f, lb_b = lb[:, :w], lb[:, w:]

    r = lax.broadcasted_iota(I32, (c_len, c_len), 0)
    c = lax.broadcasted_iota(I32, (c_len, c_len), 1)
    low_m = jnp.where(r >= c, 1.0, 0.0).astype(BF16)
    upp_m = jnp.where(r <= c, 1.0, 0.0).astype(BF16)
    rq = lax.broadcasted_iota(I32, (c_len, HG_HEADS * c_len), 0)
    ck = lax.broadcasted_iota(I32, (c_len, HG_HEADS * c_len), 1) % c_len
    low = rq >= ck
    upp = rq <= ck
    br = lax.broadcasted_iota(I32, (w, w), 0) // hd
    bc = lax.broadcasted_iota(I32, (w, w), 1) // hd
    same_head = br == bc
    mid = c_len // 2

    def bmm(a, b):
        return jnp.einsum('nik,nkj->nij', a, b, preferred_element_type=F32)

    def bmm_nt(a, b):
        return jnp.einsum('nik,njk->nij', a, b, preferred_element_type=F32)

    def scan_block(xq, xf, xi, lbd, tri_m, tri, last, order, st_ref):
        c3 = lambda a: a.reshape(nchunk, c_len, w)
        qh = c3(xq * _sigmoid(xq) * (hd ** -0.5))
        f = lbd + (1.0 - lbd) * _sigmoid(xf)
        kk = c3(1.0 - f)
        tri_b = jnp.broadcast_to(tri_m[None], (nchunk, c_len, c_len))
        g_hi, g_mid, g_lo = _split3(c3(jnp.log(f)))
        G = bmm(tri_b, g_hi) + bmm(tri_b, g_mid) + bmm(tri_b, g_lo)
        g_ref_row = G[:, mid:mid + 1, :]
        g_last = G[:, last:last + 1, :]
        qt = (qh * jnp.exp(G - g_ref_row)).astype(BF16)
        kt = (kk * jnp.exp(g_ref_row - G)).astype(BF16)
        q_in = (qh * jnp.exp(G)).astype(BF16)
        k_out = (kk * jnp.exp(g_last - G)).astype(BF16)
        vb = c3(xi).astype(BF16)
        rep = lambda a: jnp.concatenate([a] * HG_HEADS, axis=1)
        zb = jnp.zeros((nchunk, w, w), BF16)
        a = bmm_nt(qt, jnp.where(same_head[None], rep(kt), zb))
        a = jnp.where(tri[None], a, 0.0).astype(BF16)
        o = bmm(a, jnp.where(same_head[None], rep(vb), zb))
        upd = jnp.einsum('ncv,nck->nvk', vb, k_out, preferred_element_type=F32)
        upd = jnp.where(same_head[None], upd, 0.0)
        decay = jnp.exp(g_last)
        st = st_ref[...]
        inter = [None] * nchunk
        for ci in order:
            inter[ci] = _dot_nt(q_in[ci], st.astype(BF16))
            st = st * decay[ci] + upd[ci]
        st_ref[...] = st
        return o.reshape(nchunk * c_len, w) + jnp.concatenate(inter, axis=0)

    fwd_order = list(range(nchunk))
    of_ref[0] = scan_block(qf_ref[0], ff_ref[0], if_ref[0], lb_f, low_m, low, c_len - 1,
                           fwd_order, stf_ref)
    ob_ref[0] = scan_block(qb_ref[0], fb_ref[0], ib_ref[0], lb_b, upp_m, upp, 0,
                           fwd_order[::-1], stb_ref)


def _hgrn(hg3, hgrn_lb, layer, t_len):
    bsz, seq, _ = hg3.shape
    nt = seq // t_len
    w = HG_WIDTH
    assert HG_CHUNK == HG_HEAD_DIM
    fwd = lambda col: pl.BlockSpec((1, t_len, w), lambda b, j: (b, j, col))
    bwd = lambda col: pl.BlockSpec((1, t_len, w), lambda b, j: (b, nt - 1 - j, col))
    depth = hgrn_lb.shape[0]
    return pl.pallas_call(
        functools.partial(_hgrn_kernel, layer=layer),
        grid=(bsz, nt),
        in_specs=[fwd(0), fwd(1), fwd(3), bwd(0), bwd(2), bwd(3),
                  pl.BlockSpec((depth, 2 * w), lambda b, j: (0, 0))],
        out_specs=[pl.BlockSpec((1, t_len, w), lambda b, j: (b, j, 0)),
                   pl.BlockSpec((1, t_len, w), lambda b, j: (b, nt - 1 - j, 0))],
        out_shape=[jax.ShapeDtypeStruct((bsz, seq, w), F32),
                   jax.ShapeDtypeStruct((bsz, seq, w), F32)],
        scratch_shapes=[pltpu.VMEM((w, w), F32), pltpu.VMEM((w, w), F32)],
        compiler_params=_params("parallel", "arbitrary"),
        name="hgrn",
    )(hg3, hg3, hg3, hg3, hg3, hg3, hgrn_lb.astype(F32))


def _outproj_kernel(x_ref, bz_ref, zp_ref, zn_ref, yd_ref, of_ref, ob_ref, hgate_ref,
                    cw_ref, hn_ref, wo_ref, nf_ref, rw_ref,
                    x1_ref, h2_ref, aff_ref, affp_ref, *, blocks_per_seq):
    tm = x_ref.shape[0]
    cwid = CONV_WIDTH
    i = pl.program_id(0)
    pos = i % blocks_per_seq
    halo = zp_ref.shape[0]

    b = bz_ref[:, 0:cwid].astype(F32)
    z = bz_ref[:, cwid:2 * cwid].astype(F32)
    zprev = zp_ref[:, cwid:2 * cwid].astype(F32)[halo - 1:halo]
    znext = zn_ref[:, cwid:2 * cwid].astype(F32)[0:1]
    zprev = jnp.where(pos == 0, 0.0, zprev)
    znext = jnp.where(pos == blocks_per_seq - 1, 0.0, znext)
    row = lax.broadcasted_iota(I32, (tm, cwid), 0)
    z_up = jnp.where(row == 0, zprev, pltpu.roll(z, 1, axis=0))
    z_dn = jnp.where(row == tm - 1, znext, pltpu.roll(z, tm - 1, axis=0))
    cw = cw_ref[...]
    y_conv = b * (cw[0:1] * z_up + cw[1:2] * z + cw[2:3] * z_dn)

    o = of_ref[...] + ob_ref[...]
    w = HG_WIDTH
    br = lax.broadcasted_iota(I32, (w, w), 0) // HG_HEAD_DIM
    bc = lax.broadcasted_iota(I32, (w, w), 1) // HG_HEAD_DIM
    pool = jnp.where(br == bc, 1.0, 0.0).astype(BF16)
    sq = o * o
    sq_hi = sq.astype(BF16)
    sq_lo = (sq - sq_hi.astype(F32)).astype(BF16)
    ms = (_dot(sq_hi, pool) + _dot(sq_lo, pool)) * (1.0 / HG_HEAD_DIM)
    gt = hgate_ref[...]
    y_h = o * lax.rsqrt(ms + NORM_EPS) * hn_ref[...] * (gt * _sigmoid(gt))

    mixed = (_dot(y_conv.astype(BF16), wo_ref[0:cwid, :])
             + _dot(yd_ref[...], wo_ref[cwid:cwid + DIFF_WIDTH, :])
             + _dot(y_h.astype(BF16), wo_ref[cwid + DIFF_WIDTH:, :]))
    x1 = x_ref[...] + mixed
    x1_ref[...] = x1

    ms2 = jnp.mean(x1 * x1, axis=-1, keepdims=True)
    h2 = x1 * lax.rsqrt(ms2 + NORM_EPS) * nf_ref[...]
    h_hi = h2.astype(BF16)
    h2_ref[...] = h_hi

    h_lo = (h2 - h_hi.astype(F32)).astype(BF16)
    rw = rw_ref[...]
    d_hi = _dot(h_hi, rw)
    d_lo = _dot(h_lo, rw)
    ne = N_EXPERTS
    logits = d_hi[:, :ne] + (d_hi[:, ne:] + d_lo[:, :ne]) + d_lo[:, ne:]
    ex = jnp.exp(logits - jnp.max(logits, axis=-1, keepdims=True))
    aff = ex / jnp.sum(ex, axis=-1, keepdims=True)
    aff_ref[...] = aff
    a_hi = aff.astype(BF16)
    a_lo = (aff - a_hi.astype(F32)).astype(BF16)
    pad = jnp.zeros((aff.shape[0], LANES - 2 * ne), BF16)
    affp_ref[...] = jnp.concatenate([a_hi, a_lo, pad], axis=1)


def _outproj(x2, bz, ydiff, of2, ob2, hg2, conv_w, hgrn_norm, wo_bf16, norm_ffn, rw3, seq, tm):
    n = x2.shape[0]
    halo = 16
    hb = tm // halo
    nhalo = n // halo
    row = lambda w: pl.BlockSpec((tm, w), lambda i: (i, 0))
    full = lambda a, b: pl.BlockSpec((a, b), lambda i: (0, 0))
    hn = jnp.tile(hgrn_norm.astype(F32), HG_HEADS).reshape(1, HG_WIDTH)
    return pl.pallas_call(
        functools.partial(_outproj_kernel, blocks_per_seq=seq // tm),
        grid=(n // tm,),
        in_specs=[row(D_MODEL), row(2 * CONV_WIDTH),
                  pl.BlockSpec((halo, 2 * CONV_WIDTH), lambda i: (jnp.maximum(i * hb - 1, 0), 0)),
                  pl.BlockSpec((halo, 2 * CONV_WIDTH), lambda i: (jnp.minimum((i + 1) * hb, nhalo - 1), 0)),
                  row(DIFF_WIDTH), row(HG_WIDTH), row(HG_WIDTH),
                  pl.BlockSpec((tm, HG_WIDTH), lambda i: (i, 4)),
                  full(3, CONV_WIDTH), full(1, HG_WIDTH), full(D_MODEL, D_MODEL), full(1, D_MODEL),
                  full(D_MODEL, 2 * N_EXPERTS)],
        out_specs=[row(D_MODEL), row(D_MODEL), row(N_EXPERTS), row(LANES)],
        out_shape=[jax.ShapeDtypeStruct((n, D_MODEL), F32),
                   jax.ShapeDtypeStruct((n, D_MODEL), BF16),
                   jax.ShapeDtypeStruct((n, N_EXPERTS), F32),
                   jax.ShapeDtypeStruct((n, LANES), BF16)],
        compiler_params=_params("parallel"),
        name="outproj",
    )(x2, bz, bz, bz, ydiff, of2, ob2, hg2, conv_w.astype(F32), hn, wo_bf16,
      norm_ffn.reshape(1, D_MODEL).astype(F32), rw3)


def _select_kernel(aff_ref, rankm_ref, rankx_ref, pre_ref, tot_ref, *, cap):
    nblk = aff_ref.shape[0]
    ne = N_EXPERTS
    bits = pltpu.bitcast(aff_ref[...], I32)

    def count(mask):
        part = jnp.sum(jnp.where(mask, 1.0, 0.0), axis=0)
        return jnp.sum(part, axis=1, keepdims=True)

    value_bits = 31

    def search(i, tau):
        cand = tau | jnp.left_shift(jnp.int32(1), value_bits - 1 - i)
        return jnp.where(count(bits >= cand[None]) >= cap, cand, tau)

    tau = lax.fori_loop(0, value_bits, search, jnp.zeros((ne, 1), I32))[None]

    r = lax.broadcasted_iota(I32, (LANES, LANES), 0)
    c = lax.broadcasted_iota(I32, (LANES, LANES), 1)
    incl = jnp.where(r <= c, 1.0, 0.0).astype(BF16)
    ones = jnp.ones((LANES, LANES), BF16)

    def prefix(mask):
        m = jnp.where(mask, 1.0, 0.0)
        m2 = m.reshape(nblk * ne, LANES).astype(BF16)
        pre_ref[...] = _dot(m2, incl).reshape(nblk, ne, LANES) - m
        tot_ref[...] = _dot(m2, ones).reshape(nblk, ne, LANES)

        def carry(b, run):
            pre_ref[b] = pre_ref[b] + run
            return run + tot_ref[b]

        lax.fori_loop(0, nblk, carry, jnp.zeros((ne, LANES), F32))
        return pre_ref[...]

    above = bits > tau
    tie = bits == tau
    need = cap - count(above)
    sel = above | (tie & (prefix(tie) < need[None]))
    rank = prefix(sel).astype(I32)
    rankx_ref[...] = rank
    rankm_ref[...] = jnp.where(sel, rank, -1)


def _select(aff3, cap):
    nblk = aff3.shape[0]
    shp = (nblk, N_EXPERTS, LANES)
    spec = pl.BlockSpec(shp, lambda i: (0, 0, 0))
    return pl.pallas_call(
        functools.partial(_select_kernel, cap=cap),
        grid=(1,),
        in_specs=[spec],
        out_specs=[spec, spec],
        out_shape=[jax.ShapeDtypeStruct(shp, I32), jax.ShapeDtypeStruct(shp, I32)],
        scratch_shapes=[pltpu.VMEM(shp, F32), pltpu.VMEM(shp, F32)],
        compiler_params=_params("arbitrary"),
        name="select",
    )(aff3)


def _expert_kernel(base_ref, rank_ref, h_ref, g_ref, wg_ref, wu_ref, wd_ref, ye_ref, xe_ref, xg_ref,
                   *, nt, sub, tt, sj, win):
    e = pl.program_id(0)
    s = pl.program_id(1)

    @pl.when(s == 0)
    def _():
        xe_ref[...] = jnp.zeros_like(xe_ref)
        xg_ref[...] = jnp.zeros_like(xg_ref)

    @pl.when(s < nt)
    def _():
        slot = lax.broadcasted_iota(I32, (win, tt), 0)
        for sub_i in range(sub):
            tile = e * (nt * sub + 1) + s * sub + sub_i
            base = base_ref[tile]
            count = base_ref[tile + 1] - base
            wb = (base // ROW_ALIGN) * ROW_ALIGN
            nwin = jnp.where(count > 0, (base - wb + count + win - 1) // win, 0)
            rk = rank_ref[0, sub_i]
            rel = jnp.concatenate([rk[a:a + 1, :] for a in range(tt // LANES)], axis=1) - wb
            rows = slice(sub_i * tt, (sub_i + 1) * tt)

            def fill(w, carry, wb=wb, rel=rel, rows=rows):
                onehot = jnp.where(slot == rel - w * win, 1.0, 0.0).astype(BF16)
                r0 = pl.multiple_of(wb + w * win, ROW_ALIGN)
                got = xe_ref[pl.ds(r0, win), :].astype(F32) + _dot(onehot, h_ref[rows, :])
                xe_ref[pl.ds(r0, win), :] = got.astype(BF16)
                xg_ref[pl.ds(r0, win), :] += _dot(onehot, g_ref[rows, :])
                return carry

            lax.fori_loop(0, nwin, fill, 0)

    @pl.when(s >= nt)
    def _():
        j0 = pl.multiple_of((s - nt) * sj, sj)
        xj = xe_ref[pl.ds(j0, sj), :]
        a = _dot(xj, wg_ref[0])
        u = _dot(xj, wu_ref[0])
        hid = (a * _sigmoid(a) * u).astype(BF16)
        lane = lax.broadcasted_iota(I32, (sj, LANES), 1)
        mine = (lane == e) | (lane == N_EXPERTS + e)
        gate = jnp.sum(jnp.where(mine, xg_ref[pl.ds(j0, sj), :], 0.0), axis=1, keepdims=True)
        ye_ref[0] = (_dot(hid, wd_ref[0]) * gate).astype(BF16)


def _experts(base_flat, rank4, h2, affp, wg, wu, wd, layer, cap, tt, sub, sj):
    n = h2.shape[0]
    nt = n // (tt * sub)
    nj = cap // sj
    win = GATHER_ROWS
    wspec = pl.BlockSpec((pl.Squeezed(), 1, D_MODEL, D_MODEL), lambda e, s, b: (layer, e, 0, 0))
    grid_spec = pltpu.PrefetchScalarGridSpec(
        num_scalar_prefetch=1,
        grid=(N_EXPERTS, nt + nj),
        in_specs=[pl.BlockSpec((1, sub, tt // LANES, LANES),
                               lambda e, s, b: (e, jnp.minimum(s, nt - 1), 0, 0)),
                  pl.BlockSpec((sub * tt, D_MODEL), lambda e, s, b: (jnp.minimum(s, nt - 1), 0)),
                  pl.BlockSpec((sub * tt, LANES), lambda e, s, b: (jnp.minimum(s, nt - 1), 0)),
                  wspec, wspec, wspec],
        out_specs=pl.BlockSpec((1, sj, D_MODEL), lambda e, s, b: (e, jnp.maximum(s - nt, 0), 0)),
        scratch_shapes=[pltpu.VMEM((cap + win + ROW_ALIGN, D_MODEL), BF16),
                        pltpu.VMEM((cap + win + ROW_ALIGN, LANES), F32)],
    )
    return pl.pallas_call(
        functools.partial(_expert_kernel, nt=nt, sub=sub, tt=tt, sj=sj, win=win),
        grid_spec=grid_spec,
        out_shape=jax.ShapeDtypeStruct((N_EXPERTS, cap, D_MODEL), BF16),
        compiler_params=_params("arbitrary", "arbitrary"),
        name="experts",
    )(base_flat, rank4, h2, affp, wg, wu, wd)


def _main_start(base, cap, w1):
    return pl.multiple_of(jnp.minimum((base // 16) * 16, cap - w1), 16)


def _over_start(base, nxt, cap, w1, w2):
    wb = _main_start(base, cap, w1)
    need = nxt - wb > w1
    return need, pl.multiple_of(jnp.where(need, jnp.minimum(wb + w1, cap - w2), 0), 16)


def _combine_kernel(base_ref, x_ref, rank_ref, nrm_ref, ye_hbm, out_ref,
                    main_buf, over_buf, main_sem, over_sem, *, nt, cap, w1, w2, final):
    ne = N_EXPERTS
    t = pl.program_id(0)
    tt = x_ref.shape[0]
    slot = t % 2

    def windows(tile, e):
        base = base_ref[e * (nt + 1) + tile]
        nxt = base_ref[e * (nt + 1) + tile + 1]
        need, wo = _over_start(base, nxt, cap, w1, w2)
        return _main_start(base, cap, w1), need, wo

    def main_copy(e, wb, sl):
        return pltpu.make_async_copy(ye_hbm.at[e, pl.ds(wb, w1), :],
                                     main_buf.at[sl, pl.ds(e * w1, w1), :], main_sem.at[sl])

    def over_copy(e, wo, sl):
        return pltpu.make_async_copy(ye_hbm.at[e, pl.ds(wo, w2), :], over_buf.at[sl, e],
                                     over_sem.at[sl, e])

    def fetch(tile, sl):
        for e in range(ne):
            wb, need, wo = windows(tile, e)
            main_copy(e, wb, sl).start()

            @pl.when(need)
            def _(e=e, wo=wo):
                over_copy(e, wo, sl).start()

    @pl.when(t == 0)
    def _():
        fetch(0, 0)

    @pl.when(t + 1 < nt)
    def _():
        fetch(t + 1, 1 - slot)

    rank = rank_ref[...]
    wins = [windows(t, e) for e in range(ne)]

    col = lax.broadcasted_iota(I32, (1, ne), 1)
    wb_row = jnp.zeros((1, ne), I32)
    for e in range(ne):
        wb_row = jnp.where(col == e, wins[e][0], wb_row)
    rel = rank - wb_row
    adj = jnp.where((rank >= 0) & (rel < w1), rel, -1)
    per_vreg = LANES // w1
    lane = lax.broadcasted_iota(I32, (tt, LANES), 1)
    onehots = []
    for i in range(ne // per_vreg):
        target = jnp.full((tt, LANES), -1, I32)
        for a in range(per_vreg):
            e = i * per_vreg + a
            inside = (lane >= a * w1) & (lane < (a + 1) * w1)
            target = jnp.where(inside, adj[:, e:e + 1] + a * w1, target)
        onehots.append(jnp.where(lane == target, 1.0, 0.0).astype(BF16))
    for e in range(ne):
        main_copy(e, wins[e][0], slot).wait()
    out_ref[...] = x_ref[...] + _dot(jnp.concatenate(onehots, axis=1), main_buf[slot])

    for e in range(ne):
        wb, need, wo = wins[e]

        @pl.when(need)
        def _(e=e, wb=wb, wo=wo):
            over_copy(e, wo, slot).wait()
            lane2 = lax.broadcasted_iota(I32, (tt, w2), 1)
            r = rank_ref[:, e:e + 1]
            match = (lane2 == r - wo) & (r >= wb + w1)
            onehot = jnp.where(match, 1.0, 0.0).astype(BF16)
            out_ref[...] += _dot(onehot, over_buf[slot, e])

    if final:
        o = out_ref[...]
        ms = jnp.mean(o * o, axis=-1, keepdims=True)
        out_ref[...] = o * lax.rsqrt(ms + NORM_EPS) * nrm_ref[...]


def _combine(base_flat, x1, rank_tok, ye, nrm, cap, tt, final):
    n = x1.shape[0]
    nt = n // tt
    w1 = COMBINE_MAIN_ROWS
    w2 = tt + 16 - w1
    assert cap >= max(w1, w2) and cap % 16 == 0 and LANES % w1 == 0
    row = lambda w: pl.BlockSpec((tt, w), lambda t, b: (t, 0))
    grid_spec = pltpu.PrefetchScalarGridSpec(
        num_scalar_prefetch=1,
        grid=(nt,),
        in_specs=[row(D_MODEL), row(N_EXPERTS),
                  pl.BlockSpec((1, D_MODEL), lambda t, b: (0, 0)),
                  pl.BlockSpec(memory_space=pl.ANY)],
        out_specs=row(D_MODEL),
        scratch_shapes=[pltpu.VMEM((2, N_EXPERTS * w1, D_MODEL), BF16),
                        pltpu.VMEM((2, N_EXPERTS, w2, D_MODEL), BF16),
                        pltpu.SemaphoreType.DMA((2,)),
                        pltpu.SemaphoreType.DMA((2, N_EXPERTS))],
    )
    return pl.pallas_call(
        functools.partial(_combine_kernel, nt=nt, cap=cap, w1=w1, w2=w2, final=final),
        grid_spec=grid_spec,
        out_shape=jax.ShapeDtypeStruct((n, D_MODEL), F32),
        compiler_params=_params("arbitrary"),
        name="combine",
    )(base_flat, x1, rank_tok, nrm.reshape(1, D_MODEL).astype(F32), ye)


def _tile(n, want):
    t = min(n, want)
    assert n % t == 0, (n, want)
    return t


def _routed_ffn(x1, h2, aff, affp, wg, wu, wd, layer, nrm, final):
    n = x1.shape[0]
    cap = (EC_CAPACITY_FACTOR * n) // N_EXPERTS
    nblk = n // LANES
    tt_e = _tile(n, 1024)
    sj = _tile(cap, 512)
    tt_c = _tile(n, 256)

    aff3 = aff.reshape(nblk, LANES, N_EXPERTS).transpose(0, 2, 1)
    rankm3, rankx3 = _select(aff3, cap)
    rank_tok = rankm3.transpose(0, 2, 1).reshape(n, N_EXPERTS)
    rank4 = rankm3.transpose(1, 0, 2).reshape(N_EXPERTS, n // tt_e, tt_e // LANES, LANES)
    base128 = rankx3[:, :, 0].T
    last = jnp.full((N_EXPERTS, 1), cap, I32)
    base_e = jnp.concatenate([base128[:, ::tt_e // LANES], last], axis=1).reshape(-1)
    base_c = jnp.concatenate([base128[:, ::tt_c // LANES], last], axis=1).reshape(-1)

    sub = max(c for c in (4, 2, 1) if (n // tt_e) % c == 0)
    ye = _experts(base_e, rank4, h2, affp, wg, wu, wd, layer, cap, tt_e, sub, sj)
    return _combine(base_c, x1, rank_tok, ye, nrm, cap, tt_c, final)


def _trunk(x, p):
    bsz, seq, _ = x.shape
    n = bsz * seq
    depth = p["w_in"].shape[0]
    tm = _tile(seq, 512)
    tq = _tile(seq, 512)
    tk = _tile(seq, 2048)
    t_hg = _tile(seq, 512)
    tables = _rope_tables(seq)

    x2 = x.reshape(n, D_MODEL)
    for l in range(depth):
        bz, q, k, v, hg = _inproj(x2, p["norm_mix"][l], p["w_in"][l], tables, seq, tm)
        to3 = lambda a: a.reshape(bsz, seq, a.shape[-1])
        ydiff = _attention(to3(q), to3(k), to3(v), p["diff_lambda"][l], p["diff_subln"][l], l, tq, tk)
        o_f, o_b = _hgrn(to3(hg), p["hgrn_lb"], l, t_hg)
        x1, h2, aff, affp = _outproj(x2, bz, ydiff.reshape(n, DIFF_WIDTH), o_f.reshape(n, HG_WIDTH),
                               o_b.reshape(n, HG_WIDTH), hg, p["conv_w"][l], p["hgrn_norm"][l],
                               p["w_out"][l], p["norm_ffn"][l], p["router_w"][l], seq, tm)
        x2 = _routed_ffn(x1, h2, aff, affp, p["w_gate"], p["w_up"], p["w_down"], l,
                         p["norm_final"], l == depth - 1)
    return x2.reshape(bsz, seq, D_MODEL)


def _split2_param(w):
    w = w.astype(F32)
    hi = w.astype(BF16)
    lo = (w - hi.astype(F32)).astype(BF16)
    return jnp.concatenate([hi, lo], axis=-1)


def kernel(x_prompt, x_sample, norm_mix, w_in, conv_w, diff_lambda, diff_subln, hgrn_lb,
           hgrn_norm, w_out, norm_ffn, router_w, w_gate, w_up, w_down, norm_final):
    p = dict(
        norm_mix=norm_mix.astype(F32), w_in=w_in.astype(BF16), conv_w=conv_w,
        diff_lambda=diff_lambda, diff_subln=diff_subln, hgrn_lb=hgrn_lb, hgrn_norm=hgrn_norm,
        w_out=w_out.astype(BF16), norm_ffn=norm_ffn,
        router_w=_split2_param(router_w),
        w_gate=w_gate.astype(BF16), w_up=w_up.astype(BF16), w_down=w_down.astype(BF16),
        norm_final=norm_final,
    )
    return _trunk(x_prompt, p), _trunk(x_sample, p)
```

```python
import functools
import math

import jax
import jax.numpy as jnp
from jax import lax
from jax.experimental import pallas as pl
from jax.experimental.pallas import tpu as pltpu

F32 = jnp.float32
BF16 = jnp.bfloat16
I32 = jnp.int32

D_MODEL = 1024
CONV_WIDTH = 256
DIFF_WIDTH = 512
DIFF_HEADS = 4
DIFF_HEAD_DIM = 64
HG_WIDTH = 256
HG_HEADS = 4
HG_HEAD_DIM = 64
HG_CHUNK = 64
D_PROJ = 3584
ROPE_THETA = 10000.0
N_EXPERTS = 16
EC_CAPACITY_FACTOR = 2
NORM_EPS = 1e-6

VMEM_LIMIT_BYTES = 56 * 1024 * 1024
LANES = 128
ROW_ALIGN = 16
GATHER_ROWS = 256
COMBINE_MAIN_ROWS = 64

NT_DIMS = (((1,), (1,)), ((), ()))
Q_SCALE = DIFF_HEAD_DIM ** -0.5 * math.log2(math.e)
SCORE_FLOOR = -1e30


def _params(*sem):
    return pltpu.CompilerParams(dimension_semantics=sem, vmem_limit_bytes=VMEM_LIMIT_BYTES)


def _split3(x):
    hi = x.astype(BF16)
    r1 = x - hi.astype(F32)
    mid = r1.astype(BF16)
    lo = (r1 - mid.astype(F32)).astype(BF16)
    return hi, mid, lo


def _dot(a, b):
    return jnp.dot(a, b, preferred_element_type=F32)


def _dot_nt(a, b):
    return lax.dot_general(a, b, NT_DIMS, preferred_element_type=F32)


def _sigmoid(x):
    return 1.0 / (1.0 + jnp.exp(-x))


def _inproj_kernel(x_ref, g_ref, w_ref, cos_ref, sina_ref, sinb_ref,
                   bz_ref, q_ref, k_ref, v_ref, hg_ref):
    x = x_ref[...]
    ms = jnp.mean(x * x, axis=-1, keepdims=True)
    h = (x * lax.rsqrt(ms + NORM_EPS) * g_ref[...]).astype(BF16)

    def proj(a, b):
        return _dot(h, w_ref[:, a:b])

    cw = CONV_WIDTH
    bz_ref[:, 0:cw] = proj(0, cw).astype(BF16)
    bz_ref[:, cw:2 * cw] = (proj(cw, 2 * cw) * proj(2 * cw, 3 * cw)).astype(BF16)

    reps = DIFF_WIDTH // LANES
    cos = jnp.concatenate([cos_ref[...]] * reps, axis=1)
    sina = jnp.concatenate([sina_ref[...]] * reps, axis=1)
    sinb = jnp.concatenate([sinb_ref[...]] * reps, axis=1)
    half = DIFF_HEAD_DIM // 2

    def rope(t):
        return (t * cos + pltpu.roll(t, DIFF_WIDTH - half, axis=1) * sina
                + pltpu.roll(t, half, axis=1) * sinb)

    o = 3 * cw
    q_ref[...] = (rope(proj(o, o + DIFF_WIDTH)) * Q_SCALE).astype(BF16)
    o += DIFF_WIDTH
    k_ref[...] = rope(proj(o, o + DIFF_WIDTH)).astype(BF16)
    o += DIFF_WIDTH
    v = proj(o, o + DIFF_WIDTH).astype(BF16)
    hd = 2 * DIFF_HEAD_DIM
    ones = jnp.ones((v.shape[0], hd), BF16)
    for head in range(DIFF_HEADS):
        v_ref[:, 2 * head * hd:(2 * head + 1) * hd] = v[:, head * hd:(head + 1) * hd]
        v_ref[:, (2 * head + 1) * hd:(2 * head + 2) * hd] = ones
    o += DIFF_WIDTH
    hg_ref[...] = proj(o, D_PROJ)


def _rope_tables(seq):
    d = DIFF_HEAD_DIM
    inv = 1.0 / (ROPE_THETA ** (jnp.arange(0, d, 2, dtype=F32) / d))
    ang = jnp.arange(seq, dtype=F32)[:, None] * inv[None, :]
    ang = jnp.concatenate([ang, ang, ang, ang], axis=-1)
    cos, sin = jnp.cos(ang), jnp.sin(ang)
    first = (jnp.arange(LANES) % d) < (d // 2)
    sina = jnp.where(first[None, :], -sin, 0.0)
    sinb = jnp.where(first[None, :], 0.0, sin)
    return cos, sina, sinb


def _inproj(x2, g, w_bf16, tables, seq, tm):
    n = x2.shape[0]
    nblk_s = seq // tm
    cos, sina, sinb = tables
    tab_spec = pl.BlockSpec((tm, LANES), lambda i: (i % nblk_s, 0))
    row = lambda w: pl.BlockSpec((tm, w), lambda i: (i, 0))
    hgw = 5 * HG_WIDTH
    return pl.pallas_call(
        _inproj_kernel,
        grid=(n // tm,),
        in_specs=[row(D_MODEL),
                  pl.BlockSpec((1, D_MODEL), lambda i: (0, 0)),
                  pl.BlockSpec((D_MODEL, D_PROJ), lambda i: (0, 0)),
                  tab_spec, tab_spec, tab_spec],
        out_specs=[row(2 * CONV_WIDTH), row(DIFF_WIDTH), row(DIFF_WIDTH), row(2 * DIFF_WIDTH), row(hgw)],
        out_shape=[jax.ShapeDtypeStruct((n, 2 * CONV_WIDTH), BF16),
                   jax.ShapeDtypeStruct((n, DIFF_WIDTH), BF16),
                   jax.ShapeDtypeStruct((n, DIFF_WIDTH), BF16),
                   jax.ShapeDtypeStruct((n, 2 * DIFF_WIDTH), BF16),
                   jax.ShapeDtypeStruct((n, hgw), F32)],
        compiler_params=_params("parallel"),
        name="inproj",
    )(x2, g.reshape(1, D_MODEL), w_bf16, cos, sina, sinb)


def _attn_kernel(q_ref, k_ref, v_ref, lam_ref, g_ref, o_ref, *, tk, lam_init):
    q = q_ref[0]
    tq = q.shape[0]
    seq = k_ref.shape[1]
    nblk = seq // tk
    d = DIFF_HEAD_DIM
    hd = 2 * d
    lane = lax.broadcasted_iota(I32, q.shape, 1)
    zero = jnp.zeros_like(q)
    qs = (jnp.where(lane < d, q, zero), jnp.where(lane >= d, q, zero))

    def scores(j, h):
        s = _dot_nt(qs[h], k_ref[0, j * tk:(j + 1) * tk, :])
        return s, jnp.max(s, axis=1, keepdims=True)

    def consume(j, sc, st):
        m, acc = st
        s, bm = sc
        mn = jnp.maximum(m, bm)
        p = jnp.exp2(s - mn).astype(BF16)
        return mn, jnp.exp2(m - mn) * acc + _dot(p, v_ref[0, j * tk:(j + 1) * tk, :])

    init = (jnp.full((tq, 1), SCORE_FLOOR, F32), jnp.zeros((tq, 2 * hd), F32))
    state = [init, init]
    sc = [scores(0, 0), scores(0, 1)]
    for j in range(nblk):
        nxt = [None, None]
        for h in range(2):
            if j + 1 < nblk:
                nxt[h] = scores(j + 1, h)
            state[h] = consume(j, sc[h], state[h])
        sc = nxt
    (_, a1), (_, a2) = state

    lp = lam_ref[...]
    lam = (jnp.exp(jnp.sum(lp[0:1] * lp[1:2], axis=1, keepdims=True))
           - jnp.exp(jnp.sum(lp[2:3] * lp[3:4], axis=1, keepdims=True)) + lam_init)
    o = a1[:, :hd] / a1[:, hd:] - lam * (a2[:, :hd] / a2[:, hd:])
    ms = jnp.mean(o * o, axis=-1, keepdims=True)
    y = o * lax.rsqrt(ms + NORM_EPS) * g_ref[...] * (1.0 - lam_init)
    o_ref[0] = y.astype(BF16)


def _attention(q3, k3, v3, lam_params, subln_g, layer, tq, tk):
    bsz, seq, _ = q3.shape
    lam_init = 0.8 - 0.6 * math.exp(-0.3 * layer)
    hd = 2 * DIFF_HEAD_DIM
    return pl.pallas_call(
        functools.partial(_attn_kernel, tk=tk, lam_init=lam_init),
        grid=(bsz, DIFF_HEADS, seq // tq),
        in_specs=[pl.BlockSpec((1, tq, hd), lambda b, h, i: (b, i, h)),
                  pl.BlockSpec((1, seq, hd), lambda b, h, i: (b, 0, h)),
                  pl.BlockSpec((1, seq, 2 * hd), lambda b, h, i: (b, 0, h)),
                  pl.BlockSpec((4, DIFF_HEAD_DIM), lambda b, h, i: (0, 0)),
                  pl.BlockSpec((1, hd), lambda b, h, i: (0, 0))],
        out_specs=pl.BlockSpec((1, tq, hd), lambda b, h, i: (b, i, h)),
        out_shape=jax.ShapeDtypeStruct((bsz, seq, DIFF_WIDTH), BF16),
        compiler_params=_params("parallel", "parallel", "parallel"),
        name="attn",
    )(q3, k3, v3, lam_params.astype(F32), subln_g.reshape(1, hd).astype(F32))


def _hgrn_kernel(qf_ref, ff_ref, if_ref, qb_ref, fb_ref, ib_ref, lbp_ref,
                 of_ref, ob_ref, stf_ref, stb_ref, *, layer):
    c_len = HG_CHUNK
    w = HG_WIDTH
    hd = HG_HEAD_DIM
    t_len = qf_ref.shape[1]
    nchunk = t_len // c_len

    @pl.when(pl.program_id(1) == 0)
    def _():
        stf_ref[...] = jnp.zeros_like(stf_ref)
        stb_ref[...] = jnp.zeros_like(stb_ref)

    lbp = lbp_ref[...]
    e = jnp.exp(lbp - jnp.max(lbp, axis=0, keepdims=True))
    sm = e / jnp.sum(e, axis=0, keepdims=True)
    lb = jnp.zeros((1, 2 * w), F32)
    for l in range(1, layer + 1):
        lb = lb + sm[l:l + 1]
    lb_f, lb_b = lb[:, :w], lb[:, w:]

    r = lax.broadcasted_iota(I32, (c_len, c_len), 0)
    c = lax.broadcasted_iota(I32, (c_len, c_len), 1)
    low_m = jnp.where(r >= c, 1.0, 0.0).astype(BF16)
    upp_m = jnp.where(r <= c, 1.0, 0.0).astype(BF16)
    rq = lax.broadcasted_iota(I32, (c_len, HG_HEADS * c_len), 0)
    ck = lax.broadcasted_iota(I32, (c_len, HG_HEADS * c_len), 1) % c_len
    low = rq >= ck
    upp = rq <= ck
    br = lax.broadcasted_iota(I32, (w, w), 0) // hd
    bc = lax.broadcasted_iota(I32, (w, w), 1) // hd
    same_head = br == bc
    mid = c_len // 2

    def bmm(a, b):
        return jnp.einsum('nik,nkj->nij', a, b, preferred_element_type=F32)

    def bmm_nt(a, b):
        return jnp.einsum('nik,njk->nij', a, b, preferred_element_type=F32)

    def scan_block(xq, xf, xi, lbd, tri_m, tri, last, order, st_ref):
        c3 = lambda a: a.reshape(nchunk, c_len, w)
        qh = c3(xq * _sigmoid(xq) * (hd ** -0.5))
        f = lbd + (1.0 - lbd) * _sigmoid(xf)
        kk = c3(1.0 - f)
        tri_b = jnp.broadcast_to(tri_m[None], (nchunk, c_len, c_len))
        g_hi, g_mid, g_lo = _split3(c3(jnp.log(f)))
        G = bmm(tri_b, g_hi) + bmm(tri_b, g_mid) + bmm(tri_b, g_lo)
        g_ref_row = G[:, mid:mid + 1, :]
        g_last = G[:, last:last + 1, :]
        qt = (qh * jnp.exp(G - g_ref_row)).astype(BF16)
        kt = (kk * jnp.exp(g_ref_row - G)).astype(BF16)
        q_in = (qh * jnp.exp(G)).astype(BF16)
        k_out = (kk * jnp.exp(g_last - G)).astype(BF16)
        vb = c3(xi).astype(BF16)
        rep = lambda a: jnp.concatenate([a] * HG_HEADS, axis=1)
        zb = jnp.zeros((nchunk, w, w), BF16)
        a = bmm_nt(qt, jnp.where(same_head[None], rep(kt), zb))
        a = jnp.where(tri[None], a, 0.0).astype(BF16)
        o = bmm(a, jnp.where(same_head[None], rep(vb), zb))
        upd = jnp.einsum('ncv,nck->nvk', vb, k_out, preferred_element_type=F32)
        upd = jnp.where(same_head[None], upd, 0.0)
        decay = jnp.exp(g_last)
        st = st_ref[...]
        inter = [None] * nchunk
        for ci in order:
            inter[ci] = _dot_nt(q_in[ci], st.astype(BF16))
            st = st * decay[ci] + upd[ci]
        st_ref[...] = st
        return o.reshape(nchunk * c_len, w) + jnp.concatenate(inter, axis=0)

    fwd_order = list(range(nchunk))
    of_ref[0] = scan_block(qf_ref[0], ff_ref[0], if_ref[0], lb_f, low_m, low, c_len - 1,
                           fwd_order, stf_ref)
    ob_ref[0] = scan_block(qb_ref[0], fb_ref[0], ib_ref[0], lb_b, upp_m, upp, 0,
                           fwd_order[::-1], stb_ref)


def _hgrn(hg3, hgrn_lb, layer, t_len):
    bsz, seq, _ = hg3.shape
    nt = seq // t_len
    w = HG_WIDTH
    assert HG_CHUNK == HG_HEAD_DIM
    fwd = lambda col: pl.BlockSpec((1, t_len, w), lambda b, j: (b, j, col))
    bwd = lambda col: pl.BlockSpec((1, t_len, w), lambda b, j: (b, nt - 1 - j, col))
    depth = hgrn_lb.shape[0]
    return pl.pallas_call(
        functools.partial(_hgrn_kernel, layer=layer),
        grid=(bsz, nt),
        in_specs=[fwd(0), fwd(1), fwd(3), bwd(0), bwd(2), bwd(3),
                  pl.BlockSpec((depth, 2 * w), lambda b, j: (0, 0))],
        out_specs=[pl.BlockSpec((1, t_len, w), lambda b, j: (b, j, 0)),
                   pl.BlockSpec((1, t_len, w), lambda b, j: (b, nt - 1 - j, 0))],
        out_shape=[jax.ShapeDtypeStruct((bsz, seq, w), F32),
                   jax.ShapeDtypeStruct((bsz, seq, w), F32)],
        scratch_shapes=[pltpu.VMEM((w, w), F32), pltpu.VMEM((w, w), F32)],
        compiler_params=_params("parallel", "arbitrary"),
        name="hgrn",
    )(hg3, hg3, hg3, hg3, hg3, hg3, hgrn_lb.astype(F32))


def _outproj_kernel(x_ref, bz_ref, zp_ref, zn_ref, yd_ref, of_ref, ob_ref, hgate_ref,
                    cw_ref, hn_ref, wo_ref, nf_ref, rw_ref,
                    x1_ref, h2_ref, aff_ref, affp_ref, *, blocks_per_seq):
    tm = x_ref.shape[0]
    cwid = CONV_WIDTH
    i = pl.program_id(0)
    pos = i % blocks_per_seq
    halo = zp_ref.shape[0]

    b = bz_ref[:, 0:cwid].astype(F32)
    z = bz_ref[:, cwid:2 * cwid].astype(F32)
    zprev = zp_ref[:, cwid:2 * cwid].astype(F32)[halo - 1:halo]
    znext = zn_ref[:, cwid:2 * cwid].astype(F32)[0:1]
    zprev = jnp.where(pos == 0, 0.0, zprev)
    znext = jnp.where(pos == blocks_per_seq - 1, 0.0, znext)
    row = lax.broadcasted_iota(I32, (tm, cwid), 0)
    z_up = jnp.where(row == 0, zprev, pltpu.roll(z, 1, axis=0))
    z_dn = jnp.where(row == tm - 1, znext, pltpu.roll(z, tm - 1, axis=0))
    cw = cw_ref[...]
    y_conv = b * (cw[0:1] * z_up + cw[1:2] * z + cw[2:3] * z_dn)

    o = of_ref[...] + ob_ref[...]
    w = HG_WIDTH
    br = lax.broadcasted_iota(I32, (w, w), 0) // HG_HEAD_DIM
    bc = lax.broadcasted_iota(I32, (w, w), 1) // HG_HEAD_DIM
    pool = jnp.where(br == bc, 1.0, 0.0).astype(BF16)
    sq = o * o
    sq_hi = sq.astype(BF16)
    sq_lo = (sq - sq_hi.astype(F32)).astype(BF16)
    ms = (_dot(sq_hi, pool) + _dot(sq_lo, pool)) * (1.0 / HG_HEAD_DIM)
    gt = hgate_ref[...]
    y_h = o * lax.rsqrt(ms + NORM_EPS) * hn_ref[...] * (gt * _sigmoid(gt))

    mixed = (_dot(y_conv.astype(BF16), wo_ref[0:cwid, :])
             + _dot(yd_ref[...], wo_ref[cwid:cwid + DIFF_WIDTH, :])
             + _dot(y_h.astype(BF16), wo_ref[cwid + DIFF_WIDTH:, :]))
    x1 = x_ref[...] + mixed
    x1_ref[...] = x1

    ms2 = jnp.mean(x1 * x1, axis=-1, keepdims=True)
    h2 = x1 * lax.rsqrt(ms2 + NORM_EPS) * nf_ref[...]
    h_hi = h2.astype(BF16)
    h2_ref[...] = h_hi

    h_lo = (h2 - h_hi.astype(F32)).astype(BF16)
    rw = rw_ref[...]
    d_hi = _dot(h_hi, rw)
    d_lo = _dot(h_lo, rw)
    ne = N_EXPERTS
    logits = d_hi[:, :ne] + (d_hi[:, ne:] + d_lo[:, :ne]) + d_lo[:, ne:]
    ex = jnp.exp(logits - jnp.max(logits, axis=-1, keepdims=True))
    aff = ex / jnp.sum(ex, axis=-1, keepdims=True)
    aff_ref[...] = aff
    a_hi = aff.astype(BF16)
    a_lo = (aff - a_hi.astype(F32)).astype(BF16)
    pad = jnp.zeros((aff.shape[0], LANES - 2 * ne), BF16)
    affp_ref[...] = jnp.concatenate([a_hi, a_lo, pad], axis=1)


def _outproj(x2, bz, ydiff, of2, ob2, hg2, conv_w, hgrn_norm, wo_bf16, norm_ffn, rw3, seq, tm):
    n = x2.shape[0]
    halo = 16
    hb = tm // halo
    nhalo = n // halo
    row = lambda w: pl.BlockSpec((tm, w), lambda i: (i, 0))
    full = lambda a, b: pl.BlockSpec((a, b), lambda i: (0, 0))
    hn = jnp.tile(hgrn_norm.astype(F32), HG_HEADS).reshape(1, HG_WIDTH)
    return pl.pallas_call(
        functools.partial(_outproj_kernel, blocks_per_seq=seq // tm),
        grid=(n // tm,),
        in_specs=[row(D_MODEL), row(2 * CONV_WIDTH),
                  pl.BlockSpec((halo, 2 * CONV_WIDTH), lambda i: (jnp.maximum(i * hb - 1, 0), 0)),
                  pl.BlockSpec((halo, 2 * CONV_WIDTH), lambda i: (jnp.minimum((i + 1) * hb, nhalo - 1), 0)),
                  row(DIFF_WIDTH), row(HG_WIDTH), row(HG_WIDTH),
                  pl.BlockSpec((tm, HG_WIDTH), lambda i: (i, 4)),
                  full(3, CONV_WIDTH), full(1, HG_WIDTH), full(D_MODEL, D_MODEL), full(1, D_MODEL),
                  full(D_MODEL, 2 * N_EXPERTS)],
        out_specs=[row(D_MODEL), row(D_MODEL), row(N_EXPERTS), row(LANES)],
        out_shape=[jax.ShapeDtypeStruct((n, D_MODEL), F32),
                   jax.ShapeDtypeStruct((n, D_MODEL), BF16),
                   jax.ShapeDtypeStruct((n, N_EXPERTS), F32),
                   jax.ShapeDtypeStruct((n, LANES), BF16)],
        compiler_params=_params("parallel"),
        name="outproj",
    )(x2, bz, bz, bz, ydiff, of2, ob2, hg2, conv_w.astype(F32), hn, wo_bf16,
      norm_ffn.reshape(1, D_MODEL).astype(F32), rw3)


def _select_kernel(aff_ref, rankm_ref, rankx_ref, pre_ref, tot_ref, *, cap):
    nblk = aff_ref.shape[0]
    ne = N_EXPERTS
    bits = pltpu.bitcast(aff_ref[...], I32)

    def count(mask):
        part = jnp.sum(jnp.where(mask, 1.0, 0.0), axis=0)
        return jnp.sum(part, axis=1, keepdims=True)

    value_bits = 31

    def search(i, tau):
        cand = tau | jnp.left_shift(jnp.int32(1), value_bits - 1 - i)
        return jnp.where(count(bits >= cand[None]) >= cap, cand, tau)

    tau = lax.fori_loop(0, value_bits, search, jnp.zeros((ne, 1), I32))[None]

    r = lax.broadcasted_iota(I32, (LANES, LANES), 0)
    c = lax.broadcasted_iota(I32, (LANES, LANES), 1)
    incl = jnp.where(r <= c, 1.0, 0.0).astype(BF16)
    ones = jnp.ones((LANES, LANES), BF16)

    def prefix(mask):
        m = jnp.where(mask, 1.0, 0.0)
        m2 = m.reshape(nblk * ne, LANES).astype(BF16)
        pre_ref[...] = _dot(m2, incl).reshape(nblk, ne, LANES) - m
        tot_ref[...] = _dot(m2, ones).reshape(nblk, ne, LANES)

        def carry(b, run):
            pre_ref[b] = pre_ref[b] + run
            return run + tot_ref[b]

        lax.fori_loop(0, nblk, carry, jnp.zeros((ne, LANES), F32))
        return pre_ref[...]

    above = bits > tau
    tie = bits == tau
    need = cap - count(above)
    sel = above | (tie & (prefix(tie) < need[None]))
    rank = prefix(sel).astype(I32)
    rankx_ref[...] = rank
    rankm_ref[...] = jnp.where(sel, rank, -1)


def _select(aff3, cap):
    nblk = aff3.shape[0]
    shp = (nblk, N_EXPERTS, LANES)
    spec = pl.BlockSpec(shp, lambda i: (0, 0, 0))
    return pl.pallas_call(
        functools.partial(_select_kernel, cap=cap),
        grid=(1,),
        in_specs=[spec],
        out_specs=[spec, spec],
        out_shape=[jax.ShapeDtypeStruct(shp, I32), jax.ShapeDtypeStruct(shp, I32)],
        scratch_shapes=[pltpu.VMEM(shp, F32), pltpu.VMEM(shp, F32)],
        compiler_params=_params("arbitrary"),
        name="select",
    )(aff3)


def _expert_kernel(base_ref, rank_ref, h_ref, g_ref, wg_ref, wu_ref, wd_ref, ye_ref, xe_ref, xg_ref,
                   *, nt, sub, tt, sj, win):
    e = pl.program_id(0)
    s = pl.program_id(1)

    @pl.when(s == 0)
    def _():
        xe_ref[...] = jnp.zeros_like(xe_ref)
        xg_ref[...] = jnp.zeros_like(xg_ref)

    @pl.when(s < nt)
    def _():
        slot = lax.broadcasted_iota(I32, (win, tt), 0)
        for sub_i in range(sub):
            tile = e * (nt * sub + 1) + s * sub + sub_i
            base = base_ref[tile]
            count = base_ref[tile + 1] - base
            wb = (base // ROW_ALIGN) * ROW_ALIGN
            nwin = jnp.where(count > 0, (base - wb + count + win - 1) // win, 0)
            rk = rank_ref[0, sub_i]
            rel = jnp.concatenate([rk[a:a + 1, :] for a in range(tt // LANES)], axis=1) - wb
            rows = slice(sub_i * tt, (sub_i + 1) * tt)

            def fill(w, carry, wb=wb, rel=rel, rows=rows):
                onehot = jnp.where(slot == rel - w * win, 1.0, 0.0).astype(BF16)
                r0 = pl.multiple_of(wb + w * win, ROW_ALIGN)
                got = xe_ref[pl.ds(r0, win), :].astype(F32) + _dot(onehot, h_ref[rows, :])
                xe_ref[pl.ds(r0, win), :] = got.astype(BF16)
                xg_ref[pl.ds(r0, win), :] += _dot(onehot, g_ref[rows, :])
                return carry

            lax.fori_loop(0, nwin, fill, 0)

    @pl.when(s >= nt)
    def _():
        j0 = pl.multiple_of((s - nt) * sj, sj)
        xj = xe_ref[pl.ds(j0, sj), :]
        a = _dot(xj, wg_ref[0])
        u = _dot(xj, wu_ref[0])
        hid = (a * _sigmoid(a) * u).astype(BF16)
        lane = lax.broadcasted_iota(I32, (sj, LANES), 1)
        mine = (lane == e) | (lane == N_EXPERTS + e)
        gate = jnp.sum(jnp.where(mine, xg_ref[pl.ds(j0, sj), :], 0.0), axis=1, keepdims=True)
        ye_ref[0] = (_dot(hid, wd_ref[0]) * gate).astype(BF16)


def _experts(base_flat, rank4, h2, affp, wg, wu, wd, layer, cap, tt, sub, sj):
    n = h2.shape[0]
    nt = n // (tt * sub)
    nj = cap // sj
    win = GATHER_ROWS
    wspec = pl.BlockSpec((pl.Squeezed(), 1, D_MODEL, D_MODEL), lambda e, s, b: (layer, e, 0, 0))
    grid_spec = pltpu.PrefetchScalarGridSpec(
        num_scalar_prefetch=1,
        grid=(N_EXPERTS, nt + nj),
        in_specs=[pl.BlockSpec((1, sub, tt // LANES, LANES),
                               lambda e, s, b: (e, jnp.minimum(s, nt - 1), 0, 0)),
                  pl.BlockSpec((sub * tt, D_MODEL), lambda e, s, b: (jnp.minimum(s, nt - 1), 0)),
                  pl.BlockSpec((sub * tt, LANES), lambda e, s, b: (jnp.minimum(s, nt - 1), 0)),
                  wspec, wspec, wspec],
        out_specs=pl.BlockSpec((1, sj, D_MODEL), lambda e, s, b: (e, jnp.maximum(s - nt, 0), 0)),
        scratch_shapes=[pltpu.VMEM((cap + win + ROW_ALIGN, D_MODEL), BF16),
                        pltpu.VMEM((cap + win + ROW_ALIGN, LANES), F32)],
    )
    return pl.pallas_call(
        functools.partial(_expert_kernel, nt=nt, sub=sub, tt=tt, sj=sj, win=win),
        grid_spec=grid_spec,
        out_shape=jax.ShapeDtypeStruct((N_EXPERTS, cap, D_MODEL), BF16),
        compiler_params=_params("arbitrary", "arbitrary"),
        name="experts",
    )(base_flat, rank4, h2, affp, wg, wu, wd)


def _main_start(base, cap, w1):
    return pl.multiple_of(jnp.minimum((base // 16) * 16, cap - w1), 16)


def _over_start(base, nxt, cap, w1, w2):
    wb = _main_start(base, cap, w1)
    need = nxt - wb > w1
    return need, pl.multiple_of(jnp.where(need, jnp.minimum(wb + w1, cap - w2), 0), 16)


def _combine_kernel(base_ref, x_ref, rank_ref, nrm_ref, ye_hbm, out_ref,
                    main_buf, over_buf, main_sem, over_sem, *, nt, cap, w1, w2, final):
    ne = N_EXPERTS
    t = pl.program_id(0)
    tt = x_ref.shape[0]
    slot = t % 2

    def windows(tile, e):
        base = base_ref[e * (nt + 1) + tile]
        nxt = base_ref[e * (nt + 1) + tile + 1]
        need, wo = _over_start(base, nxt, cap, w1, w2)
        return _main_start(base, cap, w1), need, wo

    def main_copy(e, wb, sl):
        return pltpu.make_async_copy(ye_hbm.at[e, pl.ds(wb, w1), :],
                                     main_buf.at[sl, pl.ds(e * w1, w1), :], main_sem.at[sl])

    def over_copy(e, wo, sl):
        return pltpu.make_async_copy(ye_hbm.at[e, pl.ds(wo, w2), :], over_buf.at[sl, e],
                                     over_sem.at[sl, e])

    def fetch(tile, sl):
        for e in range(ne):
            wb, need, wo = windows(tile, e)
            main_copy(e, wb, sl).start()

            @pl.when(need)
            def _(e=e, wo=wo):
                over_copy(e, wo, sl).start()

    @pl.when(t == 0)
    def _():
        fetch(0, 0)

    @pl.when(t + 1 < nt)
    def _():
        fetch(t + 1, 1 - slot)

    rank = rank_ref[...]
    wins = [windows(t, e) for e in range(ne)]

    col = lax.broadcasted_iota(I32, (1, ne), 1)
    wb_row = jnp.zeros((1, ne), I32)
    for e in range(ne):
        wb_row = jnp.where(col == e, wins[e][0], wb_row)
    rel = rank - wb_row
    adj = jnp.where((rank >= 0) & (rel < w1), rel, -1)
    per_vreg = LANES // w1
    lane = lax.broadcasted_iota(I32, (tt, LANES), 1)
    onehots = []
    for i in range(ne // per_vreg):
        target = jnp.full((tt, LANES), -1, I32)
        for a in range(per_vreg):
            e = i * per_vreg + a
            inside = (lane >= a * w1) & (lane < (a + 1) * w1)
            target = jnp.where(inside, adj[:, e:e + 1] + a * w1, target)
        onehots.append(jnp.where(lane == target, 1.0, 0.0).astype(BF16))
    for e in range(ne):
        main_copy(e, wins[e][0], slot).wait()
    out_ref[...] = x_ref[...] + _dot(jnp.concatenate(onehots, axis=1), main_buf[slot])

    for e in range(ne):
        wb, need, wo = wins[e]

        @pl.when(need)
        def _(e=e, wb=wb, wo=wo):
            over_copy(e, wo, slot).wait()
            lane2 = lax.broadcasted_iota(I32, (tt, w2), 1)
            r = rank_ref[:, e:e + 1]
            match = (lane2 == r - wo) & (r >= wb + w1)
            onehot = jnp.where(match, 1.0, 0.0).astype(BF16)
            out_ref[...] += _dot(onehot, over_buf[slot, e])

    if final:
        o = out_ref[...]
        ms = jnp.mean(o * o, axis=-1, keepdims=True)
        out_ref[...] = o * lax.rsqrt(ms + NORM_EPS) * nrm_ref[...]


def _combine(base_flat, x1, rank_tok, ye, nrm, cap, tt, final):
    n = x1.shape[0]
    nt = n // tt
    w1 = COMBINE_MAIN_ROWS
    w2 = tt + 16 - w1
    assert cap >= max(w1, w2) and cap % 16 == 0 and LANES % w1 == 0
    row = lambda w: pl.BlockSpec((tt, w), lambda t, b: (t, 0))
    grid_spec = pltpu.PrefetchScalarGridSpec(
        num_scalar_prefetch=1,
        grid=(nt,),
        in_specs=[row(D_MODEL), row(N_EXPERTS),
                  pl.BlockSpec((1, D_MODEL), lambda t, b: (0, 0)),
                  pl.BlockSpec(memory_space=pl.ANY)],
        out_specs=row(D_MODEL),
        scratch_shapes=[pltpu.VMEM((2, N_EXPERTS * w1, D_MODEL), BF16),
                        pltpu.VMEM((2, N_EXPERTS, w2, D_MODEL), BF16),
                        pltpu.SemaphoreType.DMA((2,)),
                        pltpu.SemaphoreType.DMA((2, N_EXPERTS))],
    )
    return pl.pallas_call(
        functools.partial(_combine_kernel, nt=nt, cap=cap, w1=w1, w2=w2, final=final),
        grid_spec=grid_spec,
        out_shape=jax.ShapeDtypeStruct((n, D_MODEL), F32),
        compiler_params=_params("arbitrary"),
        name="combine",
    )(base_flat, x1, rank_tok, nrm.reshape(1, D_MODEL).astype(F32), ye)


def _tile(n, want):
    t = min(n, want)
    assert n % t == 0, (n, want)
    return t


def _routed_ffn(x1, h2, aff, affp, wg, wu, wd, layer, nrm, final):
    n = x1.shape[0]
    cap = (EC_CAPACITY_FACTOR * n) // N_EXPERTS
    nblk = n // LANES
    tt_e = _tile(n, 1024)
    sj = _tile(cap, 512)
    tt_c = _tile(n, 256)

    aff3 = aff.reshape(nblk, LANES, N_EXPERTS).transpose(0, 2, 1)
    rankm3, rankx3 = _select(aff3, cap)
    rank_tok = rankm3.transpose(0, 2, 1).reshape(n, N_EXPERTS)
    rank4 = rankm3.transpose(1, 0, 2).reshape(N_EXPERTS, n // tt_e, tt_e // LANES, LANES)
    base128 = rankx3[:, :, 0].T
    last = jnp.full((N_EXPERTS, 1), cap, I32)
    base_e = jnp.concatenate([base128[:, ::tt_e // LANES], last], axis=1).reshape(-1)
    base_c = jnp.concatenate([base128[:, ::tt_c // LANES], last], axis=1).reshape(-1)

    sub = max(c for c in (4, 2, 1) if (n // tt_e) % c == 0)
    ye = _experts(base_e, rank4, h2, affp, wg, wu, wd, layer, cap, tt_e, sub, sj)
    return _combine(base_c, x1, rank_tok, ye, nrm, cap, tt_c, final)


def _trunk(x, p):
    bsz, seq, _ = x.shape
    n = bsz * seq
    depth = p["w_in"].shape[0]
    tm = _tile(seq, 512)
    tq = _tile(seq, 512)
    tk = _tile(seq, 2048)
    t_hg = _tile(seq, 1024)
    tables = _rope_tables(seq)

    x2 = x.reshape(n, D_MODEL)
    for l in range(depth):
        bz, q, k, v, hg = _inproj(x2, p["norm_mix"][l], p["w_in"][l], tables, seq, tm)
        to3 = lambda a: a.reshape(bsz, seq, a.shape[-1])
        ydiff = _attention(to3(q), to3(k), to3(v), p["diff_lambda"][l], p["diff_subln"][l], l, tq, tk)
        o_f, o_b = _hgrn(to3(hg), p["hgrn_lb"], l, t_hg)
        x1, h2, aff, affp = _outproj(x2, bz, ydiff.reshape(n, DIFF_WIDTH), o_f.reshape(n, HG_WIDTH),
                               o_b.reshape(n, HG_WIDTH), hg, p["conv_w"][l], p["hgrn_norm"][l],
                               p["w_out"][l], p["norm_ffn"][l], p["router_w"][l], seq, tm)
        x2 = _routed_ffn(x1, h2, aff, affp, p["w_gate"], p["w_up"], p["w_down"], l,
                         p["norm_final"], l == depth - 1)
    return x2.reshape(bsz, seq, D_MODEL)


def _split2_param(w):
    w = w.astype(F32)
    hi = w.astype(BF16)
    lo = (w - hi.astype(F32)).astype(BF16)
    return jnp.concatenate([hi, lo], axis=-1)


def kernel(x_prompt, x_sample, norm_mix, w_in, conv_w, diff_lambda, diff_subln, hgrn_lb,
           hgrn_norm, w_out, norm_ffn, router_w, w_gate, w_up, w_down, norm_final):
    p = dict(
        norm_mix=norm_mix.astype(F32), w_in=w_in.astype(BF16), conv_w=conv_w,
        diff_lambda=diff_lambda, diff_subln=diff_subln, hgrn_lb=hgrn_lb, hgrn_norm=hgrn_norm,
        w_out=w_out.astype(BF16), norm_ffn=norm_ffn,
        router_w=_split2_param(router_w),
        w_gate=w_gate.astype(BF16), w_up=w_up.astype(BF16), w_down=w_down.astype(BF16),
        norm_final=norm_final,
    )
    return _trunk(x_prompt, p), _trunk(x_sample, p)
```
